```python
import jax, jax.numpy as jnp
from jax import lax
import numpy as np

D_MODEL = 1024
BATCH = 8
SEQ = 2048
DEPTH = 1
DEC_BATCH = 8
DEC_SEQ = 32
PAST_LEN = 1024

CHUNK = 64
D_MIX = D_MODEL
HGRN_WIDTH = D_MIX // 2
HGRN_KDIM = 128
HGRN_HEADS = HGRN_WIDTH // HGRN_KDIM
HGRN_VDIM = HGRN_WIDTH // HGRN_HEADS
CONV_CH = D_MIX - HGRN_WIDTH
CONV_WIDTH = 31
IN_COLS = 4 * HGRN_WIDTH + 2 * CONV_CH
N_EXPERTS = 32
TOP_K = 4
D_FF = D_MODEL
SWIGLU_LIMIT = 7.0
SWIGLU_ALPHA = 1.702
PLE_DIM = 256
MOE_BLOCK = 128
EPS = 1e-6

kernel_name = 'hybrid_hgrn2_conformer_moe_stream_step'


def _rmsnorm(x, g):
    x32 = x.astype(jnp.float32)
    y = x32 * lax.rsqrt(jnp.mean(x32 * x32, axis=-1, keepdims=True) + EPS)
    return y.astype(x.dtype) * g


def _layernorm(x, g, b):
    x32 = x.astype(jnp.float32)
    mu = jnp.mean(x32, axis=-1, keepdims=True)
    var = jnp.mean(jnp.square(x32 - mu), axis=-1, keepdims=True)
    return ((x32 - mu) * lax.rsqrt(var + EPS)).astype(x.dtype) * g + b


def _hgrn2_scan(q, logf, k, v, s0, chunk):
    bsz, t, h, _ = q.shape
    dv = v.shape[-1]
    n = t // chunk

    def to_chunks(a):
        return a.reshape(bsz, n, chunk, h, a.shape[-1]).transpose(1, 0, 3, 2, 4)

    causal = jnp.tril(jnp.ones((chunk, chunk), dtype=bool))[:, :, None]

    def step(s, xs):
        qc, gc, kc, vc = xs
        b = jnp.cumsum(gc, axis=2)
        decay = jnp.exp(jnp.where(causal, b[:, :, :, None, :] - b[:, :, None, :, :], -jnp.inf))
        scores = jnp.einsum('bhtk,bhsk,bhtsk->bhts', qc, kc, decay)
        o = jnp.einsum('bhts,bhsv->bhtv', scores, vc) + jnp.einsum('bhtk,bhkv->bhtv', qc * jnp.exp(b), s)
        b_last = b[:, :, -1:, :]
        s_new = jnp.exp(b_last[:, :, 0, :])[..., None] * s + jnp.einsum('bhsk,bhsv->bhkv', kc * jnp.exp(b_last - b), vc)
        return s_new, o

    s_fin, o = lax.scan(step, s0, (to_chunks(q), to_chunks(logf), to_chunks(k), to_chunks(v)))
    return o.transpose(1, 0, 3, 2, 4).reshape(bsz, t, h, dv), s_fin


def _moe(x2d, w_router, b_router, w_gate_up, b_gate_up, w_down, b_down):
    n_tok, d = x2d.shape
    logits = (x2d @ w_router + b_router).astype(jnp.float32)
    top_val, top_idx = lax.top_k(logits, TOP_K)
    gates = jax.nn.softmax(top_val, axis=-1).astype(x2d.dtype)
    n_assign = n_tok * TOP_K
    flat_e = top_idx.reshape(n_assign)
    order = jnp.argsort(flat_e)
    sorted_e = flat_e[order]
    tok = order // TOP_K
    counts = jnp.bincount(flat_e, length=N_EXPERTS)
    padded = ((counts + MOE_BLOCK - 1) // MOE_BLOCK) * MOE_BLOCK
    pad_end = jnp.cumsum(padded)
    pad_start = pad_end - padded
    grp_start = jnp.cumsum(counts) - counts
    dest = pad_start[sorted_e] + (jnp.arange(n_assign) - grp_start[sorted_e])
    n_blocks = -(-n_assign // MOE_BLOCK) + N_EXPERTS
    rows = n_blocks * MOE_BLOCK
    xs = jnp.zeros((rows, d), x2d.dtype).at[dest].set(x2d[tok])
    block_e = jnp.minimum(jnp.searchsorted(pad_end, jnp.arange(n_blocks) * MOE_BLOCK, side='right'), N_EXPERTS - 1)

    def expert_block(args):
        xb, e = args
        gu = xb @ w_gate_up[e] + b_gate_up[e]
        gate = jnp.minimum(gu[:, :D_FF], SWIGLU_LIMIT)
        up = jnp.clip(gu[:, D_FF:], -SWIGLU_LIMIT, SWIGLU_LIMIT)
        hid = (up + 1.0) * (gate * jax.nn.sigmoid(SWIGLU_ALPHA * gate))
        return hid @ w_down[e] + b_down[e]

    ys = lax.map(expert_block, (xs.reshape(n_blocks, MOE_BLOCK, d), block_e)).reshape(rows, d)
    w_sorted = gates.reshape(n_assign)[order]
    return jnp.zeros_like(x2d).at[tok].add(w_sorted[:, None] * ys[dest])


def _layer(x, pe, s0, conv_buf, chunk, lb, norm_mix, w_in, hgrn_norm, conv_w, conv_b, conv_ln_g, conv_ln_b,
           w_out, norm_ffn, w_router, b_router, w_gate_up, b_gate_up, w_down, b_down, w_ple, norm_ple, w_ple_gate):
    bsz, t, _ = x.shape
    h = _rmsnorm(x, norm_mix)
    proj = h @ w_in
    q, fz, inp, g, u_a, u_b = jnp.split(
        proj, [HGRN_WIDTH, 2 * HGRN_WIDTH, 3 * HGRN_WIDTH, 4 * HGRN_WIDTH, 4 * HGRN_WIDTH + CONV_CH], axis=-1)

    def heads(a):
        return a.astype(jnp.float32).reshape(bsz, t, HGRN_HEADS, -1)
    lb_h = lb.reshape(HGRN_HEADS, HGRN_KDIM)
    f = lb_h + (1.0 - lb_h) * jax.nn.sigmoid(heads(fz))
    o, s_new = _hgrn2_scan(heads(q), jnp.log(f), 1.0 - f, heads(inp), s0.astype(jnp.float32), chunk)
    o = _rmsnorm(o, hgrn_norm.reshape(HGRN_HEADS, HGRN_VDIM).astype(jnp.float32))
    o = o.reshape(bsz, t, HGRN_WIDTH).astype(x.dtype) * jax.nn.silu(g)

    u = u_a * jax.nn.sigmoid(u_b)
    upad = jnp.concatenate([conv_buf.astype(u.dtype), u], axis=1)
    c = lax.conv_general_dilated(upad, conv_w[:, None, :], window_strides=(1,), padding='VALID',
                                 dimension_numbers=('NWC', 'WIO', 'NWC'), feature_group_count=CONV_CH) + conv_b
    new_buf = upad[:, -(CONV_WIDTH - 1):]
    c = jax.nn.silu(_layernorm(c, conv_ln_g, conv_ln_b))

    x = x + jnp.concatenate([o, c], axis=-1) @ w_out
    x = x + _moe(_rmsnorm(x, norm_ffn).reshape(bsz * t, D_MODEL), w_router, b_router,
                 w_gate_up, b_gate_up, w_down, b_down).reshape(bsz, t, D_MODEL)
    x = x + jax.nn.sigmoid(_rmsnorm(x, norm_ple) @ w_ple_gate) * (pe @ w_ple)
    return x, s_new.astype(x.dtype), new_buf


def setup_inputs(seed: int = 0) -> dict:
    key = jax.random.key(seed)
    ks = jax.random.split(key, 32)
    f32 = jnp.float32

    def nrm(k, shape, scale):
        return jax.random.normal(k, shape, f32) * scale

    def gain(k, shape):
        return 1.0 + nrm(k, shape, 0.05)

    return {
        'x_prompt': nrm(ks[0], (BATCH, SEQ, D_MODEL), 1.0),
        'x_sample': nrm(ks[1], (DEC_BATCH, DEC_SEQ, D_MODEL), 1.0),
        'state_hgrn': nrm(ks[2], (DEPTH, DEC_BATCH, HGRN_HEADS, HGRN_KDIM, HGRN_VDIM), 0.5),
        'cache_conv': nrm(ks[3], (DEPTH, DEC_BATCH, CONV_WIDTH - 1, CONV_CH), 0.5),
        'p_prompt': nrm(ks[4], (DEPTH, BATCH, SEQ, PLE_DIM), 1.0),
        'p_sample': nrm(ks[5], (DEPTH, DEC_BATCH, DEC_SEQ, PLE_DIM), 1.0),
        'norm_mix': gain(ks[6], (DEPTH, D_MODEL)),
        'w_in': nrm(ks[7], (DEPTH, D_MODEL, IN_COLS), D_MODEL ** -0.5),
        'hgrn_lb_logits': nrm(ks[8], (DEPTH + 1, HGRN_WIDTH), 0.5),
        'hgrn_norm': gain(ks[9], (DEPTH, HGRN_WIDTH)),
        'conv_w': nrm(ks[10], (DEPTH, CONV_WIDTH, CONV_CH), CONV_WIDTH ** -0.5),
        'conv_b': nrm(ks[11], (DEPTH, CONV_CH), 0.01),
        'conv_ln_g': gain(ks[12], (DEPTH, CONV_CH)),
        'conv_ln_b': nrm(ks[13], (DEPTH, CONV_CH), 0.01),
        'w_out': nrm(ks[14], (DEPTH, D_MIX, D_MODEL), D_MIX ** -0.5),
        'norm_ffn': gain(ks[15], (DEPTH, D_MODEL)),
        'w_router': nrm(ks[16], (DEPTH, D_MODEL, N_EXPERTS), D_MODEL ** -0.5),
        'b_router': nrm(ks[17], (DEPTH, N_EXPERTS), 0.01),
        'w_gate_up': nrm(ks[18], (DEPTH, N_EXPERTS, D_MODEL, 2 * D_FF), D_MODEL ** -0.5),
        'b_gate_up': nrm(ks[19], (DEPTH, N_EXPERTS, 2 * D_FF), 0.01),
        'w_down': nrm(ks[20], (DEPTH, N_EXPERTS, D_FF, D_MODEL), D_FF ** -0.5),
        'b_down': nrm(ks[21], (DEPTH, N_EXPERTS, D_MODEL), 0.01),
        'w_ple': nrm(ks[22], (DEPTH, PLE_DIM, D_MODEL), PLE_DIM ** -0.5),
        'norm_ple': gain(ks[23], (DEPTH, D_MODEL)),
        'w_ple_gate': nrm(ks[24], (DEPTH, D_MODEL, D_MODEL), D_MODEL ** -0.5),
        'norm_final': gain(ks[25], (D_MODEL,)),
    }


def reference(x_prompt, x_sample, state_hgrn, cache_conv, p_prompt, p_sample, norm_mix, w_in, hgrn_lb_logits,
              hgrn_norm, conv_w, conv_b, conv_ln_g, conv_ln_b, w_out, norm_ffn, w_router, b_router, w_gate_up,
              b_gate_up, w_down, b_down, w_ple, norm_ple, w_ple_gate, norm_final):
    lb_all = jnp.cumsum(jax.nn.softmax(hgrn_lb_logits.astype(jnp.float32), axis=0), axis=0)
    bp = x_prompt.shape[0]
    s0_prompt = jnp.zeros((bp, HGRN_HEADS, HGRN_KDIM, HGRN_VDIM), jnp.float32)
    buf0_prompt = jnp.zeros((bp, CONV_WIDTH - 1, CONV_CH), x_prompt.dtype)
    sample_len = x_sample.shape[1]
    yp, ys = x_prompt, x_sample
    sp_list, cp_list, ss_list, cs_list = [], [], [], []
    for l in range(DEPTH):
        params = (norm_mix[l], w_in[l], hgrn_norm[l], conv_w[l], conv_b[l], conv_ln_g[l], conv_ln_b[l], w_out[l],
                  norm_ffn[l], w_router[l], b_router[l], w_gate_up[l], b_gate_up[l], w_down[l], b_down[l],
                  w_ple[l], norm_ple[l], w_ple_gate[l])
        yp, sp, cp = _layer(yp, p_prompt[l], s0_prompt, buf0_prompt, CHUNK, lb_all[l], *params)
        ys, ss, cs = _layer(ys, p_sample[l], state_hgrn[l], cache_conv[l], sample_len, lb_all[l], *params)
        sp_list.append(sp)
        cp_list.append(cp)
        ss_list.append(ss)
        cs_list.append(cs)
    y_prompt = _rmsnorm(yp, norm_final)
    y_sample = _rmsnorm(ys, norm_final)
    return (y_prompt, y_sample, jnp.stack(sp_list), jnp.stack(cp_list), jnp.stack(ss_list), jnp.stack(cs_list))
```

```python
import functools

import jax
import jax.numpy as jnp
from jax import lax
from jax.experimental import pallas as pl
from jax.experimental.pallas import tpu as pltpu

D_MODEL = 1024
HGRN_WIDTH = 512
HEAD_DIM = 128
N_HEADS = HGRN_WIDTH // HEAD_DIM
CONV_CH = 512
CONV_WIDTH = 31
CACHE_ROWS = CONV_WIDTH - 1
CACHE_PAD = 32
IN_COLS = 4 * HGRN_WIDTH + 2 * CONV_CH
N_EXPERTS = 32
TOP_K = 4
D_FF = 1024
SWIGLU_LIMIT = 7.0
SWIGLU_ALPHA = 1.702
PLE_DIM = 256
EPS = 1e-6

SUBLANES = 8
LANES = 128
VMEM_LIMIT_BYTES = 56 * 1024 * 1024

F32 = jnp.float32
BF16 = jnp.bfloat16


def _split3(a):
    p1 = a.astype(BF16)
    r1 = a - p1.astype(F32)
    p2 = r1.astype(BF16)
    r2 = r1 - p2.astype(F32)
    return p1, p2, r2.astype(BF16)


def _dot(a, b):
    return jnp.dot(a, b, preferred_element_type=F32)


def _dot_nt(a, b):
    return lax.dot_general(a, b, (((1,), (1,)), ((), ())), preferred_element_type=F32)


def _dot_tn(a, b):
    return lax.dot_general(a, b, (((0,), (0,)), ((), ())), preferred_element_type=F32)


def _rms(x, g):
    return x * lax.rsqrt(jnp.mean(x * x, axis=-1, keepdims=True) + EPS) * g


def _hgrn_chunk(q, kk, f, b, v, st, shift_ref, ones_sq):
    c = q.shape[0]
    row = lax.broadcasted_iota(jnp.int32, (c, c), 0)
    col = lax.broadcasted_iota(jnp.int32, (c, c), 1)
    xor = row ^ col
    diff = row - col

    sc = jnp.zeros((c, c), F32)
    m = c // 2
    while m >= SUBLANES:
        pieces = []
        for blk in range(c // (2 * m)):
            r = blk * 2 * m + m - 1
            pieces.append(jnp.broadcast_to(b[r:r + 1, :], (2 * m, HEAD_DIM)))
        bmid = pieces[0] if len(pieces) == 1 else jnp.concatenate(pieces, axis=0)
        qh = (q * jnp.exp(jnp.minimum(b - bmid, 0.0))).astype(BF16)
        kh = (kk * jnp.exp(jnp.minimum(bmid - b, 0.0))).astype(BF16)
        r_m = _dot_nt(qh, kh)
        sc = jnp.where((xor >= m) & (xor < 2 * m), r_m, sc)
        m //= 2
    sc = jnp.where(diff > 0, sc, 0.0)

    shift_ref[0, SUBLANES:SUBLANES + c, :] = f
    shift_ref[1, SUBLANES:SUBLANES + c, :] = kk
    same_blk = xor < SUBLANES
    decay = None
    for d in range(SUBLANES):
        if d == 0:
            kd = kk
        else:
            f_sh = shift_ref[0, SUBLANES - (d - 1):SUBLANES - (d - 1) + c, :]
            decay = f_sh if decay is None else decay * f_sh
            kd = shift_ref[1, SUBLANES - d:SUBLANES - d + c, :] * decay
        s_d = _dot((q * kd).astype(BF16), ones_sq)
        sc = jnp.where(same_blk & (diff == d), s_d[:, :c], sc)

    o = _dot(sc.astype(BF16), v.astype(BF16))
    o = o + _dot_nt((q * jnp.exp(b)).astype(BF16), st.astype(BF16))
    b_last = b[c - 1:c, :]
    kdec = (kk * jnp.exp(b_last - b)).astype(BF16)
    st_new = st * jnp.exp(b_last) + _dot_tn(v.astype(BF16), kdec)
    return o, st_new


def _mix_body(x_ref, s0_ref, c0_ref, nm_ref, win_ref, lb_ref, hn_ref, cw_ref, cb_ref, lng_ref,
              lnb_ref, wout_ref, nf_ref, wrh_ref, wrl_ref, br_ref,
              x1_ref, xn_ref, idx_ref, gate_ref, sout_ref, cout_ref,
              st_scr, upad_scr, shift_scr, mix_scr, *, tb, chunk):
    j = pl.program_id(1)
    last_j = pl.num_programs(1) - 1

    @pl.when(j == 0)
    def _init():
        for h in range(N_HEADS):
            st_scr[h] = s0_ref[0, h].T
        upad_scr[0:CACHE_PAD, :] = c0_ref[0]
        shift_scr[:, 0:SUBLANES, :] = jnp.zeros((2, SUBLANES, HEAD_DIM), F32)

    x = x_ref[0]
    h_in = _rms(x, nm_ref[...]).astype(BF16)
    proj = _dot(h_in, win_ref[...])

    q_all = proj[:, 0:HGRN_WIDTH]
    fz = proj[:, HGRN_WIDTH:2 * HGRN_WIDTH]
    v_all = proj[:, 2 * HGRN_WIDTH:3 * HGRN_WIDTH]
    g_all = proj[:, 3 * HGRN_WIDTH:4 * HGRN_WIDTH]
    lb = lb_ref[...]
    f_all = lb + (1.0 - lb) * jax.nn.sigmoid(fz)
    lf_all = jnp.log(f_all)
    kk_all = 1.0 - f_all

    row = lax.broadcasted_iota(jnp.int32, (tb, tb), 0)
    col = lax.broadcasted_iota(jnp.int32, (tb, tb), 1)
    tri = ((row >= col) & ((row ^ col) < chunk)).astype(BF16)
    p1, p2, p3 = _split3(lf_all)
    b_all = _dot(tri, p1) + _dot(tri, p2) + _dot(tri, p3)

    ones_sq = jnp.ones((HEAD_DIM, HEAD_DIM), BF16)
    hn = hn_ref[...]
    for h in range(N_HEADS):
        ls = slice(h * HEAD_DIM, (h + 1) * HEAD_DIM)
        st = st_scr[h]
        for ci in range(tb // chunk):
            rs = slice(ci * chunk, (ci + 1) * chunk)
            o, st = _hgrn_chunk(q_all[rs, ls], kk_all[rs, ls], f_all[rs, ls], b_all[rs, ls],
                                v_all[rs, ls], st, shift_scr, ones_sq)
            o = _rms(o, hn[:, ls])
            mix_scr[rs, ls] = (o * jax.nn.silu(g_all[rs, ls])).astype(BF16)
        st_scr[h] = st

    u = proj[:, 4 * HGRN_WIDTH:4 * HGRN_WIDTH + CONV_CH] * jax.nn.sigmoid(
        proj[:, 4 * HGRN_WIDTH + CONV_CH:IN_COLS])
    upad_scr[CACHE_PAD:CACHE_PAD + tb, :] = u
    first = CACHE_PAD - CACHE_ROWS
    acc = jnp.zeros((tb, CONV_CH), F32) + cb_ref[...]
    for tap in range(CONV_WIDTH):
        acc = acc + cw_ref[tap:tap + 1, :] * upad_scr[first + tap:first + tap + tb, :]
    mu = jnp.mean(acc, axis=-1, keepdims=True)
    cen = acc - mu
    var = jnp.mean(cen * cen, axis=-1, keepdims=True)
    cn = cen * lax.rsqrt(var + EPS) * lng_ref[...] + lnb_ref[...]
    mix_scr[:, HGRN_WIDTH:HGRN_WIDTH + CONV_CH] = jax.nn.silu(cn).astype(BF16)
    new_cache = upad_scr[tb:tb + CACHE_PAD, :]
    upad_scr[0:CACHE_PAD, :] = new_cache

    x1 = x + _dot(mix_scr[...], wout_ref[...])
    x1_ref[...] = x1
    xn = _rms(x1, nf_ref[...])
    xn_ref[...] = xn
    xh = xn.astype(BF16)
    xl = (xn - xh.astype(F32)).astype(BF16)
    wrh = wrh_ref[...]
    logits = _dot_nt(wrh, xh) + _dot_nt(wrh, xl) + _dot_nt(wrl_ref[...], xh) + br_ref[...]

    eidx = lax.broadcasted_iota(jnp.int32, (N_EXPERTS, tb), 0).astype(F32)
    vals, ids = [], []
    work = logits
    for _ in range(TOP_K):
        mx = jnp.max(work, axis=0, keepdims=True)
        pick = jnp.min(jnp.where(work == mx, eidx, float(N_EXPERTS)), axis=0, keepdims=True)
        vals.append(mx)
        ids.append(pick)
        work = jnp.where(eidx == pick, -jnp.inf, work)
    exps = [jnp.exp(val - vals[0]) for val in vals]
    inv = 1.0 / (exps[0] + exps[1] + exps[2] + exps[3])
    idx_ref[0] = jnp.concatenate(ids, axis=0).astype(jnp.int32)
    gate_ref[0] = jnp.concatenate([e * inv for e in exps], axis=0)

    @pl.when(j == last_j)
    def _fin():
        for h in range(N_HEADS):
            sout_ref[0, h] = st_scr[h].T
        cout_ref[0] = upad_scr[tb + first:tb + CACHE_PAD, :]


def _mix_stage(x, s0, c0, w, *, tb, chunk, row_offset, n_rows):
    bsz, t, _ = x.shape
    nj = t // tb
    tok_blk = row_offset // tb

    def full(a):
        return pl.BlockSpec(a.shape, lambda b, j: (0,) * a.ndim)

    weights = [w["norm_mix"], w["w_in"], w["lb"], w["hgrn_norm"], w["conv_w"], w["conv_b"], w["ln_g"],
               w["ln_b"], w["w_out"], w["norm_ffn"], w["wr_hi"], w["wr_lo"], w["b_router"]]
    in_specs = [
        pl.BlockSpec((1, tb, D_MODEL), lambda b, j: (b, j, 0)),
        pl.BlockSpec((1, N_HEADS, HEAD_DIM, HEAD_DIM), lambda b, j: (b, 0, 0, 0)),
        pl.BlockSpec((1, CACHE_PAD, CONV_CH), lambda b, j: (b, 0, 0)),
    ] + [full(a) for a in weights]
    out_shape = (
        jax.ShapeDtypeStruct((bsz * t, D_MODEL), F32),
        jax.ShapeDtypeStruct((bsz * t, D_MODEL), F32),
        jax.ShapeDtypeStruct((bsz, TOP_K, t), jnp.int32),
        jax.ShapeDtypeStruct((bsz, TOP_K, t), F32),
        jax.ShapeDtypeStruct((bsz, N_HEADS, HEAD_DIM, HEAD_DIM), F32),
        jax.ShapeDtypeStruct((bsz, CACHE_ROWS, CONV_CH), F32),
    )
    del tok_blk, n_rows
    out_specs = (
        pl.BlockSpec((tb, D_MODEL), lambda b, j: (b * nj + j, 0)),
        pl.BlockSpec((tb, D_MODEL), lambda b, j: (b * nj + j, 0)),
        pl.BlockSpec((1, TOP_K, tb), lambda b, j: (b, 0, j)),
        pl.BlockSpec((1, TOP_K, tb), lambda b, j: (b, 0, j)),
        pl.BlockSpec((1, N_HEADS, HEAD_DIM, HEAD_DIM), lambda b, j: (b, 0, 0, 0)),
        pl.BlockSpec((1, CACHE_ROWS, CONV_CH), lambda b, j: (b, 0, 0)),
    )
    return pl.pallas_call(
        functools.partial(_mix_body, tb=tb, chunk=chunk),
        grid=(bsz, nj),
        in_specs=in_specs,
        out_specs=out_specs,
        out_shape=out_shape,
        scratch_shapes=[
            pltpu.VMEM((N_HEADS, HEAD_DIM, HEAD_DIM), F32),
            pltpu.VMEM((CACHE_PAD + tb, CONV_CH), F32),
            pltpu.VMEM((2, SUBLANES + chunk, HEAD_DIM), F32),
            pltpu.VMEM((tb, D_MODEL), BF16),
        ],
        compiler_params=pltpu.CompilerParams(
            dimension_semantics=("arbitrary", "arbitrary"), vmem_limit_bytes=VMEM_LIMIT_BYTES),
        name="mix",
    )(x, s0, c0, *weights)


def _moe_body(te_ref, tv_ref, src_next_ref, src_cur_ref, dst_ref, xn_hbm, wgu_ref, bgu_ref, wd_ref, bd_ref,
              y_hbm, xbuf, ybuf, wgu_b, wd_b, gsem, ssem, *, tm, nt, n_assign):
    i = pl.program_id(0)
    slot = lax.rem(i, 2)
    used = tv_ref[i] > 0
    next_used = jnp.where(i + 1 < nt, tv_ref[jnp.minimum(i + 1, nt - 1)], 0) > 0

    def row_gather(src_ref, s):
        def body(r, carry):
            tok = src_ref[0, 0, r]
            pltpu.make_async_copy(xn_hbm.at[pl.ds(tok, 1)], xbuf.at[s, pl.ds(r, 1)], gsem.at[s]).start()
            return carry
        lax.fori_loop(0, tm, body, 0, unroll=8)

    def tile_gather_done(s):
        return pltpu.make_async_copy(xn_hbm.at[pl.ds(0, tm)], xbuf.at[s], gsem.at[s])

    def tile_scatter_done(s):
        return pltpu.make_async_copy(ybuf.at[s], y_hbm.at[pl.ds(0, tm)], ssem.at[s])

    @pl.when((i == 0) & used)
    def _prologue():
        ybuf[0] = jnp.zeros((tm, D_MODEL), F32)
        spare = pltpu.make_async_copy(ybuf.at[0], y_hbm.at[pl.ds(n_assign, tm)], ssem.at[0])
        spare.start()
        spare.wait()
        row_gather(src_cur_ref, 0)

    @pl.when(next_used)
    def _prefetch():
        row_gather(src_next_ref, 1 - slot)

    new_expert = (i == 0) | (te_ref[i] != te_ref[jnp.maximum(i - 1, 0)])

    @pl.when(used & new_expert)
    def _cast_weights():
        rows = 128
        def cast_gu(c, carry):
            r0 = pl.multiple_of(c * rows, rows)
            wgu_b[pl.ds(r0, rows), :] = wgu_ref[pl.ds(r0, rows), :].astype(BF16)
            return carry
        lax.fori_loop(0, D_MODEL // rows, cast_gu, 0)
        def cast_d(c, carry):
            r0 = pl.multiple_of(c * rows, rows)
            wd_b[pl.ds(r0, rows), :] = wd_ref[pl.ds(r0, rows), :].astype(BF16)
            return carry
        lax.fori_loop(0, D_FF // rows, cast_d, 0)

    @pl.when(used)
    def _compute():
        tile_gather_done(slot).wait()

        @pl.when(i >= 2)
        def _():
            tile_scatter_done(slot).wait()

        xb = xbuf[slot].astype(BF16)
        gu = _dot(xb, wgu_b[...]) + bgu_ref[...]
        gate = jnp.minimum(gu[:, :D_FF], SWIGLU_LIMIT)
        up = jnp.clip(gu[:, D_FF:], -SWIGLU_LIMIT, SWIGLU_LIMIT)
        hid = (up + 1.0) * (gate * jax.nn.sigmoid(SWIGLU_ALPHA * gate))
        ybuf[slot] = _dot(hid.astype(BF16), wd_b[...]) + bd_ref[...]

        def body(r, carry):
            dst = dst_ref[0, 0, r]
            pltpu.make_async_copy(ybuf.at[slot, pl.ds(r, 1)], y_hbm.at[pl.ds(dst, 1)], ssem.at[slot]).start()
            return carry
        lax.fori_loop(0, tm, body, 0, unroll=8)

        @pl.when(jnp.logical_not(next_used))
        def _drain():
            tile_scatter_done(slot).wait()

            @pl.when(i >= 1)
            def _():
                tile_scatter_done(1 - slot).wait()


def _moe_stage(xn, tile_e, tile_valid, src_rows, dst_rows, w_gu, b_gu, w_d, b_d, *, tm, nt, n_out_rows):
    def idx_spec(fn):
        return pl.BlockSpec((1, 1, tm), fn, memory_space=pltpu.SMEM)

    grid_spec = pltpu.PrefetchScalarGridSpec(
        num_scalar_prefetch=2,
        grid=(nt,),
        in_specs=[
            idx_spec(lambda i, te, tv: (jnp.minimum(i + 1, nt - 1), 0, 0)),
            idx_spec(lambda i, te, tv: (i, 0, 0)),
            idx_spec(lambda i, te, tv: (i, 0, 0)),
            pl.BlockSpec(memory_space=pl.ANY),
            pl.BlockSpec((None, D_MODEL, 2 * D_FF), lambda i, te, tv: (te[i], 0, 0)),
            pl.BlockSpec((None, 1, 2 * D_FF), lambda i, te, tv: (te[i], 0, 0)),
            pl.BlockSpec((None, D_FF, D_MODEL), lambda i, te, tv: (te[i], 0, 0)),
            pl.BlockSpec((None, 1, D_MODEL), lambda i, te, tv: (te[i], 0, 0)),
        ],
        out_specs=pl.BlockSpec(memory_space=pl.ANY),
        scratch_shapes=[
            pltpu.VMEM((2, tm, D_MODEL), F32),
            pltpu.VMEM((2, tm, D_MODEL), F32),
            pltpu.VMEM((D_MODEL, 2 * D_FF), BF16),
            pltpu.VMEM((D_FF, D_MODEL), BF16),
            pltpu.SemaphoreType.DMA((2,)),
            pltpu.SemaphoreType.DMA((2,)),
        ],
    )
    return pl.pallas_call(
        functools.partial(_moe_body, tm=tm, nt=nt, n_assign=n_out_rows - tm),
        grid_spec=grid_spec,
        out_shape=jax.ShapeDtypeStruct((n_out_rows, D_MODEL), F32),
        compiler_params=pltpu.CompilerParams(
            dimension_semantics=("arbitrary",), vmem_limit_bytes=VMEM_LIMIT_BYTES),
        name="moe",
    )(tile_e, tile_valid, src_rows, src_rows, dst_rows, xn, w_gu, b_gu, w_d, b_d)


def _combine_body(x1_ref, y0_ref, y1_ref, y2_ref, y3_ref, g_ref, pe_ref, np_ref, wpg_ref, wple_ref, nfin_ref,
                  out_ref):
    g = g_ref[...]
    x2 = x1_ref[...]
    for k, y_ref in enumerate((y0_ref, y1_ref, y2_ref, y3_ref)):
        x2 = x2 + g[:, k:k + 1] * y_ref[...]
    hp = _rms(x2, np_ref[...]).astype(BF16)
    gate = jax.nn.sigmoid(_dot(hp, wpg_ref[...]))
    emb = _dot(pe_ref[...].astype(BF16), wple_ref[...])
    x3 = x2 + gate * emb
    out_ref[...] = _rms(x3, nfin_ref[...])


def _combine_stage(x1, ybuf, gates_t, pe, w, *, tc, tok_offset, n_tok_total):
    n = x1.shape[0]
    blk0 = tok_offset // tc
    stride = n_tok_total // tc

    def full(a):
        return pl.BlockSpec(a.shape, lambda i: (0,) * a.ndim)

    def y_spec(k):
        return pl.BlockSpec((tc, D_MODEL), lambda i: (k * stride + blk0 + i, 0))

    weights = [w["norm_ple"], w["w_ple_gate"], w["w_ple"], w["norm_final"]]
    return pl.pallas_call(
        _combine_body,
        grid=(n // tc,),
        in_specs=[pl.BlockSpec((tc, D_MODEL), lambda i: (i, 0))]
        + [y_spec(k) for k in range(TOP_K)]
        + [pl.BlockSpec((tc, TOP_K), lambda i: (blk0 + i, 0)),
           pl.BlockSpec((tc, PLE_DIM), lambda i: (i, 0))]
        + [full(a) for a in weights],
        out_specs=pl.BlockSpec((tc, D_MODEL), lambda i: (i, 0)),
        out_shape=jax.ShapeDtypeStruct((n, D_MODEL), F32),
        compiler_params=pltpu.CompilerParams(
            dimension_semantics=("arbitrary",), vmem_limit_bytes=VMEM_LIMIT_BYTES),
        name="combine",
    )(x1, ybuf, ybuf, ybuf, ybuf, gates_t, pe, *weights)


def _routing_tables(idx_km, n_tok, tm, nt):
    n_assign = TOP_K * n_tok
    flat_e = idx_km.reshape(n_assign)
    order = jnp.argsort(flat_e, stable=True).astype(jnp.int32)
    counts = jnp.sum((flat_e[:, None] == jnp.arange(N_EXPERTS, dtype=jnp.int32)[None, :]).astype(jnp.int32), axis=0)
    n_tiles_e = (counts + tm - 1) // tm
    tile_end = jnp.cumsum(n_tiles_e)
    tile_start = tile_end - n_tiles_e
    grp_start = jnp.cumsum(counts) - counts
    tiles = jnp.arange(nt, dtype=jnp.int32)
    tile_e_raw = jnp.sum((tiles[:, None] >= tile_end[None, :]).astype(jnp.int32), axis=1)
    tile_used = tile_e_raw < N_EXPERTS
    last_e = jnp.max(jnp.where(counts > 0, jnp.arange(N_EXPERTS, dtype=jnp.int32), 0))
    tile_e = jnp.where(tile_used, jnp.minimum(tile_e_raw, N_EXPERTS - 1), last_e).astype(jnp.int32)
    local_tile = tiles - tile_start[tile_e]
    valid_rows = jnp.clip(counts[tile_e] - local_tile * tm, 0, tm)
    tile_valid = jnp.where(tile_used, valid_rows, 0).astype(jnp.int32)
    r = jnp.arange(tm, dtype=jnp.int32)
    pos = grp_start[tile_e][:, None] + local_tile[:, None] * tm + r[None, :]
    row_ok = r[None, :] < tile_valid[:, None]
    a = order[jnp.clip(pos, 0, n_assign - 1)]
    dst = jnp.where(row_ok, a, n_assign + r[None, :]).astype(jnp.int32)
    src = jnp.where(row_ok, a % n_tok, 0).astype(jnp.int32)
    return tile_e, tile_valid, src.reshape(nt, 1, tm), dst.reshape(nt, 1, tm)


def kernel(x_prompt, x_sample, state_hgrn, cache_conv, p_prompt, p_sample, norm_mix, w_in, hgrn_lb_logits, hgrn_norm, conv_w, conv_b, conv_ln_g, conv_ln_b, w_out, norm_ffn, w_router, b_router, w_gate_up, b_gate_up, w_down, b_down, w_ple, norm_ple, w_ple_gate, norm_final):
    depth = norm_mix.shape[0]
    assert depth == 1
    bp, tp, _ = x_prompt.shape
    bs, ts, _ = x_sample.shape
    n_p, n_s = bp * tp, bs * ts
    n_tok = n_p + n_s

    lb_all = jnp.cumsum(jax.nn.softmax(hgrn_lb_logits.astype(F32), axis=0), axis=0)
    wr_t = w_router[0].T
    wr_hi = wr_t.astype(BF16)
    w = {
        "norm_mix": norm_mix[0][None, :], "w_in": w_in[0].astype(BF16), "lb": lb_all[0][None, :],
        "hgrn_norm": hgrn_norm[0][None, :], "conv_w": conv_w[0], "conv_b": conv_b[0][None, :],
        "ln_g": conv_ln_g[0][None, :], "ln_b": conv_ln_b[0][None, :], "w_out": w_out[0].astype(BF16),
        "norm_ffn": norm_ffn[0][None, :], "wr_hi": wr_hi, "wr_lo": (wr_t - wr_hi.astype(F32)).astype(BF16),
        "b_router": b_router[0][:, None],
        "norm_ple": norm_ple[0][None, :], "w_ple_gate": w_ple_gate[0].astype(BF16),
        "w_ple": w_ple[0].astype(BF16), "norm_final": norm_final[None, :],
    }

    pad = ((0, 0), (CACHE_PAD - CACHE_ROWS, 0), (0, 0))
    s0_p = jnp.zeros((bp, N_HEADS, HEAD_DIM, HEAD_DIM), F32)
    c0_p = jnp.zeros((bp, CACHE_PAD, CONV_CH), F32)
    c0_s = jnp.pad(cache_conv[0], pad)

    tb_p = 256
    x1_p, xn_p, idx_p, gate_p, st_p, cc_p = _mix_stage(
        x_prompt, s0_p, c0_p, w, tb=tb_p, chunk=128, row_offset=0, n_rows=n_tok)
    x1_s, xn_s, idx_s, gate_s, st_s, cc_s = _mix_stage(
        x_sample, state_hgrn[0], c0_s, w, tb=ts, chunk=ts, row_offset=n_p, n_rows=n_tok)

    def token_major(a_p, a_s):
        return jnp.concatenate([a_p.transpose(1, 0, 2).reshape(TOP_K, n_p),
                                a_s.transpose(1, 0, 2).reshape(TOP_K, n_s)], axis=1)

    idx_km = token_major(idx_p, idx_s)
    gates_t = token_major(gate_p, gate_s).T
    xn = jnp.concatenate([xn_p, xn_s], axis=0)

    tm = 256
    nt = -(-(TOP_K * n_tok) // tm) + N_EXPERTS
    tile_e, tile_valid, src_rows, dst_rows = _routing_tables(idx_km, n_tok, tm, nt)
    ybuf = _moe_stage(xn, tile_e, tile_valid, src_rows, dst_rows, w_gate_up[0], b_gate_up[0][:, None, :],
                      w_down[0], b_down[0][:, None, :], tm=tm, nt=nt, n_out_rows=TOP_K * n_tok + tm)

    tc = 256
    y_p = _combine_stage(x1_p, ybuf, gates_t, p_prompt[0].reshape(n_p, PLE_DIM), w,
                         tc=tc, tok_offset=0, n_tok_total=n_tok)
    y_s = _combine_stage(x1_s, ybuf, gates_t, p_sample[0].reshape(n_s, PLE_DIM), w,
                         tc=tc, tok_offset=n_p, n_tok_total=n_tok)

    return (y_p.reshape(bp, tp, D_MODEL), y_s.reshape(bs, ts, D_MODEL),
            st_p[None], cc_p[None], st_s[None], cc_s[None])
```

```python
import functools

import jax
import jax.numpy as jnp
from jax import lax
from jax.experimental import pallas as pl
from jax.experimental.pallas import tpu as pltpu

D_MODEL = 1024
HGRN_WIDTH = 512
HEAD_DIM = 128
N_HEADS = HGRN_WIDTH // HEAD_DIM
CONV_CH = 512
CONV_WIDTH = 31
CACHE_ROWS = CONV_WIDTH - 1
CACHE_PAD = 32
IN_COLS = 4 * HGRN_WIDTH + 2 * CONV_CH
N_EXPERTS = 32
TOP_K = 4
D_FF = 1024
SWIGLU_LIMIT = 7.0
SWIGLU_ALPHA = 1.702
PLE_DIM = 256
EPS = 1e-6

SUBLANES = 8
LANES = 128
MOE_COLS = 256
VMEM_LIMIT_BYTES = 56 * 1024 * 1024

F32 = jnp.float32
BF16 = jnp.bfloat16


def _split3(a):
    p1 = a.astype(BF16)
    r1 = a - p1.astype(F32)
    p2 = r1.astype(BF16)
    r2 = r1 - p2.astype(F32)
    return p1, p2, r2.astype(BF16)


def _dot(a, b):
    return jnp.dot(a, b, preferred_element_type=F32)


def _dot_nt(a, b):
    return lax.dot_general(a, b, (((1,), (1,)), ((), ())), preferred_element_type=F32)


def _dot_tn(a, b):
    return lax.dot_general(a, b, (((0,), (0,)), ((), ())), preferred_element_type=F32)


def _rms(x, g):
    return x * lax.rsqrt(jnp.mean(x * x, axis=-1, keepdims=True) + EPS) * g


def _hgrn_chunk(q, kk, f, b, v, st, shift_ref, ones_sq):
    c = q.shape[0]
    row = lax.broadcasted_iota(jnp.int32, (c, c), 0)
    col = lax.broadcasted_iota(jnp.int32, (c, c), 1)
    xor = row ^ col
    diff = row - col

    sc = jnp.zeros((c, c), F32)
    m = c // 2
    while m >= SUBLANES:
        pieces = []
        for blk in range(c // (2 * m)):
            r = blk * 2 * m + m - 1
            pieces.append(jnp.broadcast_to(b[r:r + 1, :], (2 * m, HEAD_DIM)))
        bmid = pieces[0] if len(pieces) == 1 else jnp.concatenate(pieces, axis=0)
        qh = (q * jnp.exp(jnp.minimum(b - bmid, 0.0))).astype(BF16)
        kh = (kk * jnp.exp(jnp.minimum(bmid - b, 0.0))).astype(BF16)
        r_m = _dot_nt(qh, kh)
        sc = jnp.where((xor >= m) & (xor < 2 * m), r_m, sc)
        m //= 2
    sc = jnp.where(diff > 0, sc, 0.0)

    shift_ref[0, SUBLANES:SUBLANES + c, :] = f
    shift_ref[1, SUBLANES:SUBLANES + c, :] = kk
    same_blk = xor < SUBLANES
    decay = None
    for d in range(SUBLANES):
        if d == 0:
            kd = kk
        else:
            f_sh = shift_ref[0, SUBLANES - (d - 1):SUBLANES - (d - 1) + c, :]
            decay = f_sh if decay is None else decay * f_sh
            kd = shift_ref[1, SUBLANES - d:SUBLANES - d + c, :] * decay
        s_d = _dot((q * kd).astype(BF16), ones_sq)
        sc = jnp.where(same_blk & (diff == d), s_d[:, :c], sc)

    o = _dot(sc.astype(BF16), v.astype(BF16))
    o = o + _dot_nt((q * jnp.exp(b)).astype(BF16), st.astype(BF16))
    b_last = b[c - 1:c, :]
    kdec = (kk * jnp.exp(b_last - b)).astype(BF16)
    st_new = st * jnp.exp(b_last) + _dot_tn(v.astype(BF16), kdec)
    return o, st_new


def _mix_body(x_ref, s0_ref, c0_ref, nm_ref, win_ref, lb_ref, hn_ref, cw_ref, cb_ref, lng_ref,
              lnb_ref, wout_ref, nf_ref, wrh_ref, wrl_ref, br_ref,
              x1_ref, xn_ref, idx_ref, gate_ref, sout_ref, cout_ref,
              st_scr, upad_scr, shift_scr, mix_scr, *, tb, chunk):
    j = pl.program_id(1)
    last_j = pl.num_programs(1) - 1

    @pl.when(j == 0)
    def _init():
        for h in range(N_HEADS):
            st_scr[h] = s0_ref[0, h].T
        upad_scr[0:CACHE_PAD, :] = c0_ref[0]
        shift_scr[:, 0:SUBLANES, :] = jnp.zeros((2, SUBLANES, HEAD_DIM), F32)

    x = x_ref[0]
    h_in = _rms(x, nm_ref[...]).astype(BF16)
    proj = _dot(h_in, win_ref[...])

    q_all = proj[:, 0:HGRN_WIDTH]
    fz = proj[:, HGRN_WIDTH:2 * HGRN_WIDTH]
    v_all = proj[:, 2 * HGRN_WIDTH:3 * HGRN_WIDTH]
    g_all = proj[:, 3 * HGRN_WIDTH:4 * HGRN_WIDTH]
    lb = lb_ref[...]
    f_all = lb + (1.0 - lb) * jax.nn.sigmoid(fz)
    lf_all = jnp.log(f_all)
    kk_all = 1.0 - f_all

    row = lax.broadcasted_iota(jnp.int32, (tb, tb), 0)
    col = lax.broadcasted_iota(jnp.int32, (tb, tb), 1)
    tri = ((row >= col) & ((row ^ col) < chunk)).astype(BF16)
    p1, p2, p3 = _split3(lf_all)
    b_all = _dot(tri, p1) + _dot(tri, p2) + _dot(tri, p3)

    ones_sq = jnp.ones((HEAD_DIM, HEAD_DIM), BF16)
    hn = hn_ref[...]
    for h in range(N_HEADS):
        ls = slice(h * HEAD_DIM, (h + 1) * HEAD_DIM)
        st = st_scr[h]
        for ci in range(tb // chunk):
            rs = slice(ci * chunk, (ci + 1) * chunk)
            o, st = _hgrn_chunk(q_all[rs, ls], kk_all[rs, ls], f_all[rs, ls], b_all[rs, ls],
                                v_all[rs, ls], st, shift_scr, ones_sq)
            o = _rms(o, hn[:, ls])
            mix_scr[rs, ls] = (o * jax.nn.silu(g_all[rs, ls])).astype(BF16)
        st_scr[h] = st

    u = proj[:, 4 * HGRN_WIDTH:4 * HGRN_WIDTH + CONV_CH] * jax.nn.sigmoid(
        proj[:, 4 * HGRN_WIDTH + CONV_CH:IN_COLS])
    upad_scr[CACHE_PAD:CACHE_PAD + tb, :] = u
    first = CACHE_PAD - CACHE_ROWS
    acc = jnp.zeros((tb, CONV_CH), F32) + cb_ref[...]
    for tap in range(CONV_WIDTH):
        acc = acc + cw_ref[tap:tap + 1, :] * upad_scr[first + tap:first + tap + tb, :]
    mu = jnp.mean(acc, axis=-1, keepdims=True)
    cen = acc - mu
    var = jnp.mean(cen * cen, axis=-1, keepdims=True)
    cn = cen * lax.rsqrt(var + EPS) * lng_ref[...] + lnb_ref[...]
    mix_scr[:, HGRN_WIDTH:HGRN_WIDTH + CONV_CH] = jax.nn.silu(cn).astype(BF16)
    new_cache = upad_scr[tb:tb + CACHE_PAD, :]
    upad_scr[0:CACHE_PAD, :] = new_cache

    x1 = x + _dot(mix_scr[...], wout_ref[...])
    x1_ref[...] = x1
    xn = _rms(x1, nf_ref[...])
    for s in range(D_MODEL // LANES):
        xn_ref[pl.ds(s, tb, stride=SUBLANES), :] = xn[:, s * LANES:(s + 1) * LANES]
    xh = xn.astype(BF16)
    xl = (xn - xh.astype(F32)).astype(BF16)
    wrh = wrh_ref[...]
    logits = _dot_nt(wrh, xh) + _dot_nt(wrh, xl) + _dot_nt(wrl_ref[...], xh) + br_ref[...]

    eidx = lax.broadcasted_iota(jnp.int32, (N_EXPERTS, tb), 0).astype(F32)
    vals, ids = [], []
    work = logits
    for _ in range(TOP_K):
        mx = jnp.max(work, axis=0, keepdims=True)
        pick = jnp.min(jnp.where(work == mx, eidx, float(N_EXPERTS)), axis=0, keepdims=True)
        vals.append(mx)
        ids.append(pick)
        work = jnp.where(eidx == pick, -jnp.inf, work)
    exps = [jnp.exp(val - vals[0]) for val in vals]
    inv = 1.0 / (exps[0] + exps[1] + exps[2] + exps[3])
    idx_ref[0] = jnp.concatenate(ids, axis=0).astype(jnp.int32)
    gate_ref[0] = jnp.concatenate([e * inv for e in exps], axis=0)

    @pl.when(j == last_j)
    def _fin():
        for h in range(N_HEADS):
            sout_ref[0, h] = st_scr[h].T
        cout_ref[0] = upad_scr[tb + first:tb + CACHE_PAD, :]


def _mix_stage(x, s0, c0, w, *, tb, chunk):
    bsz, t, _ = x.shape
    nj = t // tb

    def full(a):
        return pl.BlockSpec(a.shape, lambda b, j: (0,) * a.ndim)

    weights = [w["norm_mix"], w["w_in"], w["lb"], w["hgrn_norm"], w["conv_w"], w["conv_b"], w["ln_g"],
               w["ln_b"], w["w_out"], w["norm_ffn"], w["wr_hi"], w["wr_lo"], w["b_router"]]
    in_specs = [
        pl.BlockSpec((1, tb, D_MODEL), lambda b, j: (b, j, 0)),
        pl.BlockSpec((1, N_HEADS, HEAD_DIM, HEAD_DIM), lambda b, j: (b, 0, 0, 0)),
        pl.BlockSpec((1, CACHE_PAD, CONV_CH), lambda b, j: (b, 0, 0)),
    ] + [full(a) for a in weights]
    out_shape = (
        jax.ShapeDtypeStruct((bsz * t, D_MODEL), F32),
        jax.ShapeDtypeStruct((bsz * t * SUBLANES, LANES), F32),
        jax.ShapeDtypeStruct((bsz, TOP_K, t), jnp.int32),
        jax.ShapeDtypeStruct((bsz, TOP_K, t), F32),
        jax.ShapeDtypeStruct((bsz, N_HEADS, HEAD_DIM, HEAD_DIM), F32),
        jax.ShapeDtypeStruct((bsz, CACHE_ROWS, CONV_CH), F32),
    )
    out_specs = (
        pl.BlockSpec((tb, D_MODEL), lambda b, j: (b * nj + j, 0)),
        pl.BlockSpec((tb * SUBLANES, LANES), lambda b, j: (b * nj + j, 0)),
        pl.BlockSpec((1, TOP_K, tb), lambda b, j: (b, 0, j)),
        pl.BlockSpec((1, TOP_K, tb), lambda b, j: (b, 0, j)),
        pl.BlockSpec((1, N_HEADS, HEAD_DIM, HEAD_DIM), lambda b, j: (b, 0, 0, 0)),
        pl.BlockSpec((1, CACHE_ROWS, CONV_CH), lambda b, j: (b, 0, 0)),
    )
    return pl.pallas_call(
        functools.partial(_mix_body, tb=tb, chunk=chunk),
        grid=(bsz, nj),
        in_specs=in_specs,
        out_specs=out_specs,
        out_shape=out_shape,
        scratch_shapes=[
            pltpu.VMEM((N_HEADS, HEAD_DIM, HEAD_DIM), F32),
            pltpu.VMEM((CACHE_PAD + tb, CONV_CH), F32),
            pltpu.VMEM((2, SUBLANES + chunk, HEAD_DIM), F32),
            pltpu.VMEM((tb, D_MODEL), BF16),
        ],
        compiler_params=pltpu.CompilerParams(
            dimension_semantics=("arbitrary", "arbitrary"), vmem_limit_bytes=VMEM_LIMIT_BYTES),
        name="mix",
    )(x, s0, c0, *weights)


def _moe_body(te_ref, tv_ref, src_first_ref, src_next_ref, dst_prev_ref, xn_hbm, wgu_ref, bgu_ref, wd_ref, bd_ref,
              y_hbm, xbuf, ybuf, hid_scr, wgu_b, wd_b, gsem, ssem, *, tm, nt, n_assign):
    i = pl.program_id(0)
    slot = lax.rem(i, 2)
    other = 1 - slot
    rows8 = tm * SUBLANES
    used = tv_ref[i] > 0

    def buf_rows(s, r):
        return pl.ds(pl.multiple_of(s * rows8 + r * SUBLANES, SUBLANES), SUBLANES)

    def gather_row(src_ref, r, s):
        tok8 = pl.multiple_of(src_ref[0, 0, r], SUBLANES)
        pltpu.make_async_copy(xn_hbm.at[pl.ds(tok8, SUBLANES)], xbuf.at[buf_rows(s, r)], gsem.at[s]).start()

    def scatter_row(r, s):
        dst8 = pl.multiple_of(dst_prev_ref[0, 0, r], SUBLANES)
        pltpu.make_async_copy(ybuf.at[buf_rows(s, r)], y_hbm.at[pl.ds(dst8, SUBLANES)], ssem.at[s]).start()

    def tile_rows(s):
        return pl.ds(pl.multiple_of(s * rows8, SUBLANES), rows8)

    def tile_gather_done(s):
        return pltpu.make_async_copy(xn_hbm.at[pl.ds(0, rows8)], xbuf.at[tile_rows(s)], gsem.at[s])

    def tile_scatter_done(s):
        return pltpu.make_async_copy(ybuf.at[tile_rows(s)], y_hbm.at[pl.ds(0, rows8)], ssem.at[s])

    @pl.when(i == 0)
    def _prologue():
        ybuf[...] = jnp.zeros((2 * rows8, LANES), F32)
        spare = pltpu.make_async_copy(ybuf.at[pl.ds(0, rows8)], y_hbm.at[pl.ds(n_assign * SUBLANES, rows8)],
                                      ssem.at[0])
        spare.start()
        spare.wait()

        def body(r, carry):
            gather_row(src_first_ref, r, 0)
            return carry
        lax.fori_loop(0, tm, body, 0, unroll=8)

    @pl.when(i < nt)
    def _():
        tile_gather_done(slot).wait()

    @pl.when(i >= 1)
    def _():
        tile_scatter_done(slot).wait()

    new_expert = (i == 0) | (te_ref[i] != te_ref[jnp.maximum(i - 1, 0)])

    @pl.when(used & new_expert)
    def _cast_weights():
        rows = 128

        def cast_gu(c, carry):
            r0 = pl.multiple_of(c * rows, rows)
            wgu_b[pl.ds(r0, rows), :] = wgu_ref[pl.ds(r0, rows), :].astype(BF16)
            return carry
        lax.fori_loop(0, D_MODEL // rows, cast_gu, 0)

        def cast_d(c, carry):
            r0 = pl.multiple_of(c * rows, rows)
            wd_b[pl.ds(r0, rows), :] = wd_ref[pl.ds(r0, rows), :].astype(BF16)
            return carry
        lax.fori_loop(0, D_FF // rows, cast_d, 0)

    n_chunks = D_FF // MOE_COLS
    n_batches = n_chunks
    bounds = [(bi * tm) // n_batches for bi in range(n_batches + 1)]

    def issue_batch(bi):
        if bi >= n_batches:
            return
        for r in range(bounds[bi], bounds[bi + 1]):
            gather_row(src_next_ref, r, other)
            scatter_row(r, other)

    @pl.when(used)
    def _compute():
        base = slot * rows8

        def x_piece(s):
            return xbuf[pl.ds(base + s, tm, stride=SUBLANES), :].astype(BF16)

        x_rest = [x_piece(s) for s in range(1, SUBLANES)]
        for c in range(n_chunks):
            xb = jnp.concatenate([x_piece(0)] + x_rest, axis=1)
            issue_batch(c)
            cs = slice(c * MOE_COLS, (c + 1) * MOE_COLS)
            us = slice(D_FF + c * MOE_COLS, D_FF + (c + 1) * MOE_COLS)
            gate = jnp.minimum(_dot(xb, wgu_b[:, cs]) + bgu_ref[:, cs], SWIGLU_LIMIT)
            up = jnp.clip(_dot(xb, wgu_b[:, us]) + bgu_ref[:, us], -SWIGLU_LIMIT, SWIGLU_LIMIT)
            hid_scr[:, cs] = ((up + 1.0) * (gate * jax.nn.sigmoid(SWIGLU_ALPHA * gate))).astype(BF16)
        hid = hid_scr[...]
        for c in range(n_chunks):
            cs = slice(c * MOE_COLS, (c + 1) * MOE_COLS)
            y_c = _dot(hid, wd_b[:, cs]) + bd_ref[:, cs]
            for p in range(MOE_COLS // LANES):
                s = c * (MOE_COLS // LANES) + p
                ybuf[pl.ds(base + s, tm, stride=SUBLANES), :] = y_c[:, p * LANES:(p + 1) * LANES]

    @pl.when(jnp.logical_not(used))
    def _moves_only():
        @pl.when(i + 1 < nt)
        def _():
            def body(r, carry):
                gather_row(src_next_ref, r, other)
                return carry
            lax.fori_loop(0, tm, body, 0, unroll=8)

        def body(r, carry):
            scatter_row(r, other)
            return carry
        lax.fori_loop(0, tm, body, 0, unroll=8)

    @pl.when(i == nt)
    def _drain():
        tile_scatter_done(other).wait()


def _moe_stage(xn8, tile_e, tile_valid, src8, dst8, w_gu, b_gu, w_d, b_d, *, tm, nt, n_assign):
    def idx_spec(fn):
        return pl.BlockSpec((1, 1, tm), fn, memory_space=pltpu.SMEM)

    grid_spec = pltpu.PrefetchScalarGridSpec(
        num_scalar_prefetch=2,
        grid=(nt + 1,),
        in_specs=[
            idx_spec(lambda i, te, tv: (0, 0, 0)),
            idx_spec(lambda i, te, tv: (jnp.minimum(i + 1, nt - 1), 0, 0)),
            idx_spec(lambda i, te, tv: (i, 0, 0)),
            pl.BlockSpec(memory_space=pl.ANY),
            pl.BlockSpec((None, D_MODEL, 2 * D_FF), lambda i, te, tv: (te[i], 0, 0)),
            pl.BlockSpec((None, 1, 2 * D_FF), lambda i, te, tv: (te[i], 0, 0)),
            pl.BlockSpec((None, D_FF, D_MODEL), lambda i, te, tv: (te[i], 0, 0)),
            pl.BlockSpec((None, 1, D_MODEL), lambda i, te, tv: (te[i], 0, 0)),
        ],
        out_specs=pl.BlockSpec(memory_space=pl.ANY),
        scratch_shapes=[
            pltpu.VMEM((2 * tm * SUBLANES, LANES), F32),
            pltpu.VMEM((2 * tm * SUBLANES, LANES), F32),
            pltpu.VMEM((tm, D_FF), BF16),
            pltpu.VMEM((D_MODEL, 2 * D_FF), BF16),
            pltpu.VMEM((D_FF, D_MODEL), BF16),
            pltpu.SemaphoreType.DMA((2,)),
            pltpu.SemaphoreType.DMA((2,)),
        ],
    )
    return pl.pallas_call(
        functools.partial(_moe_body, tm=tm, nt=nt, n_assign=n_assign),
        grid_spec=grid_spec,
        out_shape=jax.ShapeDtypeStruct(((n_assign + tm) * SUBLANES, LANES), F32),
        compiler_params=pltpu.CompilerParams(
            dimension_semantics=("arbitrary",), vmem_limit_bytes=VMEM_LIMIT_BYTES),
        name="moe",
    )(tile_e, tile_valid, src8, src8, dst8, xn8, w_gu, b_gu, w_d, b_d)


def _combine_body(x1_ref, y0_ref, y1_ref, y2_ref, y3_ref, g_ref, pe_ref, np_ref, wpg_ref, wple_ref, nfin_ref,
                  out_ref, *, tc):
    g = g_ref[...]
    x1 = x1_ref[...]
    pieces = []
    for s in range(D_MODEL // LANES):
        acc = x1[:, s * LANES:(s + 1) * LANES]
        for k, y_ref in enumerate((y0_ref, y1_ref, y2_ref, y3_ref)):
            acc = acc + g[:, k:k + 1] * y_ref[pl.ds(s, tc, stride=SUBLANES), :]
        pieces.append(acc)
    x2 = jnp.concatenate(pieces, axis=1)
    hp = _rms(x2, np_ref[...]).astype(BF16)
    gate = jax.nn.sigmoid(_dot(hp, wpg_ref[...]))
    emb = _dot(pe_ref[...].astype(BF16), wple_ref[...])
    x3 = x2 + gate * emb
    out_ref[...] = _rms(x3, nfin_ref[...])


def _combine_stage(x1, ybuf8, gates_t, pe, w, *, tc, tok_offset, n_tok_total):
    n = x1.shape[0]
    blk0 = tok_offset // tc
    stride = n_tok_total // tc

    def full(a):
        return pl.BlockSpec(a.shape, lambda i: (0,) * a.ndim)

    def y_spec(k):
        return pl.BlockSpec((tc * SUBLANES, LANES), lambda i: (k * stride + blk0 + i, 0))

    weights = [w["norm_ple"], w["w_ple_gate"], w["w_ple"], w["norm_final"]]
    return pl.pallas_call(
        functools.partial(_combine_body, tc=tc),
        grid=(n // tc,),
        in_specs=[pl.BlockSpec((tc, D_MODEL), lambda i: (i, 0))]
        + [y_spec(k) for k in range(TOP_K)]
        + [pl.BlockSpec((tc, TOP_K), lambda i: (blk0 + i, 0)),
           pl.BlockSpec((tc, PLE_DIM), lambda i: (i, 0))]
        + [full(a) for a in weights],
        out_specs=pl.BlockSpec((tc, D_MODEL), lambda i: (i, 0)),
        out_shape=jax.ShapeDtypeStruct((n, D_MODEL), F32),
        compiler_params=pltpu.CompilerParams(
            dimension_semantics=("arbitrary",), vmem_limit_bytes=VMEM_LIMIT_BYTES),
        name="combine",
    )(x1, ybuf8, ybuf8, ybuf8, ybuf8, gates_t, pe, *weights)


def _routing_tables(idx_km, n_tok, tm, nt):
    n_assign = TOP_K * n_tok
    flat_e = idx_km.reshape(n_assign)
    order = jnp.argsort(flat_e, stable=True).astype(jnp.int32)
    counts = jnp.sum((flat_e[:, None] == jnp.arange(N_EXPERTS, dtype=jnp.int32)[None, :]).astype(jnp.int32), axis=0)
    n_tiles_e = (counts + tm - 1) // tm
    tile_end = jnp.cumsum(n_tiles_e)
    tile_start = tile_end - n_tiles_e
    grp_start = jnp.cumsum(counts) - counts
    tiles = jnp.arange(nt + 1, dtype=jnp.int32)
    tile_e_raw = jnp.sum((tiles[:, None] >= tile_end[None, :]).astype(jnp.int32), axis=1)
    tile_used = tile_e_raw < N_EXPERTS
    last_e = jnp.max(jnp.where(counts > 0, jnp.arange(N_EXPERTS, dtype=jnp.int32), 0))
    tile_e = jnp.where(tile_used, jnp.minimum(tile_e_raw, N_EXPERTS - 1), last_e).astype(jnp.int32)
    local_tile = tiles - tile_start[tile_e]
    valid_rows = jnp.clip(counts[tile_e] - local_tile * tm, 0, tm)
    tile_valid = jnp.where(tile_used, valid_rows, 0).astype(jnp.int32)
    r = jnp.arange(tm, dtype=jnp.int32)
    pos = grp_start[tile_e][:nt, None] + local_tile[:nt, None] * tm + r[None, :]
    row_ok = r[None, :] < tile_valid[:nt, None]
    a = order[jnp.clip(pos, 0, n_assign - 1)]
    spare = jnp.broadcast_to(n_assign + r[None, :], (nt, tm))
    dst = jnp.where(row_ok, a, spare)
    dst = jnp.concatenate([spare[:1], dst], axis=0)
    src = jnp.where(row_ok, a % n_tok, 0)
    return (tile_e, tile_valid, (src * SUBLANES).astype(jnp.int32).reshape(nt, 1, tm),
            (dst * SUBLANES).astype(jnp.int32).reshape(nt + 1, 1, tm))


def kernel(x_prompt, x_sample, state_hgrn, cache_conv, p_prompt, p_sample, norm_mix, w_in, hgrn_lb_logits, hgrn_norm, conv_w, conv_b, conv_ln_g, conv_ln_b, w_out, norm_ffn, w_router, b_router, w_gate_up, b_gate_up, w_down, b_down, w_ple, norm_ple, w_ple_gate, norm_final):
    depth = norm_mix.shape[0]
    assert depth == 1
    bp, tp, _ = x_prompt.shape
    bs, ts, _ = x_sample.shape
    n_p, n_s = bp * tp, bs * ts
    n_tok = n_p + n_s

    lb_all = jnp.cumsum(jax.nn.softmax(hgrn_lb_logits.astype(F32), axis=0), axis=0)
    wr_t = w_router[0].T
    wr_hi = wr_t.astype(BF16)
    w = {
        "norm_mix": norm_mix[0][None, :], "w_in": w_in[0].astype(BF16), "lb": lb_all[0][None, :],
        "hgrn_norm": hgrn_norm[0][None, :], "conv_w": conv_w[0], "conv_b": conv_b[0][None, :],
        "ln_g": conv_ln_g[0][None, :], "ln_b": conv_ln_b[0][None, :], "w_out": w_out[0].astype(BF16),
        "norm_ffn": norm_ffn[0][None, :], "wr_hi": wr_hi, "wr_lo": (wr_t - wr_hi.astype(F32)).astype(BF16),
        "b_router": b_router[0][:, None],
        "norm_ple": norm_ple[0][None, :], "w_ple_gate": w_ple_gate[0].astype(BF16),
        "w_ple": w_ple[0].astype(BF16), "norm_final": norm_final[None, :],
    }

    pad = ((0, 0), (CACHE_PAD - CACHE_ROWS, 0), (0, 0))
    s0_p = jnp.zeros((bp, N_HEADS, HEAD_DIM, HEAD_DIM), F32)
    c0_p = jnp.zeros((bp, CACHE_PAD, CONV_CH), F32)
    c0_s = jnp.pad(cache_conv[0], pad)

    x1_p, xn_p, idx_p, gate_p, st_p, cc_p = _mix_stage(x_prompt, s0_p, c0_p, w, tb=256, chunk=128)
    x1_s, xn_s, idx_s, gate_s, st_s, cc_s = _mix_stage(x_sample, state_hgrn[0], c0_s, w, tb=ts, chunk=ts)

    def token_major(a_p, a_s):
        return jnp.concatenate([a_p.transpose(1, 0, 2).reshape(TOP_K, n_p),
                                a_s.transpose(1, 0, 2).reshape(TOP_K, n_s)], axis=1)

    idx_km = token_major(idx_p, idx_s)
    gates_t = token_major(gate_p, gate_s).T
    xn8 = jnp.concatenate([xn_p, xn_s], axis=0)

    tm = 256
    n_assign = TOP_K * n_tok
    nt = -(-n_assign // tm) + N_EXPERTS
    tile_e, tile_valid, src8, dst8 = _routing_tables(idx_km, n_tok, tm, nt)
    ybuf8 = _moe_stage(xn8, tile_e, tile_valid, src8, dst8, w_gate_up[0], b_gate_up[0][:, None, :],
                       w_down[0], b_down[0][:, None, :], tm=tm, nt=nt, n_assign=n_assign)

    tc = 256
    y_p = _combine_stage(x1_p, ybuf8, gates_t, p_prompt[0].reshape(n_p, PLE_DIM), w,
                         tc=tc, tok_offset=0, n_tok_total=n_tok)
    y_s = _combine_stage(x1_s, ybuf8, gates_t, p_sample[0].reshape(n_s, PLE_DIM), w,
                         tc=tc, tok_offset=n_p, n_tok_total=n_tok)

    return (y_p.reshape(bp, tp, D_MODEL), y_s.reshape(bs, ts, D_MODEL),
            st_p[None], cc_p[None], st_s[None], cc_s[None])
```

```python
import functools

import jax
import jax.numpy as jnp
from jax import lax
from jax.experimental import pallas as pl
from jax.experimental.pallas import tpu as pltpu

D_MODEL = 1024
HGRN_WIDTH = 512
HEAD_DIM = 128
N_HEADS = HGRN_WIDTH // HEAD_DIM
CONV_CH = 512
CONV_WIDTH = 31
CACHE_ROWS = CONV_WIDTH - 1
CACHE_PAD = 32
IN_COLS = 4 * HGRN_WIDTH + 2 * CONV_CH
N_EXPERTS = 32
TOP_K = 4
D_FF = 1024
SWIGLU_LIMIT = 7.0
SWIGLU_ALPHA = 1.702
PLE_DIM = 256
EPS = 1e-6

SUBLANES = 8
LANES = 128
VMEM_LIMIT_BYTES = 56 * 1024 * 1024

F32 = jnp.float32
BF16 = jnp.bfloat16
I32 = jnp.int32


def _split3(a):
    p1 = a.astype(BF16)
    r1 = a - p1.astype(F32)
    p2 = r1.astype(BF16)
    r2 = r1 - p2.astype(F32)
    return p1, p2, r2.astype(BF16)


def _dot(a, b):
    return jnp.dot(a, b, preferred_element_type=F32)


def _dot_nt(a, b):
    return lax.dot_general(a, b, (((1,), (1,)), ((), ())), preferred_element_type=F32)


def _dot_tn(a, b):
    return lax.dot_general(a, b, (((0,), (0,)), ((), ())), preferred_element_type=F32)


def _rms(x, g):
    return x * lax.rsqrt(jnp.mean(x * x, axis=-1, keepdims=True) + EPS) * g


def _copy_rows(src_hbm, src_row, dst_vmem, dst_row, n, sem, max_rows):
    bit = max_rows
    while bit >= 1:
        shift = bit.bit_length()
        off = (n >> shift) << shift

        @pl.when((n & bit) != 0)
        def _(bit=bit, off=off):
            src0 = pl.multiple_of((src_row + off) * SUBLANES, SUBLANES)
            dst0 = pl.multiple_of((dst_row + off) * SUBLANES, SUBLANES)
            pltpu.make_async_copy(src_hbm.at[pl.ds(src0, bit * SUBLANES)],
                                  dst_vmem.at[pl.ds(dst0, bit * SUBLANES)], sem).start()
        bit //= 2


def _to_row_tiles(ref, x, base=0):
    rows = x.shape[0]
    for s in range(D_MODEL // LANES):
        ref[pl.ds(base * SUBLANES + s, rows, stride=SUBLANES), :] = x[:, s * LANES:(s + 1) * LANES]


def _from_row_tiles(ref, rows, base=0, dtype=BF16):
    return jnp.concatenate(
        [ref[pl.ds(base * SUBLANES + s, rows, stride=SUBLANES), :].astype(dtype) for s in range(D_MODEL // LANES)],
        axis=1)


def _hgrn_chunk(q, kk, f, b, v, st, shift_ref, ones_sq):
    c = q.shape[0]
    row = lax.broadcasted_iota(I32, (c, c), 0)
    col = lax.broadcasted_iota(I32, (c, c), 1)
    xor = row ^ col
    diff = row - col

    sc = jnp.zeros((c, c), F32)
    m = c // 2
    while m >= SUBLANES:
        pieces = []
        for blk in range(c // (2 * m)):
            r = blk * 2 * m + m - 1
            pieces.append(jnp.broadcast_to(b[r:r + 1, :], (2 * m, HEAD_DIM)))
        bmid = pieces[0] if len(pieces) == 1 else jnp.concatenate(pieces, axis=0)
        qh = (q * jnp.exp(jnp.minimum(b - bmid, 0.0))).astype(BF16)
        kh = (kk * jnp.exp(jnp.minimum(bmid - b, 0.0))).astype(BF16)
        r_m = _dot_nt(qh, kh)
        sc = jnp.where((xor >= m) & (xor < 2 * m), r_m, sc)
        m //= 2
    sc = jnp.where(diff > 0, sc, 0.0)

    shift_ref[0, SUBLANES:SUBLANES + c, :] = f
    shift_ref[1, SUBLANES:SUBLANES + c, :] = kk
    same_blk = xor < SUBLANES
    decay = None
    for d in range(SUBLANES):
        if d == 0:
            kd = kk
        else:
            f_sh = shift_ref[0, SUBLANES - (d - 1):SUBLANES - (d - 1) + c, :]
            decay = f_sh if decay is None else decay * f_sh
            kd = shift_ref[1, SUBLANES - d:SUBLANES - d + c, :] * decay
        s_d = _dot((q * kd).astype(BF16), ones_sq)
        sc = jnp.where(same_blk & (diff == d), s_d[:, :c], sc)

    o = _dot(sc.astype(BF16), v.astype(BF16))
    o = o + _dot_nt((q * jnp.exp(b)).astype(BF16), st.astype(BF16))
    b_last = b[c - 1:c, :]
    kdec = (kk * jnp.exp(b_last - b)).astype(BF16)
    st_new = st * jnp.exp(b_last) + _dot_tn(v.astype(BF16), kdec)
    return o, st_new


def _mix_body(x_ref, s0_ref, c0_ref, nm_ref, win_ref, lb_ref, hn_ref, cw_ref, cb_ref, lng_ref,
              lnb_ref, wout_ref, nf_ref, wrh_ref, wrl_ref, br_ref,
              x1_ref, xs_ref, slot_ref, gate_ref, cnt_ref, sout_ref, cout_ref,
              st_scr, upad_scr, shift_scr, mix_scr, *, tb, chunk):
    j = pl.program_id(1)
    last_j = pl.num_programs(1) - 1

    @pl.when(j == 0)
    def _init():
        for h in range(N_HEADS):
            st_scr[h] = s0_ref[0, h].T
        upad_scr[0:CACHE_PAD, :] = c0_ref[0]
        shift_scr[:, 0:SUBLANES, :] = jnp.zeros((2, SUBLANES, HEAD_DIM), F32)

    x = x_ref[0]
    h_in = _rms(x, nm_ref[...]).astype(BF16)
    proj = _dot(h_in, win_ref[...])

    q_all = proj[:, 0:HGRN_WIDTH]
    fz = proj[:, HGRN_WIDTH:2 * HGRN_WIDTH]
    v_all = proj[:, 2 * HGRN_WIDTH:3 * HGRN_WIDTH]
    g_all = proj[:, 3 * HGRN_WIDTH:4 * HGRN_WIDTH]
    lb = lb_ref[...]
    f_all = lb + (1.0 - lb) * jax.nn.sigmoid(fz)
    lf_all = jnp.log(f_all)
    kk_all = 1.0 - f_all

    row = lax.broadcasted_iota(I32, (tb, tb), 0)
    col = lax.broadcasted_iota(I32, (tb, tb), 1)
    tri = ((row >= col) & ((row ^ col) < chunk)).astype(BF16)
    p1, p2, p3 = _split3(lf_all)
    b_all = _dot(tri, p1) + _dot(tri, p2) + _dot(tri, p3)

    ones_sq = jnp.ones((HEAD_DIM, HEAD_DIM), BF16)
    hn = hn_ref[...]
    for h in range(N_HEADS):
        ls = slice(h * HEAD_DIM, (h + 1) * HEAD_DIM)
        st = st_scr[h]
        for ci in range(tb // chunk):
            rs = slice(ci * chunk, (ci + 1) * chunk)
            o, st = _hgrn_chunk(q_all[rs, ls], kk_all[rs, ls], f_all[rs, ls], b_all[rs, ls],
                                v_all[rs, ls], st, shift_scr, ones_sq)
            o = _rms(o, hn[:, ls])
            mix_scr[rs, ls] = (o * jax.nn.silu(g_all[rs, ls])).astype(BF16)
        st_scr[h] = st

    u = proj[:, 4 * HGRN_WIDTH:4 * HGRN_WIDTH + CONV_CH] * jax.nn.sigmoid(
        proj[:, 4 * HGRN_WIDTH + CONV_CH:IN_COLS])
    upad_scr[CACHE_PAD:CACHE_PAD + tb, :] = u
    first = CACHE_PAD - CACHE_ROWS
    acc = jnp.zeros((tb, CONV_CH), F32) + cb_ref[...]
    for tap in range(CONV_WIDTH):
        acc = acc + cw_ref[tap:tap + 1, :] * upad_scr[first + tap:first + tap + tb, :]
    mu = jnp.mean(acc, axis=-1, keepdims=True)
    cen = acc - mu
    var = jnp.mean(cen * cen, axis=-1, keepdims=True)
    cn = cen * lax.rsqrt(var + EPS) * lng_ref[...] + lnb_ref[...]
    mix_scr[:, HGRN_WIDTH:HGRN_WIDTH + CONV_CH] = jax.nn.silu(cn).astype(BF16)
    new_cache = upad_scr[tb:tb + CACHE_PAD, :]
    upad_scr[0:CACHE_PAD, :] = new_cache

    x1 = x + _dot(mix_scr[...], wout_ref[...])
    x1_ref[...] = x1
    xn = _rms(x1, nf_ref[...])
    xh = xn.astype(BF16)
    xl = (xn - xh.astype(F32)).astype(BF16)
    wrh = wrh_ref[...]
    logits = _dot_nt(wrh, xh) + _dot_nt(wrh, xl) + _dot_nt(wrl_ref[...], xh) + br_ref[...]

    eidx = lax.broadcasted_iota(I32, (N_EXPERTS, tb), 0).astype(F32)
    vals, hots = [], []
    work = logits
    for _ in range(TOP_K):
        mx = jnp.max(work, axis=0, keepdims=True)
        pick = jnp.min(jnp.where(work == mx, eidx, float(N_EXPERTS)), axis=0, keepdims=True)
        hot = eidx == pick
        vals.append(mx)
        hots.append(hot)
        work = jnp.where(hot, -jnp.inf, work)
    exps = [jnp.exp(val - vals[0]) for val in vals]
    inv = 1.0 / (exps[0] + exps[1] + exps[2] + exps[3])
    gate_ref[0] = jnp.concatenate([e * inv for e in exps], axis=0)

    rt = lax.broadcasted_iota(I32, (tb, tb), 0)
    ct = lax.broadcasted_iota(I32, (tb, tb), 1)
    before = (rt < ct).astype(BF16)
    re = lax.broadcasted_iota(I32, (N_EXPERTS, N_EXPERTS), 0)
    ce = lax.broadcasted_iota(I32, (N_EXPERTS, N_EXPERTS), 1)
    lower = (ce < re).astype(BF16)
    hot_f = [hot.astype(F32) for hot in hots]
    tots = [jnp.sum(hf, axis=1, keepdims=True) for hf in hot_f]
    cnt = tots[0] + tots[1] + tots[2] + tots[3]
    cnt_ref[0] = cnt.astype(I32)
    start = _dot(lower, jnp.broadcast_to(cnt, (N_EXPERTS, LANES)).astype(BF16))[:, 0:1]
    slots = []
    base = start
    for k in range(TOP_K):
        pre = _dot(hot_f[k].astype(BF16), before)
        slots.append(jnp.sum(hot_f[k] * (base + pre), axis=0, keepdims=True))
        base = base + tots[k]
    slot_ref[0] = jnp.concatenate(slots, axis=0).astype(I32)

    srow = lax.broadcasted_iota(I32, (TOP_K * tb, tb), 0).astype(F32)
    perm = (srow == slots[0])
    for k in range(1, TOP_K):
        perm = perm | (srow == slots[k])
    _to_row_tiles(xs_ref, _dot(perm.astype(BF16), xh))

    @pl.when(j == last_j)
    def _fin():
        for h in range(N_HEADS):
            sout_ref[0, h] = st_scr[h].T
        cout_ref[0] = upad_scr[tb + first:tb + CACHE_PAD, :]


def _mix_stage(x, s0, c0, w, *, tb, chunk):
    bsz, t, _ = x.shape
    nj = t // tb
    nblk = bsz * nj

    def full(a):
        return pl.BlockSpec(a.shape, lambda b, j: (0,) * a.ndim)

    weights = [w["norm_mix"], w["w_in"], w["lb"], w["hgrn_norm"], w["conv_w"], w["conv_b"], w["ln_g"],
               w["ln_b"], w["w_out"], w["norm_ffn"], w["wr_hi"], w["wr_lo"], w["b_router"]]
    in_specs = [
        pl.BlockSpec((1, tb, D_MODEL), lambda b, j: (b, j, 0)),
        pl.BlockSpec((1, N_HEADS, HEAD_DIM, HEAD_DIM), lambda b, j: (b, 0, 0, 0)),
        pl.BlockSpec((1, CACHE_PAD, CONV_CH), lambda b, j: (b, 0, 0)),
    ] + [full(a) for a in weights]
    out_shape = (
        jax.ShapeDtypeStruct((bsz * t, D_MODEL), F32),
        jax.ShapeDtypeStruct((TOP_K * bsz * t * SUBLANES, LANES), F32),
        jax.ShapeDtypeStruct((bsz, TOP_K, t), I32),
        jax.ShapeDtypeStruct((bsz, TOP_K, t), F32),
        jax.ShapeDtypeStruct((nblk, N_EXPERTS, 1), I32),
        jax.ShapeDtypeStruct((bsz, N_HEADS, HEAD_DIM, HEAD_DIM), F32),
        jax.ShapeDtypeStruct((bsz, CACHE_ROWS, CONV_CH), F32),
    )
    out_specs = (
        pl.BlockSpec((tb, D_MODEL), lambda b, j: (b * nj + j, 0)),
        pl.BlockSpec((TOP_K * tb * SUBLANES, LANES), lambda b, j: (b * nj + j, 0)),
        pl.BlockSpec((1, TOP_K, tb), lambda b, j: (b, 0, j)),
        pl.BlockSpec((1, TOP_K, tb), lambda b, j: (b, 0, j)),
        pl.BlockSpec((1, N_EXPERTS, 1), lambda b, j: (b * nj + j, 0, 0)),
        pl.BlockSpec((1, N_HEADS, HEAD_DIM, HEAD_DIM), lambda b, j: (b, 0, 0, 0)),
        pl.BlockSpec((1, CACHE_ROWS, CONV_CH), lambda b, j: (b, 0, 0)),
    )
    return pl.pallas_call(
        functools.partial(_mix_body, tb=tb, chunk=chunk),
        grid=(bsz, nj),
        in_specs=in_specs,
        out_specs=out_specs,
        out_shape=out_shape,
        scratch_shapes=[
            pltpu.VMEM((N_HEADS, HEAD_DIM, HEAD_DIM), F32),
            pltpu.VMEM((CACHE_PAD + tb, CONV_CH), F32),
            pltpu.VMEM((2, SUBLANES + chunk, HEAD_DIM), F32),
            pltpu.VMEM((tb, D_MODEL), BF16),
        ],
        compiler_params=pltpu.CompilerParams(
            dimension_semantics=("arbitrary", "arbitrary"), vmem_limit_bytes=VMEM_LIMIT_BYTES),
        name="mix",
    )(x, s0, c0, *weights)


def _moe_body(te_ref, tlo_ref, tv_ref, glo_ref, ghi_ref, pre_ref, cnt_ref, src_ref,
              xs_p_hbm, xs_s_hbm, zeros_hbm, wgu_ref, bgu_ref, wd_ref, bd_ref, y_ref,
              xbuf, wgu_b, wd_b, gsem, *, tm, nt, nb, nb_p, max_run_p, max_run_s):
    i = pl.program_id(0)
    slot = lax.rem(i, 2)

    def issue_tile(t, s):
        e = te_ref[t]
        lo = tlo_ref[t]
        nv = tv_ref[t]
        hi = lo + nv
        dst0 = s * tm

        def body(g, carry):
            p0 = pre_ref[e * nb + g]
            a = jnp.maximum(p0, lo)
            b = jnp.minimum(p0 + cnt_ref[e * nb + g], hi)
            n = jnp.maximum(b - a, 0)
            src = src_ref[e * nb + g] + (a - p0)
            dst = dst0 + (a - lo)

            @pl.when(g < nb_p)
            def _():
                _copy_rows(xs_p_hbm, src, xbuf, dst, n, gsem.at[s], max_run_p)

            @pl.when(g >= nb_p)
            def _():
                _copy_rows(xs_s_hbm, src, xbuf, dst, n, gsem.at[s], max_run_s)
            return carry
        lax.fori_loop(glo_ref[t], ghi_ref[t], body, 0)
        _copy_rows(zeros_hbm, 0, xbuf, dst0 + nv, tm - nv, gsem.at[s], tm)

    @pl.when(i == 0)
    def _prologue():
        issue_tile(0, 0)

    @pl.when(i + 1 < nt)
    def _prefetch():
        issue_tile(i + 1, 1 - slot)

    used = tv_ref[i] > 0
    new_expert = (i == 0) | (te_ref[i] != te_ref[jnp.maximum(i - 1, 0)])

    @pl.when(used & new_expert)
    def _cast_weights():
        rows = 128

        def cast_gu(c, carry):
            r0 = pl.multiple_of(c * rows, rows)
            wgu_b[pl.ds(r0, rows), :] = wgu_ref[pl.ds(r0, rows), :].astype(BF16)
            return carry
        lax.fori_loop(0, D_MODEL // rows, cast_gu, 0)

        def cast_d(c, carry):
            r0 = pl.multiple_of(c * rows, rows)
            wd_b[pl.ds(r0, rows), :] = wd_ref[pl.ds(r0, rows), :].astype(BF16)
            return carry
        lax.fori_loop(0, D_FF // rows, cast_d, 0)

    tile = pl.ds(pl.multiple_of(slot * tm * SUBLANES, tm * SUBLANES), tm * SUBLANES)
    pltpu.make_async_copy(zeros_hbm, xbuf.at[tile], gsem.at[slot]).wait()

    @pl.when(used)
    def _compute():
        xb = _from_row_tiles(xbuf, tm, base=slot * tm)
        gu = _dot(xb, wgu_b[...]) + bgu_ref[...]
        gate = jnp.minimum(gu[:, :D_FF], SWIGLU_LIMIT)
        up = jnp.clip(gu[:, D_FF:], -SWIGLU_LIMIT, SWIGLU_LIMIT)
        hid = (up + 1.0) * (gate * jax.nn.sigmoid(SWIGLU_ALPHA * gate))
        _to_row_tiles(y_ref, _dot(hid.astype(BF16), wd_b[...]) + bd_ref[...])

    @pl.when(jnp.logical_not(used))
    def _unused_tile():
        y_ref[...] = jnp.zeros((tm * SUBLANES, LANES), F32)


def _moe_stage(xs_p, xs_s, zeros, tabs, w_gu, b_gu, w_d, b_d, *, tm, nt, nb, nb_p, max_run_p, max_run_s):
    n_pref = len(tabs)

    def wmap(i, te, *_):
        return (te[i], 0, 0)

    grid_spec = pltpu.PrefetchScalarGridSpec(
        num_scalar_prefetch=n_pref,
        grid=(nt,),
        in_specs=[
            pl.BlockSpec(memory_space=pl.ANY),
            pl.BlockSpec(memory_space=pl.ANY),
            pl.BlockSpec(memory_space=pl.ANY),
            pl.BlockSpec((None, D_MODEL, 2 * D_FF), wmap),
            pl.BlockSpec((None, 1, 2 * D_FF), wmap),
            pl.BlockSpec((None, D_FF, D_MODEL), wmap),
            pl.BlockSpec((None, 1, D_MODEL), wmap),
        ],
        out_specs=pl.BlockSpec((tm * SUBLANES, LANES), lambda i, *_: (i, 0)),
        scratch_shapes=[
            pltpu.VMEM((2 * tm * SUBLANES, LANES), F32),
            pltpu.VMEM((D_MODEL, 2 * D_FF), BF16),
            pltpu.VMEM((D_FF, D_MODEL), BF16),
            pltpu.SemaphoreType.DMA((2,)),
        ],
    )
    return pl.pallas_call(
        functools.partial(_moe_body, tm=tm, nt=nt, nb=nb, nb_p=nb_p, max_run_p=max_run_p, max_run_s=max_run_s),
        grid_spec=grid_spec,
        out_shape=jax.ShapeDtypeStruct((nt * tm * SUBLANES, LANES), F32),
        compiler_params=pltpu.CompilerParams(
            dimension_semantics=("arbitrary",), vmem_limit_bytes=VMEM_LIMIT_BYTES),
        name="moe",
    )(*tabs, xs_p, xs_s, zeros, w_gu, b_gu, w_d, b_d)


def _combine_body(pre_ref, cnt_ref, lst_ref, ys0_ref, y_hbm, x1_ref, slot_ref, g_ref, pe_ref, np_ref, wpg_ref,
                  wple_ref, nfin_ref, out_ref, stag, sem, *, tc, nsteps, nb, g0, blk_per_step, rows_per_blk,
                  max_run):
    i = pl.program_id(0)
    slot = lax.rem(i, 2)
    rows = TOP_K * tc

    def issue_step(step, s):
        for bi in range(blk_per_step):
            g = g0 + step * blk_per_step + bi
            dst0 = s * rows + bi * rows_per_blk

            def body(e, carry):
                src = ys0_ref[e] + pre_ref[e * nb + g]
                _copy_rows(y_hbm, src, stag, dst0 + lst_ref[g * N_EXPERTS + e], cnt_ref[e * nb + g],
                           sem.at[s], max_run)
                return carry
            lax.fori_loop(0, N_EXPERTS, body, 0)

    @pl.when(i == 0)
    def _prologue():
        issue_step(0, 0)

    @pl.when(i + 1 < nsteps)
    def _prefetch():
        issue_step(i + 1, 1 - slot)

    tile = pl.ds(pl.multiple_of(slot * rows * SUBLANES, rows * SUBLANES), rows * SUBLANES)
    pltpu.make_async_copy(y_hbm.at[pl.ds(0, rows * SUBLANES)], stag.at[tile], sem.at[slot]).wait()

    scol = lax.broadcasted_iota(I32, (tc, rows), 1)
    sl = slot_ref[...]
    g = g_ref[...]
    back = jnp.zeros((tc, rows), F32)
    for k in range(TOP_K):
        back = jnp.where(scol == sl[:, k:k + 1], g[:, k:k + 1], back)
    x2 = x1_ref[...] + _dot(back.astype(BF16), _from_row_tiles(stag, rows, base=slot * rows))
    hp = _rms(x2, np_ref[...]).astype(BF16)
    gate = jax.nn.sigmoid(_dot(hp, wpg_ref[...]))
    emb = _dot(pe_ref[...].astype(BF16), wple_ref[...])
    x3 = x2 + gate * emb
    out_ref[...] = _rms(x3, nfin_ref[...])


def _combine_stage(x1, ys, slots_t, gates_t, pe, tabs, w, *, tc, nb, g0, blk_per_step, rows_per_blk, max_run):
    n = x1.shape[0]
    nsteps = n // tc

    def full(a):
        return pl.BlockSpec(a.shape, lambda i, *_: (0,) * a.ndim)

    weights = [w["norm_ple"], w["w_ple_gate"], w["w_ple"], w["norm_final"]]
    grid_spec = pltpu.PrefetchScalarGridSpec(
        num_scalar_prefetch=len(tabs),
        grid=(nsteps,),
        in_specs=[
            pl.BlockSpec(memory_space=pl.ANY),
            pl.BlockSpec((tc, D_MODEL), lambda i, *_: (i, 0)),
            pl.BlockSpec((tc, TOP_K), lambda i, *_: (i, 0)),
            pl.BlockSpec((tc, TOP_K), lambda i, *_: (i, 0)),
            pl.BlockSpec((tc, PLE_DIM), lambda i, *_: (i, 0)),
        ] + [full(a) for a in weights],
        out_specs=pl.BlockSpec((tc, D_MODEL), lambda i, *_: (i, 0)),
        scratch_shapes=[
            pltpu.VMEM((2 * TOP_K * tc * SUBLANES, LANES), F32),
            pltpu.SemaphoreType.DMA((2,)),
        ],
    )
    return pl.pallas_call(
        functools.partial(_combine_body, tc=tc, nsteps=nsteps, nb=nb, g0=g0, blk_per_step=blk_per_step,
                          rows_per_blk=rows_per_blk, max_run=max_run),
        grid_spec=grid_spec,
        out_shape=jax.ShapeDtypeStruct((n, D_MODEL), F32),
        compiler_params=pltpu.CompilerParams(
            dimension_semantics=("arbitrary",), vmem_limit_bytes=VMEM_LIMIT_BYTES),
        name="combine",
    )(*tabs, ys, x1, slots_t, gates_t, pe, *weights)


def _dispatch_tables(cnt, blk_row0, tm, nt):
    nb = cnt.shape[0]
    pre = jnp.cumsum(cnt, axis=0) - cnt
    total = jnp.sum(cnt, axis=0)
    lst = jnp.cumsum(cnt, axis=1) - cnt
    src = blk_row0[:, None] + lst
    n_tiles_e = (total + tm - 1) // tm
    tile_end = jnp.cumsum(n_tiles_e)
    tile_start = tile_end - n_tiles_e
    tiles = jnp.arange(nt, dtype=I32)
    tile_e_raw = jnp.sum((tiles[:, None] >= tile_end[None, :]).astype(I32), axis=1)
    used = tile_e_raw < N_EXPERTS
    last_e = jnp.max(jnp.where(total > 0, jnp.arange(N_EXPERTS, dtype=I32), 0))
    tile_e = jnp.where(used, jnp.minimum(tile_e_raw, N_EXPERTS - 1), last_e).astype(I32)
    tile_lo = jnp.where(used, (tiles - tile_start[tile_e]) * tm, 0).astype(I32)
    tile_valid = jnp.where(used, jnp.clip(total[tile_e] - tile_lo, 0, tm), 0).astype(I32)
    p_t = pre.T[tile_e]
    c_t = cnt.T[tile_e]
    meets = (p_t < (tile_lo + tile_valid)[:, None]) & ((p_t + c_t) > tile_lo[:, None]) & (c_t > 0)
    gidx = jnp.arange(nb, dtype=I32)[None, :]
    g_lo = jnp.min(jnp.where(meets, gidx, nb), axis=1).astype(I32)
    g_hi = jnp.max(jnp.where(meets, gidx + 1, 0), axis=1).astype(I32)
    g_lo = jnp.minimum(g_lo, g_hi)
    flat = lambda a: a.T.reshape(-1).astype(I32)
    return dict(tile_e=tile_e, tile_lo=tile_lo, tile_valid=tile_valid, g_lo=g_lo, g_hi=g_hi,
                pre=flat(pre), cnt=flat(cnt), src=flat(src), lst=lst.reshape(-1).astype(I32),
                ys0=(tile_start * tm).astype(I32))


def kernel(x_prompt, x_sample, state_hgrn, cache_conv, p_prompt, p_sample, norm_mix, w_in, hgrn_lb_logits, hgrn_norm, conv_w, conv_b, conv_ln_g, conv_ln_b, w_out, norm_ffn, w_router, b_router, w_gate_up, b_gate_up, w_down, b_down, w_ple, norm_ple, w_ple_gate, norm_final):
    depth = norm_mix.shape[0]
    assert depth == 1
    bp, tp, _ = x_prompt.shape
    bs, ts, _ = x_sample.shape
    n_p, n_s = bp * tp, bs * ts
    n_tok = n_p + n_s

    lb_all = jnp.cumsum(jax.nn.softmax(hgrn_lb_logits.astype(F32), axis=0), axis=0)
    wr_t = w_router[0].T
    wr_hi = wr_t.astype(BF16)
    w = {
        "norm_mix": norm_mix[0][None, :], "w_in": w_in[0].astype(BF16), "lb": lb_all[0][None, :],
        "hgrn_norm": hgrn_norm[0][None, :], "conv_w": conv_w[0], "conv_b": conv_b[0][None, :],
        "ln_g": conv_ln_g[0][None, :], "ln_b": conv_ln_b[0][None, :], "w_out": w_out[0].astype(BF16),
        "norm_ffn": norm_ffn[0][None, :], "wr_hi": wr_hi, "wr_lo": (wr_t - wr_hi.astype(F32)).astype(BF16),
        "b_router": b_router[0][:, None],
        "norm_ple": norm_ple[0][None, :], "w_ple_gate": w_ple_gate[0].astype(BF16),
        "w_ple": w_ple[0].astype(BF16), "norm_final": norm_final[None, :],
    }

    pad = ((0, 0), (CACHE_PAD - CACHE_ROWS, 0), (0, 0))
    s0_p = jnp.zeros((bp, N_HEADS, HEAD_DIM, HEAD_DIM), F32)
    c0_p = jnp.zeros((bp, CACHE_PAD, CONV_CH), F32)
    c0_s = jnp.pad(cache_conv[0], pad)

    tb_p = 256
    x1_p, xs_p, slot_p, gate_p, cnt_p, st_p, cc_p = _mix_stage(x_prompt, s0_p, c0_p, w, tb=tb_p, chunk=128)
    x1_s, xs_s, slot_s, gate_s, cnt_s, st_s, cc_s = _mix_stage(x_sample, state_hgrn[0], c0_s, w, tb=ts, chunk=ts)

    nb_p, nb_s = n_p // tb_p, bs
    nb = nb_p + nb_s
    cnt = jnp.concatenate([cnt_p[:, :, 0], cnt_s[:, :, 0]], axis=0)
    blk_row0 = jnp.concatenate([jnp.arange(nb_p, dtype=I32) * (TOP_K * tb_p),
                                jnp.arange(nb_s, dtype=I32) * (TOP_K * ts)])
    tm = 256
    nt = -(-(TOP_K * n_tok) // tm) + N_EXPERTS
    t = _dispatch_tables(cnt, blk_row0, tm, nt)

    zeros = jnp.zeros((tm * SUBLANES, LANES), F32)
    moe_tabs = (t["tile_e"], t["tile_lo"], t["tile_valid"], t["g_lo"], t["g_hi"], t["pre"], t["cnt"], t["src"])
    ys = _moe_stage(xs_p, xs_s, zeros, moe_tabs, w_gate_up[0], b_gate_up[0][:, None, :], w_down[0],
                    b_down[0][:, None, :], tm=tm, nt=nt, nb=nb, nb_p=nb_p, max_run_p=tb_p, max_run_s=ts)

    def token_cols(a):
        return a.transpose(0, 2, 1).reshape(-1, TOP_K)

    comb_tabs = (t["pre"], t["cnt"], t["lst"], t["ys0"])
    y_p = _combine_stage(x1_p, ys, token_cols(slot_p), token_cols(gate_p), p_prompt[0].reshape(n_p, PLE_DIM),
                         comb_tabs, w, tc=tb_p, nb=nb, g0=0, blk_per_step=1, rows_per_blk=TOP_K * tb_p,
                         max_run=tb_p)
    slot_s_step = slot_s + (jnp.arange(bs, dtype=I32) * (TOP_K * ts))[:, None, None]
    y_s = _combine_stage(x1_s, ys, token_cols(slot_s_step), token_cols(gate_s), p_sample[0].reshape(n_s, PLE_DIM),
                         comb_tabs, w, tc=n_s, nb=nb, g0=nb_p, blk_per_step=nb_s, rows_per_blk=TOP_K * ts,
                         max_run=ts)

    return (y_p.reshape(bp, tp, D_MODEL), y_s.reshape(bs, ts, D_MODEL),
            st_p[None], cc_p[None], st_s[None], cc_s[None])
```

```python
import functools

import jax
import jax.numpy as jnp
from jax import lax
from jax.experimental import pallas as pl
from jax.experimental.pallas import tpu as pltpu

D_MODEL = 1024
HGRN_WIDTH = 512
HEAD_DIM = 128
N_HEADS = HGRN_WIDTH // HEAD_DIM
CONV_CH = 512
CONV_WIDTH = 31
CACHE_ROWS = CONV_WIDTH - 1
CACHE_PAD = 32
IN_COLS = 4 * HGRN_WIDTH + 2 * CONV_CH
N_EXPERTS = 32
TOP_K = 4
D_FF = 1024
SWIGLU_LIMIT = 7.0
SWIGLU_ALPHA = 1.702
PLE_DIM = 256
EPS = 1e-6

SUBLANES = 8
LANES = 128
COMMON_RUN = 64
VMEM_LIMIT_BYTES = 56 * 1024 * 1024

F32 = jnp.float32
BF16 = jnp.bfloat16
I32 = jnp.int32


def _split3(a):
    p1 = a.astype(BF16)
    r1 = a - p1.astype(F32)
    p2 = r1.astype(BF16)
    r2 = r1 - p2.astype(F32)
    return p1, p2, r2.astype(BF16)


def _dot(a, b):
    return jnp.dot(a, b, preferred_element_type=F32)


def _dot_nt(a, b):
    return lax.dot_general(a, b, (((1,), (1,)), ((), ())), preferred_element_type=F32)


def _dot_tn(a, b):
    return lax.dot_general(a, b, (((0,), (0,)), ((), ())), preferred_element_type=F32)


def _rms(x, g):
    return x * lax.rsqrt(jnp.mean(x * x, axis=-1, keepdims=True) + EPS) * g


def _copy_rows(src_hbm, src_row, dst_vmem, dst_row, n, sem, max_rows):
    def pieces(hi_bit, lo_bit):
        bit = hi_bit
        while bit >= lo_bit:
            shift = bit.bit_length()
            off = (n >> shift) << shift

            @pl.when((n & bit) != 0)
            def _(bit=bit, off=off):
                src0 = pl.multiple_of((src_row + off) * SUBLANES, SUBLANES)
                dst0 = pl.multiple_of((dst_row + off) * SUBLANES, SUBLANES)
                pltpu.make_async_copy(src_hbm.at[pl.ds(src0, bit * SUBLANES)],
                                      dst_vmem.at[pl.ds(dst0, bit * SUBLANES)], sem).start()
            bit //= 2

    if max_rows > COMMON_RUN:
        @pl.when(n >= COMMON_RUN)
        def _():
            pieces(max_rows, COMMON_RUN)
        pieces(COMMON_RUN // 2, 1)
    else:
        pieces(max_rows, 1)


def _to_row_tiles(ref, x, base=0):
    rows = x.shape[0]
    for s in range(D_MODEL // LANES):
        ref[pl.ds(base * SUBLANES + s, rows, stride=SUBLANES), :] = x[:, s * LANES:(s + 1) * LANES]


def _from_row_tiles(ref, rows, base=0, dtype=BF16):
    return jnp.concatenate(
        [ref[pl.ds(base * SUBLANES + s, rows, stride=SUBLANES), :].astype(dtype) for s in range(D_MODEL // LANES)],
        axis=1)


def _hgrn_chunk(q, kk, f, b, v, st, shift_ref, ones_sq):
    c = q.shape[0]
    row = lax.broadcasted_iota(I32, (c, c), 0)
    col = lax.broadcasted_iota(I32, (c, c), 1)
    xor = row ^ col
    diff = row - col

    sc = jnp.zeros((c, c), F32)
    m = c // 2
    while m >= SUBLANES:
        pieces = []
        for blk in range(c // (2 * m)):
            r = blk * 2 * m + m - 1
            pieces.append(jnp.broadcast_to(b[r:r + 1, :], (2 * m, HEAD_DIM)))
        bmid = pieces[0] if len(pieces) == 1 else jnp.concatenate(pieces, axis=0)
        e_m = jnp.exp(-jnp.abs(b - bmid))
        qh = (q * e_m).astype(BF16)
        kh = (kk * e_m).astype(BF16)
        r_m = _dot_nt(qh, kh)
        sc = jnp.where((xor >= m) & (xor < 2 * m), r_m, sc)
        m //= 2
    sc = jnp.where(diff > 0, sc, 0.0)

    shift_ref[0, SUBLANES:SUBLANES + c, :] = f
    shift_ref[1, SUBLANES:SUBLANES + c, :] = kk
    same_blk = xor < SUBLANES
    decay = None
    for d in range(SUBLANES):
        if d == 0:
            kd = kk
        else:
            f_sh = shift_ref[0, SUBLANES - (d - 1):SUBLANES - (d - 1) + c, :]
            decay = f_sh if decay is None else decay * f_sh
            kd = shift_ref[1, SUBLANES - d:SUBLANES - d + c, :] * decay
        s_d = _dot((q * kd).astype(BF16), ones_sq)
        sc = jnp.where(same_blk & (diff == d), s_d[:, :c], sc)

    o = _dot(sc.astype(BF16), v.astype(BF16))
    o = o + _dot_nt((q * jnp.exp(b)).astype(BF16), st.astype(BF16))
    b_last = b[c - 1:c, :]
    kdec = (kk * jnp.exp(b_last - b)).astype(BF16)
    st_new = st * jnp.exp(b_last) + _dot_tn(v.astype(BF16), kdec)
    return o, st_new


def _mix_body(x_ref, s0_ref, c0_ref, nm_ref, win_ref, lb_ref, hn_ref, cw_ref, cb_ref, lng_ref,
              lnb_ref, wout_ref, nf_ref, wrh_ref, wrl_ref, br_ref, tri_ref, before_ref, lower_ref, ones_ref,
              x1_ref, xs_ref, slot_ref, gate_ref, cnt_ref, sout_ref, cout_ref,
              st_scr, upad_scr, shift_scr, mix_scr, *, tb, chunk):
    j = pl.program_id(1)
    last_j = pl.num_programs(1) - 1

    @pl.when(j == 0)
    def _init():
        for h in range(N_HEADS):
            st_scr[h] = s0_ref[0, h].T
        upad_scr[0:CACHE_PAD, :] = c0_ref[0]
        shift_scr[:, 0:SUBLANES, :] = jnp.zeros((2, SUBLANES, HEAD_DIM), F32)

    x = x_ref[0]
    h_in = _rms(x, nm_ref[...]).astype(BF16)
    proj = _dot(h_in, win_ref[...])

    q_all = proj[:, 0:HGRN_WIDTH]
    fz = proj[:, HGRN_WIDTH:2 * HGRN_WIDTH]
    v_all = proj[:, 2 * HGRN_WIDTH:3 * HGRN_WIDTH]
    g_all = proj[:, 3 * HGRN_WIDTH:4 * HGRN_WIDTH]
    lb = lb_ref[...]
    f_all = lb + (1.0 - lb) * jax.nn.sigmoid(fz)
    lf_all = jnp.log(f_all)
    kk_all = 1.0 - f_all

    tri = tri_ref[...]
    p1, p2, p3 = _split3(lf_all)
    b_all = _dot(tri, p1) + _dot(tri, p2) + _dot(tri, p3)

    ones_sq = ones_ref[...]
    hn = hn_ref[...]
    for h in range(N_HEADS):
        ls = slice(h * HEAD_DIM, (h + 1) * HEAD_DIM)
        st = st_scr[h]
        for ci in range(tb // chunk):
            rs = slice(ci * chunk, (ci + 1) * chunk)
            o, st = _hgrn_chunk(q_all[rs, ls], kk_all[rs, ls], f_all[rs, ls], b_all[rs, ls],
                                v_all[rs, ls], st, shift_scr, ones_sq)
            o = _rms(o, hn[:, ls])
            mix_scr[rs, ls] = (o * jax.nn.silu(g_all[rs, ls])).astype(BF16)
        st_scr[h] = st

    u = proj[:, 4 * HGRN_WIDTH:4 * HGRN_WIDTH + CONV_CH] * jax.nn.sigmoid(
        proj[:, 4 * HGRN_WIDTH + CONV_CH:IN_COLS])
    upad_scr[CACHE_PAD:CACHE_PAD + tb, :] = u
    first = CACHE_PAD - CACHE_ROWS
    acc = jnp.zeros((tb, CONV_CH), F32) + cb_ref[...]
    for tap in range(CONV_WIDTH):
        acc = acc + cw_ref[tap:tap + 1, :] * upad_scr[first + tap:first + tap + tb, :]
    mu = jnp.mean(acc, axis=-1, keepdims=True)
    cen = acc - mu
    var = jnp.mean(cen * cen, axis=-1, keepdims=True)
    cn = cen * lax.rsqrt(var + EPS) * lng_ref[...] + lnb_ref[...]
    mix_scr[:, HGRN_WIDTH:HGRN_WIDTH + CONV_CH] = jax.nn.silu(cn).astype(BF16)
    new_cache = upad_scr[tb:tb + CACHE_PAD, :]
    upad_scr[0:CACHE_PAD, :] = new_cache

    x1 = x + _dot(mix_scr[...], wout_ref[...])
    x1_ref[...] = x1
    xn = _rms(x1, nf_ref[...])
    xh = xn.astype(BF16)
    xl = (xn - xh.astype(F32)).astype(BF16)
    wrh = wrh_ref[...]
    logits = _dot_nt(wrh, xh) + _dot_nt(wrh, xl) + _dot_nt(wrl_ref[...], xh) + br_ref[...]

    eidx = lax.broadcasted_iota(I32, (N_EXPERTS, tb), 0).astype(F32)
    vals, hots = [], []
    work = logits
    for _ in range(TOP_K):
        mx = jnp.max(work, axis=0, keepdims=True)
        pick = jnp.min(jnp.where(work == mx, eidx, float(N_EXPERTS)), axis=0, keepdims=True)
        hot = eidx == pick
        vals.append(mx)
        hots.append(hot)
        work = jnp.where(hot, -jnp.inf, work)
    exps = [jnp.exp(val - vals[0]) for val in vals]
    inv = 1.0 / (exps[0] + exps[1] + exps[2] + exps[3])
    gate_ref[0] = jnp.concatenate([e * inv for e in exps], axis=0)

    before = before_ref[...]
    lower = lower_ref[...]
    hot_f = [hot.astype(F32) for hot in hots]
    tots = [jnp.sum(hf, axis=1, keepdims=True) for hf in hot_f]
    cnt = tots[0] + tots[1] + tots[2] + tots[3]
    cnt_ref[0] = cnt.astype(I32)
    start = _dot(lower, jnp.broadcast_to(cnt, (N_EXPERTS, LANES)).astype(BF16))[:, 0:1]
    slots = []
    base = start
    for k in range(TOP_K):
        pre = _dot(hot_f[k].astype(BF16), before)
        slots.append(jnp.sum(hot_f[k] * (base + pre), axis=0, keepdims=True))
        base = base + tots[k]
    slot_ref[0] = jnp.concatenate(slots, axis=0).astype(I32)

    srow = lax.broadcasted_iota(I32, (TOP_K * tb, tb), 0).astype(F32)
    perm = (srow == slots[0])
    for k in range(1, TOP_K):
        perm = perm | (srow == slots[k])
    _to_row_tiles(xs_ref, _dot(perm.astype(BF16), xh))

    @pl.when(j == last_j)
    def _fin():
        for h in range(N_HEADS):
            sout_ref[0, h] = st_scr[h].T
        cout_ref[0] = upad_scr[tb + first:tb + CACHE_PAD, :]


def _mix_stage(x, s0, c0, w, *, tb, chunk):
    bsz, t, _ = x.shape
    nj = t // tb
    nblk = bsz * nj

    def full(a):
        return pl.BlockSpec(a.shape, lambda b, j: (0,) * a.ndim)

    weights = [w["norm_mix"], w["w_in"], w["lb"], w["hgrn_norm"], w["conv_w"], w["conv_b"], w["ln_g"],
               w["ln_b"], w["w_out"], w["norm_ffn"], w["wr_hi"], w["wr_lo"], w["b_router"]]
    r_t = lax.broadcasted_iota(I32, (tb, tb), 0)
    c_t = lax.broadcasted_iota(I32, (tb, tb), 1)
    r_e = lax.broadcasted_iota(I32, (N_EXPERTS, N_EXPERTS), 0)
    c_e = lax.broadcasted_iota(I32, (N_EXPERTS, N_EXPERTS), 1)
    weights += [((r_t >= c_t) & ((r_t ^ c_t) < chunk)).astype(BF16), (r_t < c_t).astype(BF16),
                (c_e < r_e).astype(BF16), jnp.ones((HEAD_DIM, HEAD_DIM), BF16)]
    in_specs = [
        pl.BlockSpec((1, tb, D_MODEL), lambda b, j: (b, j, 0)),
        pl.BlockSpec((1, N_HEADS, HEAD_DIM, HEAD_DIM), lambda b, j: (b, 0, 0, 0)),
        pl.BlockSpec((1, CACHE_PAD, CONV_CH), lambda b, j: (b, 0, 0)),
    ] + [full(a) for a in weights]
    out_shape = (
        jax.ShapeDtypeStruct((bsz * t, D_MODEL), F32),
        jax.ShapeDtypeStruct((TOP_K * bsz * t * SUBLANES, LANES), F32),
        jax.ShapeDtypeStruct((bsz, TOP_K, t), I32),
        jax.ShapeDtypeStruct((bsz, TOP_K, t), F32),
        jax.ShapeDtypeStruct((nblk, N_EXPERTS, 1), I32),
        jax.ShapeDtypeStruct((bsz, N_HEADS, HEAD_DIM, HEAD_DIM), F32),
        jax.ShapeDtypeStruct((bsz, CACHE_ROWS, CONV_CH), F32),
    )
    out_specs = (
        pl.BlockSpec((tb, D_MODEL), lambda b, j: (b * nj + j, 0)),
        pl.BlockSpec((TOP_K * tb * SUBLANES, LANES), lambda b, j: (b * nj + j, 0)),
        pl.BlockSpec((1, TOP_K, tb), lambda b, j: (b, 0, j)),
        pl.BlockSpec((1, TOP_K, tb), lambda b, j: (b, 0, j)),
        pl.BlockSpec((1, N_EXPERTS, 1), lambda b, j: (b * nj + j, 0, 0)),
        pl.BlockSpec((1, N_HEADS, HEAD_DIM, HEAD_DIM), lambda b, j: (b, 0, 0, 0)),
        pl.BlockSpec((1, CACHE_ROWS, CONV_CH), lambda b, j: (b, 0, 0)),
    )
    return pl.pallas_call(
        functools.partial(_mix_body, tb=tb, chunk=chunk),
        grid=(bsz, nj),
        in_specs=in_specs,
        out_specs=out_specs,
        out_shape=out_shape,
        scratch_shapes=[
            pltpu.VMEM((N_HEADS, HEAD_DIM, HEAD_DIM), F32),
            pltpu.VMEM((CACHE_PAD + tb, CONV_CH), F32),
            pltpu.VMEM((2, SUBLANES + chunk, HEAD_DIM), F32),
            pltpu.VMEM((tb, D_MODEL), BF16),
        ],
        compiler_params=pltpu.CompilerParams(
            dimension_semantics=("arbitrary", "arbitrary"), vmem_limit_bytes=VMEM_LIMIT_BYTES),
        name="mix",
    )(x, s0, c0, *weights)


def _moe_body(te_ref, tlo_ref, tv_ref, glo_ref, ghi_ref, pre_ref, cnt_ref, src_ref,
              xs_p_hbm, xs_s_hbm, zeros_hbm, wgu_ref, bgu_ref, wd_ref, bd_ref, y_ref,
              xbuf, wgu_b, wd_b, gsem, *, tm, nt, nb, nb_p, max_run_p, max_run_s):
    i = pl.program_id(0)
    slot = lax.rem(i, 2)

    def issue_tile(t, s):
        e = te_ref[t]
        lo = tlo_ref[t]
        nv = tv_ref[t]
        hi = lo + nv
        dst0 = s * tm

        def runs(xs_hbm, max_run):
            def body(g, carry):
                p0 = pre_ref[e * nb + g]
                a = jnp.maximum(p0, lo)
                b = jnp.minimum(p0 + cnt_ref[e * nb + g], hi)
                _copy_rows(xs_hbm, src_ref[e * nb + g] + (a - p0), xbuf, dst0 + (a - lo), jnp.maximum(b - a, 0),
                           gsem.at[s], max_run)
                return carry
            return body
        g_lo, g_hi = glo_ref[t], ghi_ref[t]
        lax.fori_loop(g_lo, jnp.minimum(g_hi, nb_p), runs(xs_p_hbm, max_run_p), 0)
        lax.fori_loop(jnp.maximum(g_lo, nb_p), g_hi, runs(xs_s_hbm, max_run_s), 0)
        _copy_rows(zeros_hbm, 0, xbuf, dst0 + nv, tm - nv, gsem.at[s], tm)

    @pl.when(i == 0)
    def _prologue():
        issue_tile(0, 0)

    @pl.when(i + 1 < nt)
    def _prefetch():
        issue_tile(i + 1, 1 - slot)

    used = tv_ref[i] > 0
    new_expert = (i == 0) | (te_ref[i] != te_ref[jnp.maximum(i - 1, 0)])

    @pl.when(used & new_expert)
    def _cast_weights():
        rows = 128

        def cast_gu(c, carry):
            r0 = pl.multiple_of(c * rows, rows)
            wgu_b[pl.ds(r0, rows), :] = wgu_ref[pl.ds(r0, rows), :].astype(BF16)
            return carry
        lax.fori_loop(0, D_MODEL // rows, cast_gu, 0)

        def cast_d(c, carry):
            r0 = pl.multiple_of(c * rows, rows)
            wd_b[pl.ds(r0, rows), :] = wd_ref[pl.ds(r0, rows), :].astype(BF16)
            return carry
        lax.fori_loop(0, D_FF // rows, cast_d, 0)

    tile = pl.ds(pl.multiple_of(slot * tm * SUBLANES, tm * SUBLANES), tm * SUBLANES)
    pltpu.make_async_copy(zeros_hbm, xbuf.at[tile], gsem.at[slot]).wait()

    @pl.when(used)
    def _compute():
        xb = _from_row_tiles(xbuf, tm, base=slot * tm)
        gu = _dot(xb, wgu_b[...]) + bgu_ref[...]
        gate = jnp.minimum(gu[:, :D_FF], SWIGLU_LIMIT)
        up = jnp.clip(gu[:, D_FF:], -SWIGLU_LIMIT, SWIGLU_LIMIT)
        hid = (up + 1.0) * (gate * jax.nn.sigmoid(SWIGLU_ALPHA * gate))
        _to_row_tiles(y_ref, _dot(hid.astype(BF16), wd_b[...]) + bd_ref[...])

    @pl.when(jnp.logical_not(used))
    def _unused_tile():
        y_ref[...] = jnp.zeros((tm * SUBLANES, LANES), F32)


def _moe_stage(xs_p, xs_s, zeros, tabs, w_gu, b_gu, w_d, b_d, *, tm, nt, nb, nb_p, max_run_p, max_run_s):
    n_pref = len(tabs)

    def wmap(i, te, *_):
        return (te[i], 0, 0)

    grid_spec = pltpu.PrefetchScalarGridSpec(
        num_scalar_prefetch=n_pref,
        grid=(nt,),
        in_specs=[
            pl.BlockSpec(memory_space=pl.ANY),
            pl.BlockSpec(memory_space=pl.ANY),
            pl.BlockSpec(memory_space=pl.ANY),
            pl.BlockSpec((None, D_MODEL, 2 * D_FF), wmap),
            pl.BlockSpec((None, 1, 2 * D_FF), wmap),
            pl.BlockSpec((None, D_FF, D_MODEL), wmap),
            pl.BlockSpec((None, 1, D_MODEL), wmap),
        ],
        out_specs=pl.BlockSpec((tm * SUBLANES, LANES), lambda i, *_: (i, 0)),
        scratch_shapes=[
            pltpu.VMEM((2 * tm * SUBLANES, LANES), F32),
            pltpu.VMEM((D_MODEL, 2 * D_FF), BF16),
            pltpu.VMEM((D_FF, D_MODEL), BF16),
            pltpu.SemaphoreType.DMA((2,)),
        ],
    )
    return pl.pallas_call(
        functools.partial(_moe_body, tm=tm, nt=nt, nb=nb, nb_p=nb_p, max_run_p=max_run_p, max_run_s=max_run_s),
        grid_spec=grid_spec,
        out_shape=jax.ShapeDtypeStruct((nt * tm * SUBLANES, LANES), F32),
        compiler_params=pltpu.CompilerParams(
            dimension_semantics=("arbitrary",), vmem_limit_bytes=VMEM_LIMIT_BYTES),
        name="moe",
    )(*tabs, xs_p, xs_s, zeros, w_gu, b_gu, w_d, b_d)


def _combine_body(pre_ref, cnt_ref, lst_ref, ys0_ref, y_hbm, x1_ref, slot_ref, g_ref, pe_ref, np_ref, wpg_ref,
                  wple_ref, nfin_ref, out_ref, stag, sem, *, tc, nsteps, nb, g0, blk_per_step, rows_per_blk,
                  max_run):
    i = pl.program_id(0)
    slot = lax.rem(i, 2)
    rows = TOP_K * tc

    def issue_step(step, s):
        for bi in range(blk_per_step):
            g = g0 + step * blk_per_step + bi
            dst0 = s * rows + bi * rows_per_blk

            def body(e, carry):
                src = ys0_ref[e] + pre_ref[e * nb + g]
                _copy_rows(y_hbm, src, stag, dst0 + lst_ref[g * N_EXPERTS + e], cnt_ref[e * nb + g],
                           sem.at[s], max_run)
                return carry
            lax.fori_loop(0, N_EXPERTS, body, 0)

    @pl.when(i == 0)
    def _prologue():
        issue_step(0, 0)

    @pl.when(i + 1 < nsteps)
    def _prefetch():
        issue_step(i + 1, 1 - slot)

    tile = pl.ds(pl.multiple_of(slot * rows * SUBLANES, rows * SUBLANES), rows * SUBLANES)
    pltpu.make_async_copy(y_hbm.at[pl.ds(0, rows * SUBLANES)], stag.at[tile], sem.at[slot]).wait()

    scol = lax.broadcasted_iota(I32, (tc, rows), 1)
    sl = slot_ref[...]
    g = g_ref[...]
    back = jnp.zeros((tc, rows), F32)
    for k in range(TOP_K):
        back = jnp.where(scol == sl[:, k:k + 1], g[:, k:k + 1], back)
    x2 = x1_ref[...] + _dot(back.astype(BF16), _from_row_tiles(stag, rows, base=slot * rows))
    hp = _rms(x2, np_ref[...]).astype(BF16)
    gate = jax.nn.sigmoid(_dot(hp, wpg_ref[...]))
    emb = _dot(pe_ref[...].astype(BF16), wple_ref[...])
    x3 = x2 + gate * emb
    out_ref[...] = _rms(x3, nfin_ref[...])


def _combine_stage(x1, ys, slots_t, gates_t, pe, tabs, w, *, tc, nb, g0, blk_per_step, rows_per_blk, max_run):
    n = x1.shape[0]
    nsteps = n // tc

    def full(a):
        return pl.BlockSpec(a.shape, lambda i, *_: (0,) * a.ndim)

    weights = [w["norm_ple"], w["w_ple_gate"], w["w_ple"], w["norm_final"]]
    grid_spec = pltpu.PrefetchScalarGridSpec(
        num_scalar_prefetch=len(tabs),
        grid=(nsteps,),
        in_specs=[
            pl.BlockSpec(memory_space=pl.ANY),
            pl.BlockSpec((tc, D_MODEL), lambda i, *_: (i, 0)),
            pl.BlockSpec((tc, TOP_K), lambda i, *_: (i, 0)),
            pl.BlockSpec((tc, TOP_K), lambda i, *_: (i, 0)),
            pl.BlockSpec((tc, PLE_DIM), lambda i, *_: (i, 0)),
        ] + [full(a) for a in weights],
        out_specs=pl.BlockSpec((tc, D_MODEL), lambda i, *_: (i, 0)),
        scratch_shapes=[
            pltpu.VMEM((2 * TOP_K * tc * SUBLANES, LANES), F32),
            pltpu.SemaphoreType.DMA((2,)),
        ],
    )
    return pl.pallas_call(
        functools.partial(_combine_body, tc=tc, nsteps=nsteps, nb=nb, g0=g0, blk_per_step=blk_per_step,
                          rows_per_blk=rows_per_blk, max_run=max_run),
        grid_spec=grid_spec,
        out_shape=jax.ShapeDtypeStruct((n, D_MODEL), F32),
        compiler_params=pltpu.CompilerParams(
            dimension_semantics=("arbitrary",), vmem_limit_bytes=VMEM_LIMIT_BYTES),
        name="combine",
    )(*tabs, ys, x1, slots_t, gates_t, pe, *weights)


def _dispatch_tables(cnt, blk_row0, tm, nt):
    nb = cnt.shape[0]
    pre = jnp.cumsum(cnt, axis=0) - cnt
    total = jnp.sum(cnt, axis=0)
    lst = jnp.cumsum(cnt, axis=1) - cnt
    src = blk_row0[:, None] + lst
    n_tiles_e = (total + tm - 1) // tm
    tile_end = jnp.cumsum(n_tiles_e)
    tile_start = tile_end - n_tiles_e
    tiles = jnp.arange(nt, dtype=I32)
    tile_e_raw = jnp.sum((tiles[:, None] >= tile_end[None, :]).astype(I32), axis=1)
    used = tile_e_raw < N_EXPERTS
    last_e = jnp.max(jnp.where(total > 0, jnp.arange(N_EXPERTS, dtype=I32), 0))
    tile_e = jnp.where(used, jnp.minimum(tile_e_raw, N_EXPERTS - 1), last_e).astype(I32)
    tile_lo = jnp.where(used, (tiles - tile_start[tile_e]) * tm, 0).astype(I32)
    tile_valid = jnp.where(used, jnp.clip(total[tile_e] - tile_lo, 0, tm), 0).astype(I32)
    p_t = pre.T[tile_e]
    c_t = cnt.T[tile_e]
    meets = (p_t < (tile_lo + tile_valid)[:, None]) & ((p_t + c_t) > tile_lo[:, None]) & (c_t > 0)
    gidx = jnp.arange(nb, dtype=I32)[None, :]
    g_lo = jnp.min(jnp.where(meets, gidx, nb), axis=1).astype(I32)
    g_hi = jnp.max(jnp.where(meets, gidx + 1, 0), axis=1).astype(I32)
    g_lo = jnp.minimum(g_lo, g_hi)
    flat = lambda a: a.T.reshape(-1).astype(I32)
    return dict(tile_e=tile_e, tile_lo=tile_lo, tile_valid=tile_valid, g_lo=g_lo, g_hi=g_hi,
                pre=flat(pre), cnt=flat(cnt), src=flat(src), lst=lst.reshape(-1).astype(I32),
                ys0=(tile_start * tm).astype(I32))


def kernel(x_prompt, x_sample, state_hgrn, cache_conv, p_prompt, p_sample, norm_mix, w_in, hgrn_lb_logits, hgrn_norm, conv_w, conv_b, conv_ln_g, conv_ln_b, w_out, norm_ffn, w_router, b_router, w_gate_up, b_gate_up, w_down, b_down, w_ple, norm_ple, w_ple_gate, norm_final):
    depth = norm_mix.shape[0]
    assert depth == 1
    bp, tp, _ = x_prompt.shape
    bs, ts, _ = x_sample.shape
    n_p, n_s = bp * tp, bs * ts
    n_tok = n_p + n_s

    lb_all = jnp.cumsum(jax.nn.softmax(hgrn_lb_logits.astype(F32), axis=0), axis=0)
    wr_t = w_router[0].T
    wr_hi = wr_t.astype(BF16)
    w = {
        "norm_mix": norm_mix[0][None, :], "w_in": w_in[0].astype(BF16), "lb": lb_all[0][None, :],
        "hgrn_norm": hgrn_norm[0][None, :], "conv_w": conv_w[0], "conv_b": conv_b[0][None, :],
        "ln_g": conv_ln_g[0][None, :], "ln_b": conv_ln_b[0][None, :], "w_out": w_out[0].astype(BF16),
        "norm_ffn": norm_ffn[0][None, :], "wr_hi": wr_hi, "wr_lo": (wr_t - wr_hi.astype(F32)).astype(BF16),
        "b_router": b_router[0][:, None],
        "norm_ple": norm_ple[0][None, :], "w_ple_gate": w_ple_gate[0].astype(BF16),
        "w_ple": w_ple[0].astype(BF16), "norm_final": norm_final[None, :],
    }

    pad = ((0, 0), (CACHE_PAD - CACHE_ROWS, 0), (0, 0))
    s0_p = jnp.zeros((bp, N_HEADS, HEAD_DIM, HEAD_DIM), F32)
    c0_p = jnp.zeros((bp, CACHE_PAD, CONV_CH), F32)
    c0_s = jnp.pad(cache_conv[0], pad)

    tb_p = 256
    x1_p, xs_p, slot_p, gate_p, cnt_p, st_p, cc_p = _mix_stage(x_prompt, s0_p, c0_p, w, tb=tb_p, chunk=128)
    x1_s, xs_s, slot_s, gate_s, cnt_s, st_s, cc_s = _mix_stage(x_sample, state_hgrn[0], c0_s, w, tb=ts, chunk=ts)

    nb_p, nb_s = n_p // tb_p, bs
    nb = nb_p + nb_s
    cnt = jnp.concatenate([cnt_p[:, :, 0], cnt_s[:, :, 0]], axis=0)
    blk_row0 = jnp.concatenate([jnp.arange(nb_p, dtype=I32) * (TOP_K * tb_p),
                                jnp.arange(nb_s, dtype=I32) * (TOP_K * ts)])
    tm = 256
    nt = -(-(TOP_K * n_tok) // tm) + N_EXPERTS
    t = _dispatch_tables(cnt, blk_row0, tm, nt)

    zeros = jnp.zeros((tm * SUBLANES, LANES), F32)
    moe_tabs = (t["tile_e"], t["tile_lo"], t["tile_valid"], t["g_lo"], t["g_hi"], t["pre"], t["cnt"], t["src"])
    ys = _moe_stage(xs_p, xs_s, zeros, moe_tabs, w_gate_up[0], b_gate_up[0][:, None, :], w_down[0],
                    b_down[0][:, None, :], tm=tm, nt=nt, nb=nb, nb_p=nb_p, max_run_p=tb_p, max_run_s=ts)

    def token_cols(a):
        return a.transpose(0, 2, 1).reshape(-1, TOP_K)

    comb_tabs = (t["pre"], t["cnt"], t["lst"], t["ys0"])
    y_p = _combine_stage(x1_p, ys, token_cols(slot_p), token_cols(gate_p), p_prompt[0].reshape(n_p, PLE_DIM),
                         comb_tabs, w, tc=tb_p, nb=nb, g0=0, blk_per_step=1, rows_per_blk=TOP_K * tb_p,
                         max_run=tb_p)
    slot_s_step = slot_s + (jnp.arange(bs, dtype=I32) * (TOP_K * ts))[:, None, None]
    y_s = _combine_stage(x1_s, ys, token_cols(slot_s_step), token_cols(gate_s), p_sample[0].reshape(n_s, PLE_DIM),
                         comb_tabs, w, tc=n_s, nb=nb, g0=nb_p, blk_per_step=nb_s, rows_per_blk=TOP_K * ts,
                         max_run=ts)

    return (y_p.reshape(bp, tp, D_MODEL), y_s.reshape(bs, ts, D_MODEL),
            st_p[None], cc_p[None], st_s[None], cc_s[None])
```

```python
import functools

import jax
import jax.numpy as jnp
from jax import lax
from jax.experimental import pallas as pl
from jax.experimental.pallas import tpu as pltpu

D_MODEL = 1024
HGRN_WIDTH = 512
HEAD_DIM = 128
N_HEADS = HGRN_WIDTH // HEAD_DIM
CONV_CH = 512
CONV_WIDTH = 31
CACHE_ROWS = CONV_WIDTH - 1
CACHE_PAD = 32
IN_COLS = 4 * HGRN_WIDTH + 2 * CONV_CH
N_EXPERTS = 32
TOP_K = 4
D_FF = 1024
SWIGLU_LIMIT = 7.0
SWIGLU_ALPHA = 1.702
PLE_DIM = 256
EPS = 1e-6

SUBLANES = 8
LANES = 128
COMMON_RUN = 64
VMEM_LIMIT_BYTES = 56 * 1024 * 1024

F32 = jnp.float32
BF16 = jnp.bfloat16
I32 = jnp.int32


def _split3(a):
    p1 = a.astype(BF16)
    r1 = a - p1.astype(F32)
    p2 = r1.astype(BF16)
    r2 = r1 - p2.astype(F32)
    return p1, p2, r2.astype(BF16)


def _dot(a, b):
    return jnp.dot(a, b, preferred_element_type=F32)


def _dot_nt(a, b):
    return lax.dot_general(a, b, (((1,), (1,)), ((), ())), preferred_element_type=F32)


def _dot_tn(a, b):
    return lax.dot_general(a, b, (((0,), (0,)), ((), ())), preferred_element_type=F32)


def _rms(x, g):
    return x * lax.rsqrt(jnp.mean(x * x, axis=-1, keepdims=True) + EPS) * g


def _copy_rows(src_hbm, src_row, dst_vmem, dst_row, n, sem, max_rows):
    def pieces(hi_bit, lo_bit):
        bit = hi_bit
        while bit >= lo_bit:
            shift = bit.bit_length()
            off = (n >> shift) << shift

            @pl.when((n & bit) != 0)
            def _(bit=bit, off=off):
                src0 = pl.multiple_of((src_row + off) * SUBLANES, SUBLANES)
                dst0 = pl.multiple_of((dst_row + off) * SUBLANES, SUBLANES)
                pltpu.make_async_copy(src_hbm.at[pl.ds(src0, bit * SUBLANES)],
                                      dst_vmem.at[pl.ds(dst0, bit * SUBLANES)], sem).start()
            bit //= 2

    if max_rows > COMMON_RUN:
        @pl.when(n >= COMMON_RUN)
        def _():
            pieces(max_rows, COMMON_RUN)
        pieces(COMMON_RUN // 2, 1)
    else:
        pieces(max_rows, 1)


def _to_row_tiles(ref, x, base=0):
    rows = x.shape[0]
    for s in range(D_MODEL // LANES):
        ref[pl.ds(base * SUBLANES + s, rows, stride=SUBLANES), :] = x[:, s * LANES:(s + 1) * LANES]


def _from_row_tiles(ref, rows, base=0, dtype=BF16):
    return jnp.concatenate(
        [ref[pl.ds(base * SUBLANES + s, rows, stride=SUBLANES), :].astype(dtype) for s in range(D_MODEL // LANES)],
        axis=1)


def _hgrn_chunk(q, kk, f, b, v, st, shift_ref, ones_sq):
    c = q.shape[0]
    row = lax.broadcasted_iota(I32, (c, c), 0)
    col = lax.broadcasted_iota(I32, (c, c), 1)
    xor = row ^ col
    diff = row - col

    sc = jnp.zeros((c, c), F32)
    m = c // 2
    while m >= SUBLANES:
        pieces = []
        for blk in range(c // (2 * m)):
            r = blk * 2 * m + m - 1
            pieces.append(jnp.broadcast_to(b[r:r + 1, :], (2 * m, HEAD_DIM)))
        bmid = pieces[0] if len(pieces) == 1 else jnp.concatenate(pieces, axis=0)
        e_m = jnp.exp(-jnp.abs(b - bmid))
        qh = (q * e_m).astype(BF16)
        kh = (kk * e_m).astype(BF16)
        r_m = _dot_nt(qh, kh)
        sc = jnp.where((xor >= m) & (xor < 2 * m), r_m, sc)
        m //= 2
    sc = jnp.where(diff > 0, sc, 0.0)

    shift_ref[0, SUBLANES:SUBLANES + c, :] = f
    shift_ref[1, SUBLANES:SUBLANES + c, :] = kk
    same_blk = xor < SUBLANES
    decay = None
    for d in range(SUBLANES):
        if d == 0:
            kd = kk
        else:
            f_sh = shift_ref[0, SUBLANES - (d - 1):SUBLANES - (d - 1) + c, :]
            decay = f_sh if decay is None else decay * f_sh
            kd = shift_ref[1, SUBLANES - d:SUBLANES - d + c, :] * decay
        s_d = _dot((q * kd).astype(BF16), ones_sq)
        sc = jnp.where(same_blk & (diff == d), s_d[:, :c], sc)

    o = _dot(sc.astype(BF16), v.astype(BF16))
    o = o + _dot_nt((q * jnp.exp(b)).astype(BF16), st.astype(BF16))
    b_last = b[c - 1:c, :]
    kdec = (kk * jnp.exp(b_last - b)).astype(BF16)
    st_new = st * jnp.exp(b_last) + _dot_tn(v.astype(BF16), kdec)
    return o, st_new


def _mix_body(x_ref, s0_ref, c0_ref, nm_ref, win_ref, lb_ref, hn_ref, cw_ref, cb_ref, lng_ref,
              lnb_ref, wout_ref, nf_ref, wrh_ref, wrl_ref, br_ref, tri_ref, before_ref, lower_ref, ones_ref,
              x1_ref, xs_ref, slot_ref, gate_ref, cnt_ref, sout_ref, cout_ref,
              st_scr, upad_scr, shift_scr, mix_scr, carry_scr, *, tb, chunk, nj, nblk):
    s = pl.program_id(0)
    j = lax.rem(jnp.minimum(s, nblk - 1), nj)

    @pl.when(s == 0)
    def _():
        carry_scr[...] = jnp.zeros((2, tb, D_MODEL), BF16)

    @pl.when(j == 0)
    def _init():
        for h in range(N_HEADS):
            st_scr[h] = s0_ref[0, h].T
        upad_scr[0:CACHE_PAD, :] = c0_ref[0]
        shift_scr[:, 0:SUBLANES, :] = jnp.zeros((2, SUBLANES, HEAD_DIM), F32)

    route = _route_block(carry_scr[0], carry_scr[1], wrh_ref, wrl_ref, br_ref, before_ref, lower_ref,
                         xs_ref, slot_ref, gate_ref, cnt_ref, tb=tb)

    def advance_route(n=1):
        for _ in range(n):
            next(route, None)

    advance_route()
    x = x_ref[0]
    h_in = _rms(x, nm_ref[...]).astype(BF16)
    proj = _dot(h_in, win_ref[...])
    advance_route(2)

    q_all = proj[:, 0:HGRN_WIDTH]
    fz = proj[:, HGRN_WIDTH:2 * HGRN_WIDTH]
    v_all = proj[:, 2 * HGRN_WIDTH:3 * HGRN_WIDTH]
    g_all = proj[:, 3 * HGRN_WIDTH:4 * HGRN_WIDTH]
    lb = lb_ref[...]
    f_all = lb + (1.0 - lb) * jax.nn.sigmoid(fz)
    lf_all = jnp.log(f_all)
    kk_all = 1.0 - f_all

    tri = tri_ref[...]
    p1, p2, p3 = _split3(lf_all)
    b_all = _dot(tri, p1) + _dot(tri, p2) + _dot(tri, p3)
    advance_route(2)

    ones_sq = ones_ref[...]
    hn = hn_ref[...]
    for h in range(N_HEADS):
        ls = slice(h * HEAD_DIM, (h + 1) * HEAD_DIM)
        st = st_scr[h]
        for ci in range(tb // chunk):
            rs = slice(ci * chunk, (ci + 1) * chunk)
            o, st = _hgrn_chunk(q_all[rs, ls], kk_all[rs, ls], f_all[rs, ls], b_all[rs, ls],
                                v_all[rs, ls], st, shift_scr, ones_sq)
            o = _rms(o, hn[:, ls])
            mix_scr[rs, ls] = (o * jax.nn.silu(g_all[rs, ls])).astype(BF16)
            advance_route()
        st_scr[h] = st

    u = proj[:, 4 * HGRN_WIDTH:4 * HGRN_WIDTH + CONV_CH] * jax.nn.sigmoid(
        proj[:, 4 * HGRN_WIDTH + CONV_CH:IN_COLS])
    upad_scr[CACHE_PAD:CACHE_PAD + tb, :] = u
    first = CACHE_PAD - CACHE_ROWS
    acc = jnp.zeros((tb, CONV_CH), F32) + cb_ref[...]
    for tap in range(CONV_WIDTH):
        acc = acc + cw_ref[tap:tap + 1, :] * upad_scr[first + tap:first + tap + tb, :]
    mu = jnp.mean(acc, axis=-1, keepdims=True)
    cen = acc - mu
    var = jnp.mean(cen * cen, axis=-1, keepdims=True)
    cn = cen * lax.rsqrt(var + EPS) * lng_ref[...] + lnb_ref[...]
    mix_scr[:, HGRN_WIDTH:HGRN_WIDTH + CONV_CH] = jax.nn.silu(cn).astype(BF16)
    new_cache = upad_scr[tb:tb + CACHE_PAD, :]
    upad_scr[0:CACHE_PAD, :] = new_cache
    for _ in route:
        pass

    x1 = x + _dot(mix_scr[...], wout_ref[...])
    x1_ref[...] = x1
    xn = _rms(x1, nf_ref[...])
    xh = xn.astype(BF16)
    carry_scr[0] = xh
    carry_scr[1] = (xn - xh.astype(F32)).astype(BF16)

    @pl.when((s < nblk) & (j == nj - 1))
    def _fin():
        for h in range(N_HEADS):
            sout_ref[0, h] = st_scr[h].T
        cout_ref[0] = upad_scr[tb + first:tb + CACHE_PAD, :]


def _route_block(xh, xl, wrh_ref, wrl_ref, br_ref, before_ref, lower_ref, xs_ref, slot_ref, gate_ref, cnt_ref, *, tb):
    wrh = wrh_ref[...]
    logits = _dot_nt(wrh, xh) + _dot_nt(wrh, xl) + _dot_nt(wrl_ref[...], xh) + br_ref[...]
    yield

    eidx = lax.broadcasted_iota(I32, (N_EXPERTS, tb), 0).astype(F32)
    vals, hots = [], []
    work = logits
    for _ in range(TOP_K):
        mx = jnp.max(work, axis=0, keepdims=True)
        pick = jnp.min(jnp.where(work == mx, eidx, float(N_EXPERTS)), axis=0, keepdims=True)
        hot = eidx == pick
        vals.append(mx)
        hots.append(hot)
        work = jnp.where(hot, -jnp.inf, work)
        yield
    exps = [jnp.exp(val - vals[0]) for val in vals]
    inv = 1.0 / (exps[0] + exps[1] + exps[2] + exps[3])
    gate_ref[0] = jnp.concatenate([e * inv for e in exps], axis=0)

    before = before_ref[...]
    lower = lower_ref[...]
    hot_f = [hot.astype(F32) for hot in hots]
    tots = [jnp.sum(hf, axis=1, keepdims=True) for hf in hot_f]
    cnt = tots[0] + tots[1] + tots[2] + tots[3]
    cnt_ref[0] = cnt.astype(I32)
    start = _dot(lower, jnp.broadcast_to(cnt, (N_EXPERTS, LANES)).astype(BF16))[:, 0:1]
    yield
    slots = []
    base = start
    for k in range(TOP_K):
        pre = _dot(hot_f[k].astype(BF16), before)
        slots.append(jnp.sum(hot_f[k] * (base + pre), axis=0, keepdims=True))
        base = base + tots[k]
        yield
    slot_ref[0] = jnp.concatenate(slots, axis=0).astype(I32)

    srow = lax.broadcasted_iota(I32, (TOP_K * tb, tb), 0).astype(F32)
    perm = (srow == slots[0])
    for k in range(1, TOP_K):
        perm = perm | (srow == slots[k])
    yield
    xs = _dot(perm.astype(BF16), xh)
    yield
    _to_row_tiles(xs_ref, xs)


def _mix_stage(x, s0, c0, w, *, tb, chunk):
    bsz, t, _ = x.shape
    nj = t // tb
    nblk = bsz * nj

    def full(a):
        return pl.BlockSpec(a.shape, lambda s: (0,) * a.ndim)

    def cur(s):
        sc = jnp.minimum(s, nblk - 1)
        return sc // nj, sc % nj

    def prev(s):
        return jnp.maximum(s - 1, 0)

    weights = [w["norm_mix"], w["w_in"], w["lb"], w["hgrn_norm"], w["conv_w"], w["conv_b"], w["ln_g"],
               w["ln_b"], w["w_out"], w["norm_ffn"], w["wr_hi"], w["wr_lo"], w["b_router"]]
    r_t = lax.broadcasted_iota(I32, (tb, tb), 0)
    c_t = lax.broadcasted_iota(I32, (tb, tb), 1)
    r_e = lax.broadcasted_iota(I32, (N_EXPERTS, N_EXPERTS), 0)
    c_e = lax.broadcasted_iota(I32, (N_EXPERTS, N_EXPERTS), 1)
    weights += [((r_t >= c_t) & ((r_t ^ c_t) < chunk)).astype(BF16), (r_t < c_t).astype(BF16),
                (c_e < r_e).astype(BF16), jnp.ones((HEAD_DIM, HEAD_DIM), BF16)]
    in_specs = [
        pl.BlockSpec((1, tb, D_MODEL), lambda s: (*cur(s), 0)),
        pl.BlockSpec((1, N_HEADS, HEAD_DIM, HEAD_DIM), lambda s: (cur(s)[0], 0, 0, 0)),
        pl.BlockSpec((1, CACHE_PAD, CONV_CH), lambda s: (cur(s)[0], 0, 0)),
    ] + [full(a) for a in weights]
    out_shape = (
        jax.ShapeDtypeStruct(((nblk + 1) * tb, D_MODEL), F32),
        jax.ShapeDtypeStruct((TOP_K * bsz * t * SUBLANES, LANES), F32),
        jax.ShapeDtypeStruct((bsz, TOP_K, t), I32),
        jax.ShapeDtypeStruct((bsz, TOP_K, t), F32),
        jax.ShapeDtypeStruct((nblk, N_EXPERTS, 1), I32),
        jax.ShapeDtypeStruct((bsz, N_HEADS, HEAD_DIM, HEAD_DIM), F32),
        jax.ShapeDtypeStruct((bsz, CACHE_ROWS, CONV_CH), F32),
    )
    out_specs = (
        pl.BlockSpec((tb, D_MODEL), lambda s: (s, 0)),
        pl.BlockSpec((TOP_K * tb * SUBLANES, LANES), lambda s: (prev(s), 0)),
        pl.BlockSpec((1, TOP_K, tb), lambda s: (prev(s) // nj, 0, prev(s) % nj)),
        pl.BlockSpec((1, TOP_K, tb), lambda s: (prev(s) // nj, 0, prev(s) % nj)),
        pl.BlockSpec((1, N_EXPERTS, 1), lambda s: (prev(s), 0, 0)),
        pl.BlockSpec((1, N_HEADS, HEAD_DIM, HEAD_DIM), lambda s: (cur(s)[0], 0, 0, 0)),
        pl.BlockSpec((1, CACHE_ROWS, CONV_CH), lambda s: (cur(s)[0], 0, 0)),
    )
    return pl.pallas_call(
        functools.partial(_mix_body, tb=tb, chunk=chunk, nj=nj, nblk=nblk),
        grid=(nblk + 1,),
        in_specs=in_specs,
        out_specs=out_specs,
        out_shape=out_shape,
        scratch_shapes=[
            pltpu.VMEM((N_HEADS, HEAD_DIM, HEAD_DIM), F32),
            pltpu.VMEM((CACHE_PAD + tb, CONV_CH), F32),
            pltpu.VMEM((2, SUBLANES + chunk, HEAD_DIM), F32),
            pltpu.VMEM((tb, D_MODEL), BF16),
            pltpu.VMEM((2, tb, D_MODEL), BF16),
        ],
        compiler_params=pltpu.CompilerParams(
            dimension_semantics=("arbitrary",), vmem_limit_bytes=VMEM_LIMIT_BYTES),
        name="mix",
    )(x, s0, c0, *weights)


def _moe_body(te_ref, tlo_ref, tv_ref, glo_ref, ghi_ref, pre_ref, cnt_ref, src_ref,
              xs_p_hbm, xs_s_hbm, zeros_hbm, wgu_ref, bgu_ref, wd_ref, bd_ref, y_ref,
              xbuf, wgu_b, wd_b, gsem, *, tm, nt, nb, nb_p, max_run_p, max_run_s):
    i = pl.program_id(0)
    slot = lax.rem(i, 2)

    def issue_tile(t, s):
        e = te_ref[t]
        lo = tlo_ref[t]
        nv = tv_ref[t]
        hi = lo + nv
        dst0 = s * tm

        def runs(xs_hbm, max_run):
            def body(g, carry):
                p0 = pre_ref[e * nb + g]
                a = jnp.maximum(p0, lo)
                b = jnp.minimum(p0 + cnt_ref[e * nb + g], hi)
                _copy_rows(xs_hbm, src_ref[e * nb + g] + (a - p0), xbuf, dst0 + (a - lo), jnp.maximum(b - a, 0),
                           gsem.at[s], max_run)
                return carry
            return body
        g_lo, g_hi = glo_ref[t], ghi_ref[t]
        lax.fori_loop(g_lo, jnp.minimum(g_hi, nb_p), runs(xs_p_hbm, max_run_p), 0)
        lax.fori_loop(jnp.maximum(g_lo, nb_p), g_hi, runs(xs_s_hbm, max_run_s), 0)
        _copy_rows(zeros_hbm, 0, xbuf, dst0 + nv, tm - nv, gsem.at[s], tm)

    @pl.when(i == 0)
    def _prologue():
        issue_tile(0, 0)

    @pl.when(i + 1 < nt)
    def _prefetch():
        issue_tile(i + 1, 1 - slot)

    used = tv_ref[i] > 0
    new_expert = (i == 0) | (te_ref[i] != te_ref[jnp.maximum(i - 1, 0)])

    @pl.when(used & new_expert)
    def _cast_weights():
        rows = 128

        def cast_gu(c, carry):
            r0 = pl.multiple_of(c * rows, rows)
            wgu_b[pl.ds(r0, rows), :] = wgu_ref[pl.ds(r0, rows), :].astype(BF16)
            return carry
        lax.fori_loop(0, D_MODEL // rows, cast_gu, 0)

        def cast_d(c, carry):
            r0 = pl.multiple_of(c * rows, rows)
            wd_b[pl.ds(r0, rows), :] = wd_ref[pl.ds(r0, rows), :].astype(BF16)
            return carry
        lax.fori_loop(0, D_FF // rows, cast_d, 0)

    tile = pl.ds(pl.multiple_of(slot * tm * SUBLANES, tm * SUBLANES), tm * SUBLANES)
    pltpu.make_async_copy(zeros_hbm, xbuf.at[tile], gsem.at[slot]).wait()

    @pl.when(used)
    def _compute():
        xb = _from_row_tiles(xbuf, tm, base=slot * tm)
        gu = _dot(xb, wgu_b[...]) + bgu_ref[...]
        gate = jnp.minimum(gu[:, :D_FF], SWIGLU_LIMIT)
        up = jnp.clip(gu[:, D_FF:], -SWIGLU_LIMIT, SWIGLU_LIMIT)
        hid = (up + 1.0) * (gate * jax.nn.sigmoid(SWIGLU_ALPHA * gate))
        _to_row_tiles(y_ref, _dot(hid.astype(BF16), wd_b[...]) + bd_ref[...])

    @pl.when(jnp.logical_not(used))
    def _unused_tile():
        y_ref[...] = jnp.zeros((tm * SUBLANES, LANES), F32)


def _moe_stage(xs_p, xs_s, zeros, tabs, w_gu, b_gu, w_d, b_d, *, tm, nt, nb, nb_p, max_run_p, max_run_s):
    n_pref = len(tabs)

    def wmap(i, te, *_):
        return (te[i], 0, 0)

    grid_spec = pltpu.PrefetchScalarGridSpec(
        num_scalar_prefetch=n_pref,
        grid=(nt,),
        in_specs=[
            pl.BlockSpec(memory_space=pl.ANY),
            pl.BlockSpec(memory_space=pl.ANY),
            pl.BlockSpec(memory_space=pl.ANY),
            pl.BlockSpec((None, D_MODEL, 2 * D_FF), wmap),
            pl.BlockSpec((None, 1, 2 * D_FF), wmap),
            pl.BlockSpec((None, D_FF, D_MODEL), wmap),
            pl.BlockSpec((None, 1, D_MODEL), wmap),
        ],
        out_specs=pl.BlockSpec((tm * SUBLANES, LANES), lambda i, *_: (i, 0)),
        scratch_shapes=[
            pltpu.VMEM((2 * tm * SUBLANES, LANES), F32),
            pltpu.VMEM((D_MODEL, 2 * D_FF), BF16),
            pltpu.VMEM((D_FF, D_MODEL), BF16),
            pltpu.SemaphoreType.DMA((2,)),
        ],
    )
    return pl.pallas_call(
        functools.partial(_moe_body, tm=tm, nt=nt, nb=nb, nb_p=nb_p, max_run_p=max_run_p, max_run_s=max_run_s),
        grid_spec=grid_spec,
        out_shape=jax.ShapeDtypeStruct((nt * tm * SUBLANES, LANES), F32),
        compiler_params=pltpu.CompilerParams(
            dimension_semantics=("arbitrary",), vmem_limit_bytes=VMEM_LIMIT_BYTES),
        name="moe",
    )(*tabs, xs_p, xs_s, zeros, w_gu, b_gu, w_d, b_d)


def _combine_body(pre_ref, cnt_ref, lst_ref, ys0_ref, y_hbm, x1_ref, slot_ref, g_ref, pe_ref, np_ref, wpg_ref,
                  wple_ref, nfin_ref, out_ref, stag, sem, *, tc, nsteps, nb, g0, blk_per_step, rows_per_blk,
                  max_run):
    i = pl.program_id(0)
    slot = lax.rem(i, 2)
    rows = TOP_K * tc

    def issue_step(step, s):
        for bi in range(blk_per_step):
            g = g0 + step * blk_per_step + bi
            dst0 = s * rows + bi * rows_per_blk

            def body(e, carry):
                src = ys0_ref[e] + pre_ref[e * nb + g]
                _copy_rows(y_hbm, src, stag, dst0 + lst_ref[g * N_EXPERTS + e], cnt_ref[e * nb + g],
                           sem.at[s], max_run)
                return carry
            lax.fori_loop(0, N_EXPERTS, body, 0)

    @pl.when(i == 0)
    def _prologue():
        issue_step(0, 0)

    @pl.when(i + 1 < nsteps)
    def _prefetch():
        issue_step(i + 1, 1 - slot)

    tile = pl.ds(pl.multiple_of(slot * rows * SUBLANES, rows * SUBLANES), rows * SUBLANES)
    pltpu.make_async_copy(y_hbm.at[pl.ds(0, rows * SUBLANES)], stag.at[tile], sem.at[slot]).wait()

    scol = lax.broadcasted_iota(I32, (tc, rows), 1)
    sl = slot_ref[...]
    g = g_ref[...]
    back = jnp.zeros((tc, rows), F32)
    for k in range(TOP_K):
        back = jnp.where(scol == sl[:, k:k + 1], g[:, k:k + 1], back)
    x2 = x1_ref[...] + _dot(back.astype(BF16), _from_row_tiles(stag, rows, base=slot * rows))
    hp = _rms(x2, np_ref[...]).astype(BF16)
    gate = jax.nn.sigmoid(_dot(hp, wpg_ref[...]))
    emb = _dot(pe_ref[...].astype(BF16), wple_ref[...])
    x3 = x2 + gate * emb
    out_ref[...] = _rms(x3, nfin_ref[...])


def _combine_stage(x1, ys, slots_t, gates_t, pe, tabs, w, *, tc, nb, g0, blk_per_step, rows_per_blk, max_run):
    n = pe.shape[0]
    nsteps = n // tc

    def full(a):
        return pl.BlockSpec(a.shape, lambda i, *_: (0,) * a.ndim)

    weights = [w["norm_ple"], w["w_ple_gate"], w["w_ple"], w["norm_final"]]
    grid_spec = pltpu.PrefetchScalarGridSpec(
        num_scalar_prefetch=len(tabs),
        grid=(nsteps,),
        in_specs=[
            pl.BlockSpec(memory_space=pl.ANY),
            pl.BlockSpec((tc, D_MODEL), lambda i, *_: (i, 0)),
            pl.BlockSpec((tc, TOP_K), lambda i, *_: (i, 0)),
            pl.BlockSpec((tc, TOP_K), lambda i, *_: (i, 0)),
            pl.BlockSpec((tc, PLE_DIM), lambda i, *_: (i, 0)),
        ] + [full(a) for a in weights],
        out_specs=pl.BlockSpec((tc, D_MODEL), lambda i, *_: (i, 0)),
        scratch_shapes=[
            pltpu.VMEM((2 * TOP_K * tc * SUBLANES, LANES), F32),
            pltpu.SemaphoreType.DMA((2,)),
        ],
    )
    return pl.pallas_call(
        functools.partial(_combine_body, tc=tc, nsteps=nsteps, nb=nb, g0=g0, blk_per_step=blk_per_step,
                          rows_per_blk=rows_per_blk, max_run=max_run),
        grid_spec=grid_spec,
        out_shape=jax.ShapeDtypeStruct((n, D_MODEL), F32),
        compiler_params=pltpu.CompilerParams(
            dimension_semantics=("arbitrary",), vmem_limit_bytes=VMEM_LIMIT_BYTES),
        name="combine",
    )(*tabs, ys, x1, slots_t, gates_t, pe, *weights)


def _dispatch_tables(cnt, blk_row0, tm, nt):
    nb = cnt.shape[0]
    pre = jnp.cumsum(cnt, axis=0) - cnt
    total = jnp.sum(cnt, axis=0)
    lst = jnp.cumsum(cnt, axis=1) - cnt
    src = blk_row0[:, None] + lst
    n_tiles_e = (total + tm - 1) // tm
    tile_end = jnp.cumsum(n_tiles_e)
    tile_start = tile_end - n_tiles_e
    tiles = jnp.arange(nt, dtype=I32)
    tile_e_raw = jnp.sum((tiles[:, None] >= tile_end[None, :]).astype(I32), axis=1)
    used = tile_e_raw < N_EXPERTS
    last_e = jnp.max(jnp.where(total > 0, jnp.arange(N_EXPERTS, dtype=I32), 0))
    tile_e = jnp.where(used, jnp.minimum(tile_e_raw, N_EXPERTS - 1), last_e).astype(I32)
    tile_lo = jnp.where(used, (tiles - tile_start[tile_e]) * tm, 0).astype(I32)
    tile_valid = jnp.where(used, jnp.clip(total[tile_e] - tile_lo, 0, tm), 0).astype(I32)
    p_t = pre.T[tile_e]
    c_t = cnt.T[tile_e]
    meets = (p_t < (tile_lo + tile_valid)[:, None]) & ((p_t + c_t) > tile_lo[:, None]) & (c_t > 0)
    gidx = jnp.arange(nb, dtype=I32)[None, :]
    g_lo = jnp.min(jnp.where(meets, gidx, nb), axis=1).astype(I32)
    g_hi = jnp.max(jnp.where(meets, gidx + 1, 0), axis=1).astype(I32)
    g_lo = jnp.minimum(g_lo, g_hi)
    flat = lambda a: a.T.reshape(-1).astype(I32)
    return dict(tile_e=tile_e, tile_lo=tile_lo, tile_valid=tile_valid, g_lo=g_lo, g_hi=g_hi,
                pre=flat(pre), cnt=flat(cnt), src=flat(src), lst=lst.reshape(-1).astype(I32),
                ys0=(tile_start * tm).astype(I32))


def kernel(x_prompt, x_sample, state_hgrn, cache_conv, p_prompt, p_sample, norm_mix, w_in, hgrn_lb_logits, hgrn_norm, conv_w, conv_b, conv_ln_g, conv_ln_b, w_out, norm_ffn, w_router, b_router, w_gate_up, b_gate_up, w_down, b_down, w_ple, norm_ple, w_ple_gate, norm_final):
    depth = norm_mix.shape[0]
    assert depth == 1
    bp, tp, _ = x_prompt.shape
    bs, ts, _ = x_sample.shape
    n_p, n_s = bp * tp, bs * ts
    n_tok = n_p + n_s

    lb_all = jnp.cumsum(jax.nn.softmax(hgrn_lb_logits.astype(F32), axis=0), axis=0)
    wr_t = w_router[0].T
    wr_hi = wr_t.astype(BF16)
    w = {
        "norm_mix": norm_mix[0][None, :], "w_in": w_in[0].astype(BF16), "lb": lb_all[0][None, :],
        "hgrn_norm": hgrn_norm[0][None, :], "conv_w": conv_w[0], "conv_b": conv_b[0][None, :],
        "ln_g": conv_ln_g[0][None, :], "ln_b": conv_ln_b[0][None, :], "w_out": w_out[0].astype(BF16),
        "norm_ffn": norm_ffn[0][None, :], "wr_hi": wr_hi, "wr_lo": (wr_t - wr_hi.astype(F32)).astype(BF16),
        "b_router": b_router[0][:, None],
        "norm_ple": norm_ple[0][None, :], "w_ple_gate": w_ple_gate[0].astype(BF16),
        "w_ple": w_ple[0].astype(BF16), "norm_final": norm_final[None, :],
    }

    pad = ((0, 0), (CACHE_PAD - CACHE_ROWS, 0), (0, 0))
    s0_p = jnp.zeros((bp, N_HEADS, HEAD_DIM, HEAD_DIM), F32)
    c0_p = jnp.zeros((bp, CACHE_PAD, CONV_CH), F32)
    c0_s = jnp.pad(cache_conv[0], pad)

    tb_p = 256
    x1_p, xs_p, slot_p, gate_p, cnt_p, st_p, cc_p = _mix_stage(x_prompt, s0_p, c0_p, w, tb=tb_p, chunk=128)
    x1_s, xs_s, slot_s, gate_s, cnt_s, st_s, cc_s = _mix_stage(x_sample, state_hgrn[0], c0_s, w, tb=ts, chunk=ts)

    nb_p, nb_s = n_p // tb_p, bs
    nb = nb_p + nb_s
    cnt = jnp.concatenate([cnt_p[:, :, 0], cnt_s[:, :, 0]], axis=0)
    blk_row0 = jnp.concatenate([jnp.arange(nb_p, dtype=I32) * (TOP_K * tb_p),
                                jnp.arange(nb_s, dtype=I32) * (TOP_K * ts)])
    tm = 256
    nt = -(-(TOP_K * n_tok) // tm) + N_EXPERTS
    t = _dispatch_tables(cnt, blk_row0, tm, nt)

    zeros = jnp.zeros((tm * SUBLANES, LANES), F32)
    moe_tabs = (t["tile_e"], t["tile_lo"], t["tile_valid"], t["g_lo"], t["g_hi"], t["pre"], t["cnt"], t["src"])
    ys = _moe_stage(xs_p, xs_s, zeros, moe_tabs, w_gate_up[0], b_gate_up[0][:, None, :], w_down[0],
                    b_down[0][:, None, :], tm=tm, nt=nt, nb=nb, nb_p=nb_p, max_run_p=tb_p, max_run_s=ts)

    def token_cols(a):
        return a.transpose(0, 2, 1).reshape(-1, TOP_K)

    comb_tabs = (t["pre"], t["cnt"], t["lst"], t["ys0"])
    y_p = _combine_stage(x1_p, ys, token_cols(slot_p), token_cols(gate_p), p_prompt[0].reshape(n_p, PLE_DIM),
                         comb_tabs, w, tc=tb_p, nb=nb, g0=0, blk_per_step=1, rows_per_blk=TOP_K * tb_p,
                         max_run=tb_p)
    slot_s_step = slot_s + (jnp.arange(bs, dtype=I32) * (TOP_K * ts))[:, None, None]
    y_s = _combine_stage(x1_s, ys, token_cols(slot_s_step), token_cols(gate_s), p_sample[0].reshape(n_s, PLE_DIM),
                         comb_tabs, w, tc=n_s, nb=nb, g0=nb_p, blk_per_step=nb_s, rows_per_blk=TOP_K * ts,
                         max_run=ts)

    return (y_p.reshape(bp, tp, D_MODEL), y_s.reshape(bs, ts, D_MODEL),
            st_p[None], cc_p[None], st_s[None], cc_s[None])
```

```python
import functools

import jax
import jax.numpy as jnp
from jax import lax
from jax.experimental import pallas as pl
from jax.experimental.pallas import tpu as pltpu

D_MODEL = 1024
HGRN_WIDTH = 512
HEAD_DIM = 128
N_HEADS = HGRN_WIDTH // HEAD_DIM
CONV_CH = 512
CONV_WIDTH = 31
CACHE_ROWS = CONV_WIDTH - 1
CACHE_PAD = 32
IN_COLS = 4 * HGRN_WIDTH + 2 * CONV_CH
N_EXPERTS = 32
TOP_K = 4
D_FF = 1024
SWIGLU_LIMIT = 7.0
SWIGLU_ALPHA = 1.702
PLE_DIM = 256
EPS = 1e-6

SUBLANES = 8
LANES = 128
VMEM_LIMIT_BYTES = 56 * 1024 * 1024

F32 = jnp.float32
BF16 = jnp.bfloat16
I32 = jnp.int32


def _split3(a):
    p1 = a.astype(BF16)
    r1 = a - p1.astype(F32)
    p2 = r1.astype(BF16)
    r2 = r1 - p2.astype(F32)
    return p1, p2, r2.astype(BF16)


def _dot(a, b):
    return jnp.dot(a, b, preferred_element_type=F32)


def _dot_nt(a, b):
    return lax.dot_general(a, b, (((1,), (1,)), ((), ())), preferred_element_type=F32)


def _dot_tn(a, b):
    return lax.dot_general(a, b, (((0,), (0,)), ((), ())), preferred_element_type=F32)


def _rms(x, g):
    return x * lax.rsqrt(jnp.mean(x * x, axis=-1, keepdims=True) + EPS) * g


def _copy_rows(src_hbm, src_row, dst_vmem, dst_row, n, sem):
    @pl.when(n > 0)
    def _():
        size = pl.multiple_of(n * SUBLANES, SUBLANES)
        pltpu.make_async_copy(src_hbm.at[pl.ds(pl.multiple_of(src_row * SUBLANES, SUBLANES), size)],
                              dst_vmem.at[pl.ds(pl.multiple_of(dst_row * SUBLANES, SUBLANES), size)], sem).start()


def _to_row_tiles(ref, x, base=0):
    rows = x.shape[0]
    for s in range(D_MODEL // LANES):
        ref[pl.ds(base * SUBLANES + s, rows, stride=SUBLANES), :] = x[:, s * LANES:(s + 1) * LANES]


def _from_row_tiles(ref, rows, base=0, dtype=BF16):
    return jnp.concatenate(
        [ref[pl.ds(base * SUBLANES + s, rows, stride=SUBLANES), :].astype(dtype) for s in range(D_MODEL // LANES)],
        axis=1)


def _hgrn_chunk(q, kk, f, b, v, st, shift_ref, ones_sq):
    c = q.shape[0]
    row = lax.broadcasted_iota(I32, (c, c), 0)
    col = lax.broadcasted_iota(I32, (c, c), 1)
    xor = row ^ col
    diff = row - col

    sc = jnp.zeros((c, c), F32)
    m = c // 2
    while m >= SUBLANES:
        pieces = []
        for blk in range(c // (2 * m)):
            r = blk * 2 * m + m - 1
            pieces.append(jnp.broadcast_to(b[r:r + 1, :], (2 * m, HEAD_DIM)))
        bmid = pieces[0] if len(pieces) == 1 else jnp.concatenate(pieces, axis=0)
        e_m = jnp.exp(-jnp.abs(b - bmid))
        qh = (q * e_m).astype(BF16)
        kh = (kk * e_m).astype(BF16)
        r_m = _dot_nt(qh, kh)
        sc = jnp.where((xor >= m) & (xor < 2 * m), r_m, sc)
        m //= 2
    sc = jnp.where(diff > 0, sc, 0.0)

    shift_ref[0, SUBLANES:SUBLANES + c, :] = f
    shift_ref[1, SUBLANES:SUBLANES + c, :] = kk
    same_blk = xor < SUBLANES
    decay = None
    for d in range(SUBLANES):
        if d == 0:
            kd = kk
        else:
            f_sh = shift_ref[0, SUBLANES - (d - 1):SUBLANES - (d - 1) + c, :]
            decay = f_sh if decay is None else decay * f_sh
            kd = shift_ref[1, SUBLANES - d:SUBLANES - d + c, :] * decay
        s_d = _dot((q * kd).astype(BF16), ones_sq)
        sc = jnp.where(same_blk & (diff == d), s_d[:, :c], sc)

    o = _dot(sc.astype(BF16), v.astype(BF16))
    o = o + _dot_nt((q * jnp.exp(b)).astype(BF16), st.astype(BF16))
    b_last = b[c - 1:c, :]
    kdec = (kk * jnp.exp(b_last - b)).astype(BF16)
    st_new = st * jnp.exp(b_last) + _dot_tn(v.astype(BF16), kdec)
    return o, st_new


def _mix_body(x_ref, s0_ref, c0_ref, nm_ref, win_ref, lb_ref, hn_ref, cw_ref, cb_ref, lng_ref,
              lnb_ref, wout_ref, nf_ref, wrh_ref, wrl_ref, br_ref, tri_ref, before_ref, lower_ref, ones_ref,
              x1_ref, xs_ref, slot_ref, gate_ref, cnt_ref, sout_ref, cout_ref,
              st_scr, upad_scr, shift_scr, mix_scr, carry_scr, *, tb, chunk, nj, nblk):
    s = pl.program_id(0)
    j = lax.rem(jnp.minimum(s, nblk - 1), nj)

    @pl.when(s == 0)
    def _():
        carry_scr[...] = jnp.zeros((2, tb, D_MODEL), BF16)

    @pl.when(j == 0)
    def _init():
        for h in range(N_HEADS):
            st_scr[h] = s0_ref[0, h].T
        upad_scr[0:CACHE_PAD, :] = c0_ref[0]
        shift_scr[:, 0:SUBLANES, :] = jnp.zeros((2, SUBLANES, HEAD_DIM), F32)

    route = _route_block(carry_scr[0], carry_scr[1], wrh_ref, wrl_ref, br_ref, before_ref, lower_ref,
                         xs_ref, slot_ref, gate_ref, cnt_ref, tb=tb)

    def advance_route(n=1):
        for _ in range(n):
            next(route, None)

    advance_route()
    x = x_ref[0]
    h_in = _rms(x, nm_ref[...]).astype(BF16)
    proj = _dot(h_in, win_ref[...])
    advance_route(2)

    q_all = proj[:, 0:HGRN_WIDTH]
    fz = proj[:, HGRN_WIDTH:2 * HGRN_WIDTH]
    v_all = proj[:, 2 * HGRN_WIDTH:3 * HGRN_WIDTH]
    g_all = proj[:, 3 * HGRN_WIDTH:4 * HGRN_WIDTH]
    lb = lb_ref[...]
    f_all = lb + (1.0 - lb) * jax.nn.sigmoid(fz)
    lf_all = jnp.log(f_all)
    kk_all = 1.0 - f_all

    tri = tri_ref[...]
    p1, p2, p3 = _split3(lf_all)
    b_all = _dot(tri, p1) + _dot(tri, p2) + _dot(tri, p3)
    advance_route(2)

    ones_sq = ones_ref[...]
    hn = hn_ref[...]
    for h in range(N_HEADS):
        ls = slice(h * HEAD_DIM, (h + 1) * HEAD_DIM)
        st = st_scr[h]
        for ci in range(tb // chunk):
            rs = slice(ci * chunk, (ci + 1) * chunk)
            o, st = _hgrn_chunk(q_all[rs, ls], kk_all[rs, ls], f_all[rs, ls], b_all[rs, ls],
                                v_all[rs, ls], st, shift_scr, ones_sq)
            o = _rms(o, hn[:, ls])
            mix_scr[rs, ls] = (o * jax.nn.silu(g_all[rs, ls])).astype(BF16)
            advance_route()
        st_scr[h] = st

    u = proj[:, 4 * HGRN_WIDTH:4 * HGRN_WIDTH + CONV_CH] * jax.nn.sigmoid(
        proj[:, 4 * HGRN_WIDTH + CONV_CH:IN_COLS])
    upad_scr[CACHE_PAD:CACHE_PAD + tb, :] = u
    first = CACHE_PAD - CACHE_ROWS
    acc = jnp.zeros((tb, CONV_CH), F32) + cb_ref[...]
    for tap in range(CONV_WIDTH):
        acc = acc + cw_ref[tap:tap + 1, :] * upad_scr[first + tap:first + tap + tb, :]
    mu = jnp.mean(acc, axis=-1, keepdims=True)
    cen = acc - mu
    var = jnp.mean(cen * cen, axis=-1, keepdims=True)
    cn = cen * lax.rsqrt(var + EPS) * lng_ref[...] + lnb_ref[...]
    mix_scr[:, HGRN_WIDTH:HGRN_WIDTH + CONV_CH] = jax.nn.silu(cn).astype(BF16)
    new_cache = upad_scr[tb:tb + CACHE_PAD, :]
    upad_scr[0:CACHE_PAD, :] = new_cache
    for _ in route:
        pass

    x1 = x + _dot(mix_scr[...], wout_ref[...])
    x1_ref[...] = x1
    xn = _rms(x1, nf_ref[...])
    xh = xn.astype(BF16)
    carry_scr[0] = xh
    carry_scr[1] = (xn - xh.astype(F32)).astype(BF16)

    @pl.when((s < nblk) & (j == nj - 1))
    def _fin():
        for h in range(N_HEADS):
            sout_ref[0, h] = st_scr[h].T
        cout_ref[0] = upad_scr[tb + first:tb + CACHE_PAD, :]


def _route_block(xh, xl, wrh_ref, wrl_ref, br_ref, before_ref, lower_ref, xs_ref, slot_ref, gate_ref, cnt_ref, *, tb):
    wrh = wrh_ref[...]
    logits = _dot_nt(wrh, xh) + _dot_nt(wrh, xl) + _dot_nt(wrl_ref[...], xh) + br_ref[...]
    yield

    eidx = lax.broadcasted_iota(I32, (N_EXPERTS, tb), 0).astype(F32)
    vals, hots = [], []
    work = logits
    for _ in range(TOP_K):
        mx = jnp.max(work, axis=0, keepdims=True)
        pick = jnp.min(jnp.where(work == mx, eidx, float(N_EXPERTS)), axis=0, keepdims=True)
        hot = eidx == pick
        vals.append(mx)
        hots.append(hot)
        work = jnp.where(hot, -jnp.inf, work)
        yield
    exps = [jnp.exp(val - vals[0]) for val in vals]
    inv = 1.0 / (exps[0] + exps[1] + exps[2] + exps[3])
    gate_ref[0] = jnp.concatenate([e * inv for e in exps], axis=0)

    before = before_ref[...]
    lower = lower_ref[...]
    hot_f = [hot.astype(F32) for hot in hots]
    tots = [jnp.sum(hf, axis=1, keepdims=True) for hf in hot_f]
    cnt = tots[0] + tots[1] + tots[2] + tots[3]
    cnt_ref[0] = cnt.astype(I32)
    start = _dot(lower, jnp.broadcast_to(cnt, (N_EXPERTS, LANES)).astype(BF16))[:, 0:1]
    yield
    slots = []
    base = start
    for k in range(TOP_K):
        pre = _dot(hot_f[k].astype(BF16), before)
        slots.append(jnp.sum(hot_f[k] * (base + pre), axis=0, keepdims=True))
        base = base + tots[k]
        yield
    slot_ref[0] = jnp.concatenate(slots, axis=0).astype(I32)

    srow = lax.broadcasted_iota(I32, (TOP_K * tb, tb), 0).astype(F32)
    perm = (srow == slots[0])
    for k in range(1, TOP_K):
        perm = perm | (srow == slots[k])
    yield
    xs = _dot(perm.astype(BF16), xh)
    yield
    _to_row_tiles(xs_ref, xs)


def _mix_stage(x, s0, c0, w, *, tb, chunk):
    bsz, t, _ = x.shape
    nj = t // tb
    nblk = bsz * nj

    def full(a):
        return pl.BlockSpec(a.shape, lambda s: (0,) * a.ndim)

    def cur(s):
        sc = jnp.minimum(s, nblk - 1)
        return sc // nj, sc % nj

    def prev(s):
        return jnp.maximum(s - 1, 0)

    weights = [w["norm_mix"], w["w_in"], w["lb"], w["hgrn_norm"], w["conv_w"], w["conv_b"], w["ln_g"],
               w["ln_b"], w["w_out"], w["norm_ffn"], w["wr_hi"], w["wr_lo"], w["b_router"]]
    r_t = lax.broadcasted_iota(I32, (tb, tb), 0)
    c_t = lax.broadcasted_iota(I32, (tb, tb), 1)
    r_e = lax.broadcasted_iota(I32, (N_EXPERTS, N_EXPERTS), 0)
    c_e = lax.broadcasted_iota(I32, (N_EXPERTS, N_EXPERTS), 1)
    weights += [((r_t >= c_t) & ((r_t ^ c_t) < chunk)).astype(BF16), (r_t < c_t).astype(BF16),
                (c_e < r_e).astype(BF16), jnp.ones((HEAD_DIM, HEAD_DIM), BF16)]
    in_specs = [
        pl.BlockSpec((1, tb, D_MODEL), lambda s: (*cur(s), 0)),
        pl.BlockSpec((1, N_HEADS, HEAD_DIM, HEAD_DIM), lambda s: (cur(s)[0], 0, 0, 0)),
        pl.BlockSpec((1, CACHE_PAD, CONV_CH), lambda s: (cur(s)[0], 0, 0)),
    ] + [full(a) for a in weights]
    out_shape = (
        jax.ShapeDtypeStruct(((nblk + 1) * tb, D_MODEL), F32),
        jax.ShapeDtypeStruct((TOP_K * bsz * t * SUBLANES, LANES), F32),
        jax.ShapeDtypeStruct((bsz, TOP_K, t), I32),
        jax.ShapeDtypeStruct((bsz, TOP_K, t), F32),
        jax.ShapeDtypeStruct((nblk, N_EXPERTS, 1), I32),
        jax.ShapeDtypeStruct((bsz, N_HEADS, HEAD_DIM, HEAD_DIM), F32),
        jax.ShapeDtypeStruct((bsz, CACHE_ROWS, CONV_CH), F32),
    )
    out_specs = (
        pl.BlockSpec((tb, D_MODEL), lambda s: (s, 0)),
        pl.BlockSpec((TOP_K * tb * SUBLANES, LANES), lambda s: (prev(s), 0)),
        pl.BlockSpec((1, TOP_K, tb), lambda s: (prev(s) // nj, 0, prev(s) % nj)),
        pl.BlockSpec((1, TOP_K, tb), lambda s: (prev(s) // nj, 0, prev(s) % nj)),
        pl.BlockSpec((1, N_EXPERTS, 1), lambda s: (prev(s), 0, 0)),
        pl.BlockSpec((1, N_HEADS, HEAD_DIM, HEAD_DIM), lambda s: (cur(s)[0], 0, 0, 0)),
        pl.BlockSpec((1, CACHE_ROWS, CONV_CH), lambda s: (cur(s)[0], 0, 0)),
    )
    return pl.pallas_call(
        functools.partial(_mix_body, tb=tb, chunk=chunk, nj=nj, nblk=nblk),
        grid=(nblk + 1,),
        in_specs=in_specs,
        out_specs=out_specs,
        out_shape=out_shape,
        scratch_shapes=[
            pltpu.VMEM((N_HEADS, HEAD_DIM, HEAD_DIM), F32),
            pltpu.VMEM((CACHE_PAD + tb, CONV_CH), F32),
            pltpu.VMEM((2, SUBLANES + chunk, HEAD_DIM), F32),
            pltpu.VMEM((tb, D_MODEL), BF16),
            pltpu.VMEM((2, tb, D_MODEL), BF16),
        ],
        compiler_params=pltpu.CompilerParams(
            dimension_semantics=("arbitrary",), vmem_limit_bytes=VMEM_LIMIT_BYTES),
        name="mix",
    )(x, s0, c0, *weights)


def _moe_body(te_ref, tlo_ref, tv_ref, glo_ref, ghi_ref, wslot_ref, enext_ref, pre_ref, cnt_ref, src_ref,
              xs_p_hbm, xs_s_hbm, zeros_hbm, wgu_hbm, bgu_ref, wd_hbm, bd_ref, y_ref,
              xbuf, wgu_f, wd_f, wgu_b, wd_b, gsem, wsem, *, tm, nt, nb, nb_p):
    i = pl.program_id(0)
    slot = lax.rem(i, 2)

    def issue_tile(t, s):
        e = te_ref[t]
        lo = tlo_ref[t]
        nv = tv_ref[t]
        hi = lo + nv
        dst0 = s * tm

        def runs(xs_hbm):
            def body(g, carry):
                p0 = pre_ref[e * nb + g]
                a = jnp.maximum(p0, lo)
                b = jnp.minimum(p0 + cnt_ref[e * nb + g], hi)
                _copy_rows(xs_hbm, src_ref[e * nb + g] + (a - p0), xbuf, dst0 + (a - lo), b - a, gsem.at[s])
                return carry
            return body
        g_lo, g_hi = glo_ref[t], ghi_ref[t]
        lax.fori_loop(g_lo, jnp.minimum(g_hi, nb_p), runs(xs_p_hbm), 0)
        lax.fori_loop(jnp.maximum(g_lo, nb_p), g_hi, runs(xs_s_hbm), 0)
        _copy_rows(zeros_hbm, 0, xbuf, dst0 + nv, tm - nv, gsem.at[s])

    def weight_fetch(e, ws):
        return (pltpu.make_async_copy(wgu_hbm.at[e], wgu_f.at[ws], wsem.at[ws]),
                pltpu.make_async_copy(wd_hbm.at[e], wd_f.at[ws], wsem.at[ws]))

    @pl.when(i == 0)
    def _prologue():
        issue_tile(0, 0)
        for c in weight_fetch(te_ref[0], wslot_ref[0]):
            c.start()

    @pl.when(i + 1 < nt)
    def _prefetch():
        issue_tile(i + 1, 1 - slot)

    used = tv_ref[i] > 0
    new_expert = (i == 0) | (te_ref[i] != te_ref[jnp.maximum(i - 1, 0)])

    @pl.when(used & new_expert)
    def _switch_expert():
        ws = wslot_ref[i]
        for c in weight_fetch(0, ws):
            c.wait()
        nxt = enext_ref[i]

        @pl.when(nxt >= 0)
        def _():
            for c in weight_fetch(nxt, 1 - ws):
                c.start()
        rows = 128

        def cast_gu(c, carry):
            r0 = pl.multiple_of(c * rows, rows)
            wgu_b[pl.ds(r0, rows), :] = wgu_f[ws, pl.ds(r0, rows), :].astype(BF16)
            return carry
        lax.fori_loop(0, D_MODEL // rows, cast_gu, 0)

        def cast_d(c, carry):
            r0 = pl.multiple_of(c * rows, rows)
            wd_b[pl.ds(r0, rows), :] = wd_f[ws, pl.ds(r0, rows), :].astype(BF16)
            return carry
        lax.fori_loop(0, D_FF // rows, cast_d, 0)

    tile = pl.ds(pl.multiple_of(slot * tm * SUBLANES, tm * SUBLANES), tm * SUBLANES)
    pltpu.make_async_copy(zeros_hbm, xbuf.at[tile], gsem.at[slot]).wait()

    @pl.when(used)
    def _compute():
        xb = _from_row_tiles(xbuf, tm, base=slot * tm)
        gu = _dot(xb, wgu_b[...]) + bgu_ref[...]
        gate = jnp.minimum(gu[:, :D_FF], SWIGLU_LIMIT)
        up = jnp.clip(gu[:, D_FF:], -SWIGLU_LIMIT, SWIGLU_LIMIT)
        hid = (up + 1.0) * (gate * jax.nn.sigmoid(SWIGLU_ALPHA * gate))
        _to_row_tiles(y_ref, _dot(hid.astype(BF16), wd_b[...]) + bd_ref[...])

    @pl.when(jnp.logical_not(used))
    def _unused_tile():
        y_ref[...] = jnp.zeros((tm * SUBLANES, LANES), F32)


def _moe_stage(xs_p, xs_s, zeros, tabs, w_gu, b_gu, w_d, b_d, *, tm, nt, nb, nb_p):
    n_pref = len(tabs)

    def wmap(i, te, *_):
        return (te[i], 0, 0)

    grid_spec = pltpu.PrefetchScalarGridSpec(
        num_scalar_prefetch=n_pref,
        grid=(nt,),
        in_specs=[
            pl.BlockSpec(memory_space=pl.ANY),
            pl.BlockSpec(memory_space=pl.ANY),
            pl.BlockSpec(memory_space=pl.ANY),
            pl.BlockSpec(memory_space=pl.ANY),
            pl.BlockSpec((None, 1, 2 * D_FF), wmap),
            pl.BlockSpec(memory_space=pl.ANY),
            pl.BlockSpec((None, 1, D_MODEL), wmap),
        ],
        out_specs=pl.BlockSpec((tm * SUBLANES, LANES), lambda i, *_: (i, 0)),
        scratch_shapes=[
            pltpu.VMEM((2 * tm * SUBLANES, LANES), F32),
            pltpu.VMEM((2, D_MODEL, 2 * D_FF), F32),
            pltpu.VMEM((2, D_FF, D_MODEL), F32),
            pltpu.VMEM((D_MODEL, 2 * D_FF), BF16),
            pltpu.VMEM((D_FF, D_MODEL), BF16),
            pltpu.SemaphoreType.DMA((2,)),
            pltpu.SemaphoreType.DMA((2,)),
        ],
    )
    return pl.pallas_call(
        functools.partial(_moe_body, tm=tm, nt=nt, nb=nb, nb_p=nb_p),
        grid_spec=grid_spec,
        out_shape=jax.ShapeDtypeStruct((nt * tm * SUBLANES, LANES), F32),
        compiler_params=pltpu.CompilerParams(
            dimension_semantics=("arbitrary",), vmem_limit_bytes=VMEM_LIMIT_BYTES),
        name="moe",
    )(*tabs, xs_p, xs_s, zeros, w_gu, b_gu, w_d, b_d)


def _combine_body(pre_ref, cnt_ref, lst_ref, ys0_ref, y_hbm, x1_ref, slot_ref, g_ref, pe_ref, np_ref, wpg_ref,
                  wple_ref, nfin_ref, out_ref, stag, sem, *, tc, nsteps, nb, g0, blk_per_step, rows_per_blk):
    i = pl.program_id(0)
    slot = lax.rem(i, 2)
    rows = TOP_K * tc

    def issue_step(step, s):
        for bi in range(blk_per_step):
            g = g0 + step * blk_per_step + bi
            dst0 = s * rows + bi * rows_per_blk

            def body(e, carry):
                src = ys0_ref[e] + pre_ref[e * nb + g]
                _copy_rows(y_hbm, src, stag, dst0 + lst_ref[g * N_EXPERTS + e], cnt_ref[e * nb + g], sem.at[s])
                return carry
            lax.fori_loop(0, N_EXPERTS, body, 0)

    @pl.when(i == 0)
    def _prologue():
        issue_step(0, 0)

    @pl.when(i + 1 < nsteps)
    def _prefetch():
        issue_step(i + 1, 1 - slot)

    tile = pl.ds(pl.multiple_of(slot * rows * SUBLANES, rows * SUBLANES), rows * SUBLANES)
    pltpu.make_async_copy(y_hbm.at[pl.ds(0, rows * SUBLANES)], stag.at[tile], sem.at[slot]).wait()

    scol = lax.broadcasted_iota(I32, (tc, rows), 1)
    sl = slot_ref[...]
    g = g_ref[...]
    back = jnp.zeros((tc, rows), F32)
    for k in range(TOP_K):
        back = jnp.where(scol == sl[:, k:k + 1], g[:, k:k + 1], back)
    x2 = x1_ref[...] + _dot(back.astype(BF16), _from_row_tiles(stag, rows, base=slot * rows))
    hp = _rms(x2, np_ref[...]).astype(BF16)
    gate = jax.nn.sigmoid(_dot(hp, wpg_ref[...]))
    emb = _dot(pe_ref[...].astype(BF16), wple_ref[...])
    x3 = x2 + gate * emb
    out_ref[...] = _rms(x3, nfin_ref[...])


def _combine_stage(x1, ys, slots_t, gates_t, pe, tabs, w, *, tc, nb, g0, blk_per_step, rows_per_blk):
    n = pe.shape[0]
    nsteps = n // tc

    def full(a):
        return pl.BlockSpec(a.shape, lambda i, *_: (0,) * a.ndim)

    weights = [w["norm_ple"], w["w_ple_gate"], w["w_ple"], w["norm_final"]]
    grid_spec = pltpu.PrefetchScalarGridSpec(
        num_scalar_prefetch=len(tabs),
        grid=(nsteps,),
        in_specs=[
            pl.BlockSpec(memory_space=pl.ANY),
            pl.BlockSpec((tc, D_MODEL), lambda i, *_: (i, 0)),
            pl.BlockSpec((tc, TOP_K), lambda i, *_: (i, 0)),
            pl.BlockSpec((tc, TOP_K), lambda i, *_: (i, 0)),
            pl.BlockSpec((tc, PLE_DIM), lambda i, *_: (i, 0)),
        ] + [full(a) for a in weights],
        out_specs=pl.BlockSpec((tc, D_MODEL), lambda i, *_: (i, 0)),
        scratch_shapes=[
            pltpu.VMEM((2 * TOP_K * tc * SUBLANES, LANES), F32),
            pltpu.SemaphoreType.DMA((2,)),
        ],
    )
    return pl.pallas_call(
        functools.partial(_combine_body, tc=tc, nsteps=nsteps, nb=nb, g0=g0, blk_per_step=blk_per_step,
                          rows_per_blk=rows_per_blk),
        grid_spec=grid_spec,
        out_shape=jax.ShapeDtypeStruct((n, D_MODEL), F32),
        compiler_params=pltpu.CompilerParams(
            dimension_semantics=("arbitrary",), vmem_limit_bytes=VMEM_LIMIT_BYTES),
        name="combine",
    )(*tabs, ys, x1, slots_t, gates_t, pe, *weights)


def _dispatch_tables(cnt, blk_row0, tm, nt):
    nb = cnt.shape[0]
    experts = jnp.arange(N_EXPERTS, dtype=I32)
    pre = jnp.cumsum(cnt, axis=0) - cnt
    total = jnp.sum(cnt, axis=0)
    lst = jnp.cumsum(cnt, axis=1) - cnt
    src = blk_row0[:, None] + lst
    n_tiles_e = (total + tm - 1) // tm
    tile_end = jnp.cumsum(n_tiles_e)
    tile_start = tile_end - n_tiles_e
    tiles = jnp.arange(nt, dtype=I32)
    tile_e_raw = jnp.sum((tiles[:, None] >= tile_end[None, :]).astype(I32), axis=1)
    used = tile_e_raw < N_EXPERTS
    has = total > 0
    last_e = jnp.max(jnp.where(has, experts, 0))
    tile_e = jnp.where(used, jnp.minimum(tile_e_raw, N_EXPERTS - 1), last_e).astype(I32)
    hot = tile_e[:, None] == experts[None, :]

    def per_tile(v):
        if v.ndim == 1:
            return jnp.sum(jnp.where(hot, v[None, :], 0), axis=1)
        return jnp.sum(jnp.where(hot[:, :, None], v[None, :, :], 0), axis=1)

    tile_lo = jnp.where(used, (tiles - per_tile(tile_start)) * tm, 0).astype(I32)
    tile_valid = jnp.where(used, jnp.clip(per_tile(total) - tile_lo, 0, tm), 0).astype(I32)
    p_t = per_tile(pre.T)
    c_t = per_tile(cnt.T)
    meets = (p_t < (tile_lo + tile_valid)[:, None]) & ((p_t + c_t) > tile_lo[:, None]) & (c_t > 0)
    gidx = jnp.arange(nb, dtype=I32)[None, :]
    g_lo = jnp.min(jnp.where(meets, gidx, nb), axis=1).astype(I32)
    g_hi = jnp.max(jnp.where(meets, gidx + 1, 0), axis=1).astype(I32)
    g_lo = jnp.minimum(g_lo, g_hi)
    w_slot_e = (jnp.cumsum(has.astype(I32)) - 1) % 2
    later = has[None, :] & (experts[None, :] > experts[:, None])
    e_next = jnp.min(jnp.where(later, experts[None, :], N_EXPERTS), axis=1)
    e_next = jnp.where(e_next >= N_EXPERTS, -1, e_next)
    flat = lambda a: a.T.reshape(-1).astype(I32)
    return dict(tile_e=tile_e, tile_lo=tile_lo, tile_valid=tile_valid, g_lo=g_lo, g_hi=g_hi,
                w_slot=per_tile(w_slot_e).astype(I32), e_next=per_tile(e_next).astype(I32),
                pre=flat(pre), cnt=flat(cnt), src=flat(src), lst=lst.reshape(-1).astype(I32),
                ys0=(tile_start * tm).astype(I32))


def kernel(x_prompt, x_sample, state_hgrn, cache_conv, p_prompt, p_sample, norm_mix, w_in, hgrn_lb_logits, hgrn_norm, conv_w, conv_b, conv_ln_g, conv_ln_b, w_out, norm_ffn, w_router, b_router, w_gate_up, b_gate_up, w_down, b_down, w_ple, norm_ple, w_ple_gate, norm_final):
    depth = norm_mix.shape[0]
    assert depth == 1
    bp, tp, _ = x_prompt.shape
    bs, ts, _ = x_sample.shape
    n_p, n_s = bp * tp, bs * ts
    n_tok = n_p + n_s

    lb_all = jnp.cumsum(jax.nn.softmax(hgrn_lb_logits.astype(F32), axis=0), axis=0)
    wr_t = w_router[0].T
    wr_hi = wr_t.astype(BF16)
    w = {
        "norm_mix": norm_mix[0][None, :], "w_in": w_in[0].astype(BF16), "lb": lb_all[0][None, :],
        "hgrn_norm": hgrn_norm[0][None, :], "conv_w": conv_w[0], "conv_b": conv_b[0][None, :],
        "ln_g": conv_ln_g[0][None, :], "ln_b": conv_ln_b[0][None, :], "w_out": w_out[0].astype(BF16),
        "norm_ffn": norm_ffn[0][None, :], "wr_hi": wr_hi, "wr_lo": (wr_t - wr_hi.astype(F32)).astype(BF16),
        "b_router": b_router[0][:, None],
        "norm_ple": norm_ple[0][None, :], "w_ple_gate": w_ple_gate[0].astype(BF16),
        "w_ple": w_ple[0].astype(BF16), "norm_final": norm_final[None, :],
    }

    pad = ((0, 0), (CACHE_PAD - CACHE_ROWS, 0), (0, 0))
    s0_p = jnp.zeros((bp, N_HEADS, HEAD_DIM, HEAD_DIM), F32)
    c0_p = jnp.zeros((bp, CACHE_PAD, CONV_CH), F32)
    c0_s = jnp.pad(cache_conv[0], pad)

    tb_p = 256
    x1_p, xs_p, slot_p, gate_p, cnt_p, st_p, cc_p = _mix_stage(x_prompt, s0_p, c0_p, w, tb=tb_p, chunk=128)
    x1_s, xs_s, slot_s, gate_s, cnt_s, st_s, cc_s = _mix_stage(x_sample, state_hgrn[0], c0_s, w, tb=ts, chunk=ts)

    nb_p, nb_s = n_p // tb_p, bs
    nb = nb_p + nb_s
    cnt = jnp.concatenate([cnt_p[:, :, 0], cnt_s[:, :, 0]], axis=0)
    blk_row0 = jnp.concatenate([jnp.arange(nb_p, dtype=I32) * (TOP_K * tb_p),
                                jnp.arange(nb_s, dtype=I32) * (TOP_K * ts)])
    tm = 256
    nt = -(-(TOP_K * n_tok) // tm) + N_EXPERTS
    t = _dispatch_tables(cnt, blk_row0, tm, nt)

    zeros = jnp.zeros((tm * SUBLANES, LANES), F32)
    moe_tabs = (t["tile_e"], t["tile_lo"], t["tile_valid"], t["g_lo"], t["g_hi"], t["w_slot"], t["e_next"],
                t["pre"], t["cnt"], t["src"])
    ys = _moe_stage(xs_p, xs_s, zeros, moe_tabs, w_gate_up[0], b_gate_up[0][:, None, :], w_down[0],
                    b_down[0][:, None, :], tm=tm, nt=nt, nb=nb, nb_p=nb_p)

    def token_cols(a):
        return a.transpose(0, 2, 1).reshape(-1, TOP_K)

    comb_tabs = (t["pre"], t["cnt"], t["lst"], t["ys0"])
    y_p = _combine_stage(x1_p, ys, token_cols(slot_p), token_cols(gate_p), p_prompt[0].reshape(n_p, PLE_DIM),
                         comb_tabs, w, tc=tb_p, nb=nb, g0=0, blk_per_step=1, rows_per_blk=TOP_K * tb_p)
    slot_s_step = slot_s + (jnp.arange(bs, dtype=I32) * (TOP_K * ts))[:, None, None]
    y_s = _combine_stage(x1_s, ys, token_cols(slot_s_step), token_cols(gate_s), p_sample[0].reshape(n_s, PLE_DIM),
                         comb_tabs, w, tc=n_s, nb=nb, g0=nb_p, blk_per_step=nb_s, rows_per_blk=TOP_K * ts)

    return (y_p.reshape(bp, tp, D_MODEL), y_s.reshape(bs, ts, D_MODEL),
            st_p[None], cc_p[None], st_s[None], cc_s[None])
```

```python
import functools

import jax
import jax.numpy as jnp
from jax import lax
from jax.experimental import pallas as pl
from jax.experimental.pallas import tpu as pltpu

D_MODEL = 1024
HGRN_WIDTH = 512
HEAD_DIM = 128
N_HEADS = HGRN_WIDTH // HEAD_DIM
CONV_CH = 512
CONV_WIDTH = 31
CACHE_ROWS = CONV_WIDTH - 1
CACHE_PAD = 32
IN_COLS = 4 * HGRN_WIDTH + 2 * CONV_CH
N_EXPERTS = 32
TOP_K = 4
D_FF = 1024
SWIGLU_LIMIT = 7.0
SWIGLU_ALPHA = 1.702
PLE_DIM = 256
EPS = 1e-6

SUBLANES = 8
LANES = 128
VMEM_LIMIT_BYTES = 56 * 1024 * 1024

F32 = jnp.float32
BF16 = jnp.bfloat16
I32 = jnp.int32


def _split3(a):
    p1 = a.astype(BF16)
    r1 = a - p1.astype(F32)
    p2 = r1.astype(BF16)
    r2 = r1 - p2.astype(F32)
    return p1, p2, r2.astype(BF16)


def _dot(a, b):
    return jnp.dot(a, b, preferred_element_type=F32)


def _dot_nt(a, b):
    return lax.dot_general(a, b, (((1,), (1,)), ((), ())), preferred_element_type=F32)


def _dot_tn(a, b):
    return lax.dot_general(a, b, (((0,), (0,)), ((), ())), preferred_element_type=F32)


def _rms(x, g):
    return x * lax.rsqrt(jnp.mean(x * x, axis=-1, keepdims=True) + EPS) * g


def _copy_rows(src_hbm, src_row, dst_vmem, dst_row, n, sem):
    @pl.when(n > 0)
    def _():
        size = pl.multiple_of(n * SUBLANES, SUBLANES)
        pltpu.make_async_copy(src_hbm.at[pl.ds(pl.multiple_of(src_row * SUBLANES, SUBLANES), size)],
                              dst_vmem.at[pl.ds(pl.multiple_of(dst_row * SUBLANES, SUBLANES), size)], sem).start()


def _to_row_tiles(ref, x, base=0):
    rows = x.shape[0]
    for s in range(D_MODEL // LANES):
        ref[pl.ds(base * SUBLANES + s, rows, stride=SUBLANES), :] = x[:, s * LANES:(s + 1) * LANES]


def _from_row_tiles(ref, rows, base=0, dtype=BF16):
    return jnp.concatenate(
        [ref[pl.ds(base * SUBLANES + s, rows, stride=SUBLANES), :].astype(dtype) for s in range(D_MODEL // LANES)],
        axis=1)


def _hgrn_chunk(q, kk, f, b, v, st, shift_ref, ones_sq, pair_ref):
    c = q.shape[0]
    sc = None
    n_mask = 0
    m = c // 2
    while m >= SUBLANES:
        pieces = []
        for blk in range(c // (2 * m)):
            r = blk * 2 * m + m - 1
            pieces.append(jnp.broadcast_to(b[r:r + 1, :], (2 * m, HEAD_DIM)))
        bmid = pieces[0] if len(pieces) == 1 else jnp.concatenate(pieces, axis=0)
        e_m = jnp.exp(-jnp.abs(b - bmid))
        qh = (q * e_m).astype(BF16)
        kh = (kk * e_m).astype(BF16)
        term = pair_ref[n_mask] * _dot_nt(qh, kh)
        sc = term if sc is None else sc + term
        n_mask += 1
        m //= 2

    shift_ref[0, SUBLANES:SUBLANES + c, :] = f
    shift_ref[1, SUBLANES:SUBLANES + c, :] = kk
    decay = None
    for d in range(SUBLANES):
        if d == 0:
            kd = kk
        else:
            f_sh = shift_ref[0, SUBLANES - (d - 1):SUBLANES - (d - 1) + c, :]
            decay = f_sh if decay is None else decay * f_sh
            kd = shift_ref[1, SUBLANES - d:SUBLANES - d + c, :] * decay
        s_d = _dot((q * kd).astype(BF16), ones_sq)
        term = pair_ref[n_mask + d] * s_d[:, :c]
        sc = term if sc is None else sc + term

    o = _dot(sc.astype(BF16), v.astype(BF16))
    o = o + _dot_nt((q * jnp.exp(b)).astype(BF16), st.astype(BF16))
    b_last = b[c - 1:c, :]
    kdec = (kk * jnp.exp(b_last - b)).astype(BF16)
    st_new = st * jnp.exp(b_last) + _dot_tn(v.astype(BF16), kdec)
    return o, st_new


def _mix_body(x_ref, s0_ref, c0_ref, nm_ref, win_ref, lb_ref, hn_ref, cw_ref, cb_ref, lng_ref,
              lnb_ref, wout_ref, nf_ref, wrh_ref, wrl_ref, br_ref, tri_ref, before_ref, lower_ref, ones_ref, pair_ref,
              x1_ref, xs_ref, slot_ref, gate_ref, cnt_ref, sout_ref, cout_ref,
              st_scr, upad_scr, ush_scr, shift_scr, mix_scr, carry_scr, *, tb, chunk, nj, nblk):
    s = pl.program_id(0)
    j = lax.rem(jnp.minimum(s, nblk - 1), nj)

    @pl.when(s == 0)
    def _():
        carry_scr[...] = jnp.zeros((2, tb, D_MODEL), BF16)

    @pl.when(j == 0)
    def _init():
        for h in range(N_HEADS):
            st_scr[h] = s0_ref[0, h].T
        upad_scr[0:CACHE_PAD, :] = c0_ref[0]
        shift_scr[:, 0:SUBLANES, :] = jnp.zeros((2, SUBLANES, HEAD_DIM), F32)

    route = _route_block(carry_scr[0], carry_scr[1], wrh_ref, wrl_ref, br_ref, before_ref, lower_ref,
                         xs_ref, slot_ref, gate_ref, cnt_ref, tb=tb)

    def advance_route(n=1):
        for _ in range(n):
            next(route, None)

    advance_route()
    x = x_ref[0]
    h_in = _rms(x, nm_ref[...]).astype(BF16)
    proj = _dot(h_in, win_ref[...])
    advance_route(2)

    q_all = proj[:, 0:HGRN_WIDTH]
    fz = proj[:, HGRN_WIDTH:2 * HGRN_WIDTH]
    v_all = proj[:, 2 * HGRN_WIDTH:3 * HGRN_WIDTH]
    g_all = proj[:, 3 * HGRN_WIDTH:4 * HGRN_WIDTH]
    lb = lb_ref[...]
    f_all = lb + (1.0 - lb) * jax.nn.sigmoid(fz)
    lf_all = jnp.log(f_all)
    kk_all = 1.0 - f_all

    tri = tri_ref[...]
    p1, p2, p3 = _split3(lf_all)
    b_all = _dot(tri, p1) + _dot(tri, p2) + _dot(tri, p3)
    advance_route(2)

    ones_sq = ones_ref[...]
    hn = hn_ref[...]
    for h in range(N_HEADS):
        ls = slice(h * HEAD_DIM, (h + 1) * HEAD_DIM)
        st = st_scr[h]
        for ci in range(tb // chunk):
            rs = slice(ci * chunk, (ci + 1) * chunk)
            o, st = _hgrn_chunk(q_all[rs, ls], kk_all[rs, ls], f_all[rs, ls], b_all[rs, ls],
                                v_all[rs, ls], st, shift_scr, ones_sq, pair_ref)
            o = _rms(o, hn[:, ls])
            mix_scr[rs, ls] = (o * jax.nn.silu(g_all[rs, ls])).astype(BF16)
            advance_route()
        st_scr[h] = st

    u = proj[:, 4 * HGRN_WIDTH:4 * HGRN_WIDTH + CONV_CH] * jax.nn.sigmoid(
        proj[:, 4 * HGRN_WIDTH + CONV_CH:IN_COLS])
    upad_scr[CACHE_PAD:CACHE_PAD + tb, :] = u
    first = CACHE_PAD - CACHE_ROWS
    span = tb + CACHE_PAD - SUBLANES
    for r in range(1, SUBLANES):
        ush_scr[r, 0:span, :] = upad_scr[r:r + span, :]
    acc = jnp.zeros((tb, CONV_CH), F32) + cb_ref[...]
    for tap in range(CONV_WIDTH):
        a, r = divmod(first + tap, SUBLANES)
        rows = slice(a * SUBLANES, a * SUBLANES + tb)
        shifted = upad_scr[rows, :] if r == 0 else ush_scr[r, rows, :]
        acc = acc + cw_ref[tap:tap + 1, :] * shifted
    mu = jnp.mean(acc, axis=-1, keepdims=True)
    cen = acc - mu
    var = jnp.mean(cen * cen, axis=-1, keepdims=True)
    cn = cen * lax.rsqrt(var + EPS) * lng_ref[...] + lnb_ref[...]
    mix_scr[:, HGRN_WIDTH:HGRN_WIDTH + CONV_CH] = jax.nn.silu(cn).astype(BF16)
    new_cache = upad_scr[tb:tb + CACHE_PAD, :]
    upad_scr[0:CACHE_PAD, :] = new_cache
    for _ in route:
        pass

    x1 = x + _dot(mix_scr[...], wout_ref[...])
    x1_ref[...] = x1
    xn = _rms(x1, nf_ref[...])
    xh = xn.astype(BF16)
    carry_scr[0] = xh
    carry_scr[1] = (xn - xh.astype(F32)).astype(BF16)

    @pl.when((s < nblk) & (j == nj - 1))
    def _fin():
        for h in range(N_HEADS):
            sout_ref[0, h] = st_scr[h].T
        cout_ref[0] = upad_scr[tb + first:tb + CACHE_PAD, :]


def _route_block(xh, xl, wrh_ref, wrl_ref, br_ref, before_ref, lower_ref, xs_ref, slot_ref, gate_ref, cnt_ref, *, tb):
    wrh = wrh_ref[...]
    logits = _dot_nt(wrh, xh) + _dot_nt(wrh, xl) + _dot_nt(wrl_ref[...], xh) + br_ref[...]
    yield

    eidx = lax.broadcasted_iota(I32, (N_EXPERTS, tb), 0).astype(F32)
    vals, hots = [], []
    work = logits
    for _ in range(TOP_K):
        mx = jnp.max(work, axis=0, keepdims=True)
        pick = jnp.min(jnp.where(work == mx, eidx, float(N_EXPERTS)), axis=0, keepdims=True)
        hot = eidx == pick
        vals.append(mx)
        hots.append(hot)
        work = jnp.where(hot, -jnp.inf, work)
        yield
    exps = [jnp.exp(val - vals[0]) for val in vals]
    inv = 1.0 / (exps[0] + exps[1] + exps[2] + exps[3])
    gate_ref[0] = jnp.concatenate([e * inv for e in exps], axis=0)

    before = before_ref[...]
    lower = lower_ref[...]
    hot_f = [hot.astype(F32) for hot in hots]
    tots = [jnp.sum(hf, axis=1, keepdims=True) for hf in hot_f]
    cnt = tots[0] + tots[1] + tots[2] + tots[3]
    cnt_ref[0] = cnt.astype(I32)
    start = _dot(lower, jnp.broadcast_to(cnt, (N_EXPERTS, LANES)).astype(BF16))[:, 0:1]
    yield
    slots = []
    base = start
    for k in range(TOP_K):
        pre = _dot(hot_f[k].astype(BF16), before)
        slots.append(jnp.sum(hot_f[k] * (base + pre), axis=0, keepdims=True))
        base = base + tots[k]
        yield
    slot_ref[0] = jnp.concatenate(slots, axis=0).astype(I32)

    srow = lax.broadcasted_iota(I32, (TOP_K * tb, tb), 0).astype(F32)
    perm = (srow == slots[0])
    for k in range(1, TOP_K):
        perm = perm | (srow == slots[k])
    yield
    xs = _dot(perm.astype(BF16), xh)
    yield
    _to_row_tiles(xs_ref, xs)


def _mix_stage(x, s0, c0, w, *, tb, chunk):
    bsz, t, _ = x.shape
    nj = t // tb
    nblk = bsz * nj

    def full(a):
        return pl.BlockSpec(a.shape, lambda s: (0,) * a.ndim)

    def cur(s):
        sc = jnp.minimum(s, nblk - 1)
        return sc // nj, sc % nj

    def prev(s):
        return jnp.maximum(s - 1, 0)

    weights = [w["norm_mix"], w["w_in"], w["lb"], w["hgrn_norm"], w["conv_w"], w["conv_b"], w["ln_g"],
               w["ln_b"], w["w_out"], w["norm_ffn"], w["wr_hi"], w["wr_lo"], w["b_router"]]
    r_t = lax.broadcasted_iota(I32, (tb, tb), 0)
    c_t = lax.broadcasted_iota(I32, (tb, tb), 1)
    r_e = lax.broadcasted_iota(I32, (N_EXPERTS, N_EXPERTS), 0)
    c_e = lax.broadcasted_iota(I32, (N_EXPERTS, N_EXPERTS), 1)
    weights += [((r_t >= c_t) & ((r_t ^ c_t) < chunk)).astype(BF16), (r_t < c_t).astype(BF16),
                (c_e < r_e).astype(BF16), jnp.ones((HEAD_DIM, HEAD_DIM), BF16)]
    r_c = lax.broadcasted_iota(I32, (chunk, chunk), 0)
    c_c = lax.broadcasted_iota(I32, (chunk, chunk), 1)
    x_c = r_c ^ c_c
    pair = []
    m = chunk // 2
    while m >= SUBLANES:
        pair.append((x_c >= m) & (x_c < 2 * m) & (r_c > c_c))
        m //= 2
    pair += [(x_c < SUBLANES) & (r_c - c_c == d) for d in range(SUBLANES)]
    weights.append(jnp.stack(pair).astype(F32))
    in_specs = [
        pl.BlockSpec((1, tb, D_MODEL), lambda s: (*cur(s), 0)),
        pl.BlockSpec((1, N_HEADS, HEAD_DIM, HEAD_DIM), lambda s: (cur(s)[0], 0, 0, 0)),
        pl.BlockSpec((1, CACHE_PAD, CONV_CH), lambda s: (cur(s)[0], 0, 0)),
    ] + [full(a) for a in weights]
    out_shape = (
        jax.ShapeDtypeStruct(((nblk + 1) * tb, D_MODEL), F32),
        jax.ShapeDtypeStruct((TOP_K * bsz * t * SUBLANES, LANES), F32),
        jax.ShapeDtypeStruct((bsz, TOP_K, t), I32),
        jax.ShapeDtypeStruct((bsz, TOP_K, t), F32),
        jax.ShapeDtypeStruct((nblk, N_EXPERTS, 1), I32),
        jax.ShapeDtypeStruct((bsz, N_HEADS, HEAD_DIM, HEAD_DIM), F32),
        jax.ShapeDtypeStruct((bsz, CACHE_ROWS, CONV_CH), F32),
    )
    out_specs = (
        pl.BlockSpec((tb, D_MODEL), lambda s: (s, 0)),
        pl.BlockSpec((TOP_K * tb * SUBLANES, LANES), lambda s: (prev(s), 0)),
        pl.BlockSpec((1, TOP_K, tb), lambda s: (prev(s) // nj, 0, prev(s) % nj)),
        pl.BlockSpec((1, TOP_K, tb), lambda s: (prev(s) // nj, 0, prev(s) % nj)),
        pl.BlockSpec((1, N_EXPERTS, 1), lambda s: (prev(s), 0, 0)),
        pl.BlockSpec((1, N_HEADS, HEAD_DIM, HEAD_DIM), lambda s: (cur(s)[0], 0, 0, 0)),
        pl.BlockSpec((1, CACHE_ROWS, CONV_CH), lambda s: (cur(s)[0], 0, 0)),
    )
    return pl.pallas_call(
        functools.partial(_mix_body, tb=tb, chunk=chunk, nj=nj, nblk=nblk),
        grid=(nblk + 1,),
        in_specs=in_specs,
        out_specs=out_specs,
        out_shape=out_shape,
        scratch_shapes=[
            pltpu.VMEM((N_HEADS, HEAD_DIM, HEAD_DIM), F32),
            pltpu.VMEM((CACHE_PAD + tb, CONV_CH), F32),
            pltpu.VMEM((SUBLANES, CACHE_PAD + tb - SUBLANES, CONV_CH), F32),
            pltpu.VMEM((2, SUBLANES + chunk, HEAD_DIM), F32),
            pltpu.VMEM((tb, D_MODEL), BF16),
            pltpu.VMEM((2, tb, D_MODEL), BF16),
        ],
        compiler_params=pltpu.CompilerParams(
            dimension_semantics=("arbitrary",), vmem_limit_bytes=VMEM_LIMIT_BYTES),
        name="mix",
    )(x, s0, c0, *weights)


def _moe_body(te_ref, tlo_ref, tv_ref, glo_ref, ghi_ref, wslot_ref, enext_ref, pre_ref, cnt_ref, src_ref,
              xs_p_hbm, xs_s_hbm, zeros_hbm, wgu_hbm, bgu_ref, wd_hbm, bd_ref, y_ref,
              xbuf, wgu_f, wd_f, wgu_b, wd_b, gsem, wsem, *, tm, nt, nb, nb_p):
    i = pl.program_id(0)
    slot = lax.rem(i, 2)

    def issue_tile(t, s):
        e = te_ref[t]
        lo = tlo_ref[t]
        nv = tv_ref[t]
        hi = lo + nv
        dst0 = s * tm

        def runs(xs_hbm):
            def body(g, carry):
                p0 = pre_ref[e * nb + g]
                a = jnp.maximum(p0, lo)
                b = jnp.minimum(p0 + cnt_ref[e * nb + g], hi)
                _copy_rows(xs_hbm, src_ref[e * nb + g] + (a - p0), xbuf, dst0 + (a - lo), b - a, gsem.at[s])
                return carry
            return body
        g_lo, g_hi = glo_ref[t], ghi_ref[t]
        lax.fori_loop(g_lo, jnp.minimum(g_hi, nb_p), runs(xs_p_hbm), 0)
        lax.fori_loop(jnp.maximum(g_lo, nb_p), g_hi, runs(xs_s_hbm), 0)
        _copy_rows(zeros_hbm, 0, xbuf, dst0 + nv, tm - nv, gsem.at[s])

    def weight_fetch(e, ws):
        return (pltpu.make_async_copy(wgu_hbm.at[e], wgu_f.at[ws], wsem.at[ws]),
                pltpu.make_async_copy(wd_hbm.at[e], wd_f.at[ws], wsem.at[ws]))

    @pl.when(i == 0)
    def _prologue():
        issue_tile(0, 0)
        for c in weight_fetch(te_ref[0], wslot_ref[0]):
            c.start()

    @pl.when(i + 1 < nt)
    def _prefetch():
        issue_tile(i + 1, 1 - slot)

    used = tv_ref[i] > 0
    new_expert = (i == 0) | (te_ref[i] != te_ref[jnp.maximum(i - 1, 0)])

    @pl.when(used & new_expert)
    def _switch_expert():
        ws = wslot_ref[i]
        for c in weight_fetch(0, ws):
            c.wait()
        nxt = enext_ref[i]

        @pl.when(nxt >= 0)
        def _():
            for c in weight_fetch(nxt, 1 - ws):
                c.start()
        rows = 128

        def cast_gu(c, carry):
            r0 = pl.multiple_of(c * rows, rows)
            wgu_b[pl.ds(r0, rows), :] = wgu_f[ws, pl.ds(r0, rows), :].astype(BF16)
            return carry
        lax.fori_loop(0, D_MODEL // rows, cast_gu, 0)

        def cast_d(c, carry):
            r0 = pl.multiple_of(c * rows, rows)
            wd_b[pl.ds(r0, rows), :] = wd_f[ws, pl.ds(r0, rows), :].astype(BF16)
            return carry
        lax.fori_loop(0, D_FF // rows, cast_d, 0)

    tile = pl.ds(pl.multiple_of(slot * tm * SUBLANES, tm * SUBLANES), tm * SUBLANES)
    pltpu.make_async_copy(zeros_hbm, xbuf.at[tile], gsem.at[slot]).wait()

    @pl.when(used)
    def _compute():
        xb = _from_row_tiles(xbuf, tm, base=slot * tm)
        gu = _dot(xb, wgu_b[...]) + bgu_ref[...]
        gate = jnp.minimum(gu[:, :D_FF], SWIGLU_LIMIT)
        up = jnp.clip(gu[:, D_FF:], -SWIGLU_LIMIT, SWIGLU_LIMIT)
        hid = (up + 1.0) * (gate * jax.nn.sigmoid(SWIGLU_ALPHA * gate))
        _to_row_tiles(y_ref, _dot(hid.astype(BF16), wd_b[...]) + bd_ref[...])

    @pl.when(jnp.logical_not(used))
    def _unused_tile():
        y_ref[...] = jnp.zeros((tm * SUBLANES, LANES), F32)


def _moe_stage(xs_p, xs_s, zeros, tabs, w_gu, b_gu, w_d, b_d, *, tm, nt, nb, nb_p):
    n_pref = len(tabs)

    def wmap(i, te, *_):
        return (te[i], 0, 0)

    grid_spec = pltpu.PrefetchScalarGridSpec(
        num_scalar_prefetch=n_pref,
        grid=(nt,),
        in_specs=[
            pl.BlockSpec(memory_space=pl.ANY),
            pl.BlockSpec(memory_space=pl.ANY),
            pl.BlockSpec(memory_space=pl.ANY),
            pl.BlockSpec(memory_space=pl.ANY),
            pl.BlockSpec((None, 1, 2 * D_FF), wmap),
            pl.BlockSpec(memory_space=pl.ANY),
            pl.BlockSpec((None, 1, D_MODEL), wmap),
        ],
        out_specs=pl.BlockSpec((tm * SUBLANES, LANES), lambda i, *_: (i, 0)),
        scratch_shapes=[
            pltpu.VMEM((2 * tm * SUBLANES, LANES), F32),
            pltpu.VMEM((2, D_MODEL, 2 * D_FF), F32),
            pltpu.VMEM((2, D_FF, D_MODEL), F32),
            pltpu.VMEM((D_MODEL, 2 * D_FF), BF16),
            pltpu.VMEM((D_FF, D_MODEL), BF16),
            pltpu.SemaphoreType.DMA((2,)),
            pltpu.SemaphoreType.DMA((2,)),
        ],
    )
    return pl.pallas_call(
        functools.partial(_moe_body, tm=tm, nt=nt, nb=nb, nb_p=nb_p),
        grid_spec=grid_spec,
        out_shape=jax.ShapeDtypeStruct((nt * tm * SUBLANES, LANES), F32),
        compiler_params=pltpu.CompilerParams(
            dimension_semantics=("arbitrary",), vmem_limit_bytes=VMEM_LIMIT_BYTES),
        name="moe",
    )(*tabs, xs_p, xs_s, zeros, w_gu, b_gu, w_d, b_d)


def _combine_body(pre_ref, cnt_ref, lst_ref, ys0_ref, y_hbm, x1_ref, slot_ref, g_ref, pe_ref, np_ref, wpg_ref,
                  wple_ref, nfin_ref, out_ref, stag, sem, *, tc, nsteps, nb, g0, blk_per_step, rows_per_blk):
    i = pl.program_id(0)
    slot = lax.rem(i, 2)
    rows = TOP_K * tc

    def issue_step(step, s):
        for bi in range(blk_per_step):
            g = g0 + step * blk_per_step + bi
            dst0 = s * rows + bi * rows_per_blk

            def body(e, carry):
                src = ys0_ref[e] + pre_ref[e * nb + g]
                _copy_rows(y_hbm, src, stag, dst0 + lst_ref[g * N_EXPERTS + e], cnt_ref[e * nb + g], sem.at[s])
                return carry
            lax.fori_loop(0, N_EXPERTS, body, 0)

    @pl.when(i == 0)
    def _prologue():
        issue_step(0, 0)

    @pl.when(i + 1 < nsteps)
    def _prefetch():
        issue_step(i + 1, 1 - slot)

    tile = pl.ds(pl.multiple_of(slot * rows * SUBLANES, rows * SUBLANES), rows * SUBLANES)
    pltpu.make_async_copy(y_hbm.at[pl.ds(0, rows * SUBLANES)], stag.at[tile], sem.at[slot]).wait()

    scol = lax.broadcasted_iota(I32, (tc, rows), 1)
    sl = slot_ref[...]
    g = g_ref[...]
    back = jnp.zeros((tc, rows), F32)
    for k in range(TOP_K):
        back = jnp.where(scol == sl[:, k:k + 1], g[:, k:k + 1], back)
    x2 = x1_ref[...] + _dot(back.astype(BF16), _from_row_tiles(stag, rows, base=slot * rows))
    hp = _rms(x2, np_ref[...]).astype(BF16)
    gate = jax.nn.sigmoid(_dot(hp, wpg_ref[...]))
    emb = _dot(pe_ref[...].astype(BF16), wple_ref[...])
    x3 = x2 + gate * emb
    out_ref[...] = _rms(x3, nfin_ref[...])


def _combine_stage(x1, ys, slots_t, gates_t, pe, tabs, w, *, tc, nb, g0, blk_per_step, rows_per_blk):
    n = pe.shape[0]
    nsteps = n // tc

    def full(a):
        return pl.BlockSpec(a.shape, lambda i, *_: (0,) * a.ndim)

    weights = [w["norm_ple"], w["w_ple_gate"], w["w_ple"], w["norm_final"]]
    grid_spec = pltpu.PrefetchScalarGridSpec(
        num_scalar_prefetch=len(tabs),
        grid=(nsteps,),
        in_specs=[
            pl.BlockSpec(memory_space=pl.ANY),
            pl.BlockSpec((tc, D_MODEL), lambda i, *_: (i, 0)),
            pl.BlockSpec((tc, TOP_K), lambda i, *_: (i, 0)),
            pl.BlockSpec((tc, TOP_K), lambda i, *_: (i, 0)),
            pl.BlockSpec((tc, PLE_DIM), lambda i, *_: (i, 0)),
        ] + [full(a) for a in weights],
        out_specs=pl.BlockSpec((tc, D_MODEL), lambda i, *_: (i, 0)),
        scratch_shapes=[
            pltpu.VMEM((2 * TOP_K * tc * SUBLANES, LANES), F32),
            pltpu.SemaphoreType.DMA((2,)),
        ],
    )
    return pl.pallas_call(
        functools.partial(_combine_body, tc=tc, nsteps=nsteps, nb=nb, g0=g0, blk_per_step=blk_per_step,
                          rows_per_blk=rows_per_blk),
        grid_spec=grid_spec,
        out_shape=jax.ShapeDtypeStruct((n, D_MODEL), F32),
        compiler_params=pltpu.CompilerParams(
            dimension_semantics=("arbitrary",), vmem_limit_bytes=VMEM_LIMIT_BYTES),
        name="combine",
    )(*tabs, ys, x1, slots_t, gates_t, pe, *weights)


def _dispatch_tables(cnt, blk_row0, tm, nt):
    nb = cnt.shape[0]
    experts = jnp.arange(N_EXPERTS, dtype=I32)
    pre = jnp.cumsum(cnt, axis=0) - cnt
    total = jnp.sum(cnt, axis=0)
    lst = jnp.cumsum(cnt, axis=1) - cnt
    src = blk_row0[:, None] + lst
    n_tiles_e = (total + tm - 1) // tm
    tile_end = jnp.cumsum(n_tiles_e)
    tile_start = tile_end - n_tiles_e
    tiles = jnp.arange(nt, dtype=I32)
    tile_e_raw = jnp.sum((tiles[:, None] >= tile_end[None, :]).astype(I32), axis=1)
    used = tile_e_raw < N_EXPERTS
    has = total > 0
    last_e = jnp.max(jnp.where(has, experts, 0))
    tile_e = jnp.where(used, jnp.minimum(tile_e_raw, N_EXPERTS - 1), last_e).astype(I32)
    hot = tile_e[:, None] == experts[None, :]

    def per_tile(v):
        if v.ndim == 1:
            return jnp.sum(jnp.where(hot, v[None, :], 0), axis=1)
        return jnp.sum(jnp.where(hot[:, :, None], v[None, :, :], 0), axis=1)

    tile_lo = jnp.where(used, (tiles - per_tile(tile_start)) * tm, 0).astype(I32)
    tile_valid = jnp.where(used, jnp.clip(per_tile(total) - tile_lo, 0, tm), 0).astype(I32)
    p_t = per_tile(pre.T)
    c_t = per_tile(cnt.T)
    meets = (p_t < (tile_lo + tile_valid)[:, None]) & ((p_t + c_t) > tile_lo[:, None]) & (c_t > 0)
    gidx = jnp.arange(nb, dtype=I32)[None, :]
    g_lo = jnp.min(jnp.where(meets, gidx, nb), axis=1).astype(I32)
    g_hi = jnp.max(jnp.where(meets, gidx + 1, 0), axis=1).astype(I32)
    g_lo = jnp.minimum(g_lo, g_hi)
    w_slot_e = (jnp.cumsum(has.astype(I32)) - 1) % 2
    later = has[None, :] & (experts[None, :] > experts[:, None])
    e_next = jnp.min(jnp.where(later, experts[None, :], N_EXPERTS), axis=1)
    e_next = jnp.where(e_next >= N_EXPERTS, -1, e_next)
    flat = lambda a: a.T.reshape(-1).astype(I32)
    return dict(tile_e=tile_e, tile_lo=tile_lo, tile_valid=tile_valid, g_lo=g_lo, g_hi=g_hi,
                w_slot=per_tile(w_slot_e).astype(I32), e_next=per_tile(e_next).astype(I32),
                pre=flat(pre), cnt=flat(cnt), src=flat(src), lst=lst.reshape(-1).astype(I32),
                ys0=(tile_start * tm).astype(I32))


def kernel(x_prompt, x_sample, state_hgrn, cache_conv, p_prompt, p_sample, norm_mix, w_in, hgrn_lb_logits, hgrn_norm, conv_w, conv_b, conv_ln_g, conv_ln_b, w_out, norm_ffn, w_router, b_router, w_gate_up, b_gate_up, w_down, b_down, w_ple, norm_ple, w_ple_gate, norm_final):
    depth = norm_mix.shape[0]
    assert depth == 1
    bp, tp, _ = x_prompt.shape
    bs, ts, _ = x_sample.shape
    n_p, n_s = bp * tp, bs * ts
    n_tok = n_p + n_s

    lb_all = jnp.cumsum(jax.nn.softmax(hgrn_lb_logits.astype(F32), axis=0), axis=0)
    wr_t = w_router[0].T
    wr_hi = wr_t.astype(BF16)
    w = {
        "norm_mix": norm_mix[0][None, :], "w_in": w_in[0].astype(BF16), "lb": lb_all[0][None, :],
        "hgrn_norm": hgrn_norm[0][None, :], "conv_w": conv_w[0], "conv_b": conv_b[0][None, :],
        "ln_g": conv_ln_g[0][None, :], "ln_b": conv_ln_b[0][None, :], "w_out": w_out[0].astype(BF16),
        "norm_ffn": norm_ffn[0][None, :], "wr_hi": wr_hi, "wr_lo": (wr_t - wr_hi.astype(F32)).astype(BF16),
        "b_router": b_router[0][:, None],
        "norm_ple": norm_ple[0][None, :], "w_ple_gate": w_ple_gate[0].astype(BF16),
        "w_ple": w_ple[0].astype(BF16), "norm_final": norm_final[None, :],
    }

    pad = ((0, 0), (CACHE_PAD - CACHE_ROWS, 0), (0, 0))
    s0_p = jnp.zeros((bp, N_HEADS, HEAD_DIM, HEAD_DIM), F32)
    c0_p = jnp.zeros((bp, CACHE_PAD, CONV_CH), F32)
    c0_s = jnp.pad(cache_conv[0], pad)

    tb_p = 256
    x1_p, xs_p, slot_p, gate_p, cnt_p, st_p, cc_p = _mix_stage(x_prompt, s0_p, c0_p, w, tb=tb_p, chunk=128)
    x1_s, xs_s, slot_s, gate_s, cnt_s, st_s, cc_s = _mix_stage(x_sample, state_hgrn[0], c0_s, w, tb=ts, chunk=ts)

    nb_p, nb_s = n_p // tb_p, bs
    nb = nb_p + nb_s
    cnt = jnp.concatenate([cnt_p[:, :, 0], cnt_s[:, :, 0]], axis=0)
    blk_row0 = jnp.concatenate([jnp.arange(nb_p, dtype=I32) * (TOP_K * tb_p),
                                jnp.arange(nb_s, dtype=I32) * (TOP_K * ts)])
    tm = 256
    nt = -(-(TOP_K * n_tok) // tm) + N_EXPERTS
    t = _dispatch_tables(cnt, blk_row0, tm, nt)

    zeros = jnp.zeros((tm * SUBLANES, LANES), F32)
    moe_tabs = (t["tile_e"], t["tile_lo"], t["tile_valid"], t["g_lo"], t["g_hi"], t["w_slot"], t["e_next"],
                t["pre"], t["cnt"], t["src"])
    ys = _moe_stage(xs_p, xs_s, zeros, moe_tabs, w_gate_up[0], b_gate_up[0][:, None, :], w_down[0],
                    b_down[0][:, None, :], tm=tm, nt=nt, nb=nb, nb_p=nb_p)

    def token_cols(a):
        return a.transpose(0, 2, 1).reshape(-1, TOP_K)

    comb_tabs = (t["pre"], t["cnt"], t["lst"], t["ys0"])
    y_p = _combine_stage(x1_p, ys, token_cols(slot_p), token_cols(gate_p), p_prompt[0].reshape(n_p, PLE_DIM),
                         comb_tabs, w, tc=tb_p, nb=nb, g0=0, blk_per_step=1, rows_per_blk=TOP_K * tb_p)
    slot_s_step = slot_s + (jnp.arange(bs, dtype=I32) * (TOP_K * ts))[:, None, None]
    y_s = _combine_stage(x1_s, ys, token_cols(slot_s_step), token_cols(gate_s), p_sample[0].reshape(n_s, PLE_DIM),
                         comb_tabs, w, tc=n_s, nb=nb, g0=nb_p, blk_per_step=nb_s, rows_per_blk=TOP_K * ts)

    return (y_p.reshape(bp, tp, D_MODEL), y_s.reshape(bs, ts, D_MODEL),
            st_p[None], cc_p[None], st_s[None], cc_s[None])
```

```python
import functools

import jax
import jax.numpy as jnp
from jax import lax
from jax.experimental import pallas as pl
from jax.experimental.pallas import tpu as pltpu

D_MODEL = 1024
HGRN_WIDTH = 512
HEAD_DIM = 128
N_HEADS = HGRN_WIDTH // HEAD_DIM
CONV_CH = 512
CONV_WIDTH = 31
CACHE_ROWS = CONV_WIDTH - 1
CACHE_PAD = 32
IN_COLS = 4 * HGRN_WIDTH + 2 * CONV_CH
N_EXPERTS = 32
TOP_K = 4
D_FF = 1024
SWIGLU_LIMIT = 7.0
SWIGLU_ALPHA = 1.702
PLE_DIM = 256
EPS = 1e-6

SUBLANES = 8
LANES = 128
VMEM_LIMIT_BYTES = 56 * 1024 * 1024

F32 = jnp.float32
BF16 = jnp.bfloat16
I32 = jnp.int32


def _split3(a):
    p1 = a.astype(BF16)
    r1 = a - p1.astype(F32)
    p2 = r1.astype(BF16)
    r2 = r1 - p2.astype(F32)
    return p1, p2, r2.astype(BF16)


def _dot(a, b):
    return jnp.dot(a, b, preferred_element_type=F32)


def _dot_nt(a, b):
    return lax.dot_general(a, b, (((1,), (1,)), ((), ())), preferred_element_type=F32)


def _dot_tn(a, b):
    return lax.dot_general(a, b, (((0,), (0,)), ((), ())), preferred_element_type=F32)


def _rms(x, g):
    return x * lax.rsqrt(jnp.mean(x * x, axis=-1, keepdims=True) + EPS) * g


def _copy_rows(src_hbm, src_row, dst_vmem, dst_row, n, sem):
    @pl.when(n > 0)
    def _():
        size = pl.multiple_of(n * SUBLANES, SUBLANES)
        pltpu.make_async_copy(src_hbm.at[pl.ds(pl.multiple_of(src_row * SUBLANES, SUBLANES), size)],
                              dst_vmem.at[pl.ds(pl.multiple_of(dst_row * SUBLANES, SUBLANES), size)], sem).start()


def _to_row_tiles(ref, x, base=0):
    rows = x.shape[0]
    for s in range(D_MODEL // LANES):
        ref[pl.ds(base * SUBLANES + s, rows, stride=SUBLANES), :] = x[:, s * LANES:(s + 1) * LANES]


def _from_row_tiles(ref, rows, base=0, dtype=BF16):
    return jnp.concatenate(
        [ref[pl.ds(base * SUBLANES + s, rows, stride=SUBLANES), :].astype(dtype) for s in range(D_MODEL // LANES)],
        axis=1)


def _hgrn_chunk(q, kk, f, b, v, st, shift_ref, ones_sq, pair_ref):
    c = q.shape[0]
    sc = None
    n_mask = 0
    m = c // 2
    while m >= SUBLANES:
        pieces = []
        for blk in range(c // (2 * m)):
            r = blk * 2 * m + m - 1
            pieces.append(jnp.broadcast_to(b[r:r + 1, :], (2 * m, HEAD_DIM)))
        bmid = pieces[0] if len(pieces) == 1 else jnp.concatenate(pieces, axis=0)
        e_m = jnp.exp(-jnp.abs(b - bmid))
        qh = (q * e_m).astype(BF16)
        kh = (kk * e_m).astype(BF16)
        term = pair_ref[n_mask] * _dot_nt(qh, kh)
        sc = term if sc is None else sc + term
        n_mask += 1
        m //= 2

    shift_ref[0, SUBLANES:SUBLANES + c, :] = f
    shift_ref[1, SUBLANES:SUBLANES + c, :] = kk
    decay = None
    for d in range(SUBLANES):
        if d == 0:
            kd = kk
        else:
            f_sh = shift_ref[0, SUBLANES - (d - 1):SUBLANES - (d - 1) + c, :]
            decay = f_sh if decay is None else decay * f_sh
            kd = shift_ref[1, SUBLANES - d:SUBLANES - d + c, :] * decay
        s_d = _dot((q * kd).astype(BF16), ones_sq)
        term = pair_ref[n_mask + d] * s_d[:, :c]
        sc = term if sc is None else sc + term

    o = _dot(sc.astype(BF16), v.astype(BF16))
    o = o + _dot_nt((q * jnp.exp(b)).astype(BF16), st.astype(BF16))
    b_last = b[c - 1:c, :]
    kdec = (kk * jnp.exp(b_last - b)).astype(BF16)
    st_new = st * jnp.exp(b_last) + _dot_tn(v.astype(BF16), kdec)
    return o, st_new


def _mix_body(x_ref, s0_ref, c0_ref, nm_ref, win_ref, lb_ref, hn_ref, cw_ref, cb_ref, lng_ref,
              lnb_ref, wout_ref, nf_ref, wrh_ref, wrl_ref, br_ref, tri_ref, before_ref, lower_ref, ones_ref, pair_ref,
              x1_ref, xs_ref, slot_ref, gate_ref, cnt_ref, sout_ref, cout_ref,
              st_scr, upad_scr, ush_scr, shift_scr, mix_scr, carry_scr, *, tb, chunk, nj, nblk, nseq):
    pipelined = nblk > 1
    tseq = tb // nseq
    s = pl.program_id(0)
    j = lax.rem(jnp.minimum(s, nblk - 1), nj)

    if pipelined:
        @pl.when(s == 0)
        def _():
            carry_scr[...] = jnp.zeros((2, tb, D_MODEL), BF16)

    @pl.when(j == 0)
    def _init():
        for q in range(nseq):
            for h in range(N_HEADS):
                st_scr[q * N_HEADS + h] = s0_ref[q, h].T
            upad_scr[q, 0:CACHE_PAD, :] = c0_ref[q]
        shift_scr[:, 0:SUBLANES, :] = jnp.zeros((2, SUBLANES, HEAD_DIM), F32)

    def router(xh, xl):
        return _route_block(xh, xl, wrh_ref, wrl_ref, br_ref, before_ref, lower_ref,
                            xs_ref, slot_ref, gate_ref, cnt_ref, tb=tb)

    route = router(carry_scr[0], carry_scr[1]) if pipelined else iter(())

    def advance_route(n=1):
        for _ in range(n):
            next(route, None)

    advance_route()
    x = x_ref[...].reshape(tb, D_MODEL)
    h_in = _rms(x, nm_ref[...]).astype(BF16)
    proj = _dot(h_in, win_ref[...])
    advance_route(2)

    q_all = proj[:, 0:HGRN_WIDTH]
    fz = proj[:, HGRN_WIDTH:2 * HGRN_WIDTH]
    v_all = proj[:, 2 * HGRN_WIDTH:3 * HGRN_WIDTH]
    g_all = proj[:, 3 * HGRN_WIDTH:4 * HGRN_WIDTH]
    lb = lb_ref[...]
    f_all = lb + (1.0 - lb) * jax.nn.sigmoid(fz)
    lf_all = jnp.log(f_all)
    kk_all = 1.0 - f_all

    tri = tri_ref[...]
    p1, p2, p3 = _split3(lf_all)
    b_all = _dot(tri, p1) + _dot(tri, p2) + _dot(tri, p3)
    advance_route(2)

    ones_sq = ones_ref[...]
    hn = hn_ref[...]
    per_seq = tseq // chunk
    for h in range(N_HEADS):
        ls = slice(h * HEAD_DIM, (h + 1) * HEAD_DIM)
        for ci in range(tb // chunk):
            q, cq = divmod(ci, per_seq)
            if cq == 0:
                st = st_scr[q * N_HEADS + h]
            rs = slice(ci * chunk, (ci + 1) * chunk)
            o, st = _hgrn_chunk(q_all[rs, ls], kk_all[rs, ls], f_all[rs, ls], b_all[rs, ls],
                                v_all[rs, ls], st, shift_scr, ones_sq, pair_ref)
            o = _rms(o, hn[:, ls])
            mix_scr[rs, ls] = (o * jax.nn.silu(g_all[rs, ls])).astype(BF16)
            if cq == per_seq - 1:
                st_scr[q * N_HEADS + h] = st
            advance_route()

    u = proj[:, 4 * HGRN_WIDTH:4 * HGRN_WIDTH + CONV_CH] * jax.nn.sigmoid(
        proj[:, 4 * HGRN_WIDTH + CONV_CH:IN_COLS])
    first = CACHE_PAD - CACHE_ROWS
    span = tseq + CACHE_PAD - SUBLANES
    accs = []
    for q in range(nseq):
        upad = upad_scr.at[q]
        upad[CACHE_PAD:CACHE_PAD + tseq, :] = u[q * tseq:(q + 1) * tseq, :]
        for r in range(1, SUBLANES):
            ush_scr[r, 0:span, :] = upad[r:r + span, :]
        acc = jnp.zeros((tseq, CONV_CH), F32) + cb_ref[...]
        for tap in range(CONV_WIDTH):
            a, r = divmod(first + tap, SUBLANES)
            rows = slice(a * SUBLANES, a * SUBLANES + tseq)
            shifted = upad[rows, :] if r == 0 else ush_scr[r, rows, :]
            acc = acc + cw_ref[tap:tap + 1, :] * shifted
        accs.append(acc)
        new_cache = upad[tseq:tseq + CACHE_PAD, :]
        upad[0:CACHE_PAD, :] = new_cache
    acc = accs[0] if nseq == 1 else jnp.concatenate(accs, axis=0)
    mu = jnp.mean(acc, axis=-1, keepdims=True)
    cen = acc - mu
    var = jnp.mean(cen * cen, axis=-1, keepdims=True)
    cn = cen * lax.rsqrt(var + EPS) * lng_ref[...] + lnb_ref[...]
    mix_scr[:, HGRN_WIDTH:HGRN_WIDTH + CONV_CH] = jax.nn.silu(cn).astype(BF16)
    for _ in route:
        pass

    x1 = x + _dot(mix_scr[...], wout_ref[...])
    x1_ref[...] = x1
    xn = _rms(x1, nf_ref[...])
    xh = xn.astype(BF16)
    xl = (xn - xh.astype(F32)).astype(BF16)
    if pipelined:
        carry_scr[0] = xh
        carry_scr[1] = xl
    else:
        for _ in router(xh, xl):
            pass

    @pl.when((s < nblk) & (j == nj - 1))
    def _fin():
        for q in range(nseq):
            for h in range(N_HEADS):
                sout_ref[q, h] = st_scr[q * N_HEADS + h].T
            cout_ref[q] = upad_scr[q, tseq + first:tseq + CACHE_PAD, :]


def _route_block(xh, xl, wrh_ref, wrl_ref, br_ref, before_ref, lower_ref, xs_ref, slot_ref, gate_ref, cnt_ref, *, tb):
    wrh = wrh_ref[...]
    logits = _dot_nt(wrh, xh) + _dot_nt(wrh, xl) + _dot_nt(wrl_ref[...], xh) + br_ref[...]
    yield

    eidx = lax.broadcasted_iota(I32, (N_EXPERTS, tb), 0).astype(F32)
    vals, hots = [], []
    work = logits
    for _ in range(TOP_K):
        mx = jnp.max(work, axis=0, keepdims=True)
        pick = jnp.min(jnp.where(work == mx, eidx, float(N_EXPERTS)), axis=0, keepdims=True)
        hot = eidx == pick
        vals.append(mx)
        hots.append(hot)
        work = jnp.where(hot, -jnp.inf, work)
        yield
    exps = [jnp.exp(val - vals[0]) for val in vals]
    inv = 1.0 / (exps[0] + exps[1] + exps[2] + exps[3])
    gate_ref[0] = jnp.concatenate([e * inv for e in exps], axis=0)

    before = before_ref[...]
    lower = lower_ref[...]
    hot_f = [hot.astype(F32) for hot in hots]
    tots = [jnp.sum(hf, axis=1, keepdims=True) for hf in hot_f]
    cnt = tots[0] + tots[1] + tots[2] + tots[3]
    cnt_ref[0] = cnt.astype(I32)
    start = _dot(lower, jnp.broadcast_to(cnt, (N_EXPERTS, LANES)).astype(BF16))[:, 0:1]
    yield
    slots = []
    base = start
    for k in range(TOP_K):
        pre = _dot(hot_f[k].astype(BF16), before)
        slots.append(jnp.sum(hot_f[k] * (base + pre), axis=0, keepdims=True))
        base = base + tots[k]
        yield
    slot_ref[0] = jnp.concatenate(slots, axis=0).astype(I32)

    srow = lax.broadcasted_iota(I32, (TOP_K * tb, tb), 0).astype(F32)
    perm = (srow == slots[0])
    for k in range(1, TOP_K):
        perm = perm | (srow == slots[k])
    yield
    xs = _dot(perm.astype(BF16), xh)
    yield
    _to_row_tiles(xs_ref, xs)


def _mix_stage(x, s0, c0, w, *, nseq, tseq, chunk):
    bsz, t, _ = x.shape
    tb = nseq * tseq
    nj = t // tseq
    nblk = (bsz // nseq) * nj
    pipelined = nblk > 1
    n_steps = nblk + 1 if pipelined else 1

    def full(a):
        return pl.BlockSpec(a.shape, lambda s: (0,) * a.ndim)

    def cur(s):
        sc = jnp.minimum(s, nblk - 1)
        return sc // nj, sc % nj

    def prev(s):
        return jnp.maximum(s - 1, 0) if pipelined else s

    weights = [w["norm_mix"], w["w_in"], w["lb"], w["hgrn_norm"], w["conv_w"], w["conv_b"], w["ln_g"],
               w["ln_b"], w["w_out"], w["norm_ffn"], w["wr_hi"], w["wr_lo"], w["b_router"]]
    r_t = lax.broadcasted_iota(I32, (tb, tb), 0)
    c_t = lax.broadcasted_iota(I32, (tb, tb), 1)
    r_e = lax.broadcasted_iota(I32, (N_EXPERTS, N_EXPERTS), 0)
    c_e = lax.broadcasted_iota(I32, (N_EXPERTS, N_EXPERTS), 1)
    weights += [((r_t >= c_t) & ((r_t ^ c_t) < chunk)).astype(BF16), (r_t < c_t).astype(BF16),
                (c_e < r_e).astype(BF16), jnp.ones((HEAD_DIM, HEAD_DIM), BF16)]
    r_c = lax.broadcasted_iota(I32, (chunk, chunk), 0)
    c_c = lax.broadcasted_iota(I32, (chunk, chunk), 1)
    x_c = r_c ^ c_c
    pair = []
    m = chunk // 2
    while m >= SUBLANES:
        pair.append((x_c >= m) & (x_c < 2 * m) & (r_c > c_c))
        m //= 2
    pair += [(x_c < SUBLANES) & (r_c - c_c == d) for d in range(SUBLANES)]
    weights.append(jnp.stack(pair).astype(F32))
    in_specs = [
        pl.BlockSpec((nseq, tseq, D_MODEL), lambda s: (*cur(s), 0)),
        pl.BlockSpec((nseq, N_HEADS, HEAD_DIM, HEAD_DIM), lambda s: (cur(s)[0], 0, 0, 0)),
        pl.BlockSpec((nseq, CACHE_PAD, CONV_CH), lambda s: (cur(s)[0], 0, 0)),
    ] + [full(a) for a in weights]
    out_shape = (
        jax.ShapeDtypeStruct((n_steps * tb, D_MODEL), F32),
        jax.ShapeDtypeStruct((TOP_K * nblk * tb * SUBLANES, LANES), F32),
        jax.ShapeDtypeStruct((nblk, TOP_K, tb), I32),
        jax.ShapeDtypeStruct((nblk, TOP_K, tb), F32),
        jax.ShapeDtypeStruct((nblk, N_EXPERTS, 1), I32),
        jax.ShapeDtypeStruct((bsz, N_HEADS, HEAD_DIM, HEAD_DIM), F32),
        jax.ShapeDtypeStruct((bsz, CACHE_ROWS, CONV_CH), F32),
    )
    out_specs = (
        pl.BlockSpec((tb, D_MODEL), lambda s: (s, 0)),
        pl.BlockSpec((TOP_K * tb * SUBLANES, LANES), lambda s: (prev(s), 0)),
        pl.BlockSpec((1, TOP_K, tb), lambda s: (prev(s), 0, 0)),
        pl.BlockSpec((1, TOP_K, tb), lambda s: (prev(s), 0, 0)),
        pl.BlockSpec((1, N_EXPERTS, 1), lambda s: (prev(s), 0, 0)),
        pl.BlockSpec((nseq, N_HEADS, HEAD_DIM, HEAD_DIM), lambda s: (cur(s)[0], 0, 0, 0)),
        pl.BlockSpec((nseq, CACHE_ROWS, CONV_CH), lambda s: (cur(s)[0], 0, 0)),
    )
    return pl.pallas_call(
        functools.partial(_mix_body, tb=tb, chunk=chunk, nj=nj, nblk=nblk, nseq=nseq),
        grid=(n_steps,),
        in_specs=in_specs,
        out_specs=out_specs,
        out_shape=out_shape,
        scratch_shapes=[
            pltpu.VMEM((nseq * N_HEADS, HEAD_DIM, HEAD_DIM), F32),
            pltpu.VMEM((nseq, CACHE_PAD + tseq, CONV_CH), F32),
            pltpu.VMEM((SUBLANES, CACHE_PAD + tseq - SUBLANES, CONV_CH), F32),
            pltpu.VMEM((2, SUBLANES + chunk, HEAD_DIM), F32),
            pltpu.VMEM((tb, D_MODEL), BF16),
            pltpu.VMEM((2, tb, D_MODEL), BF16),
        ],
        compiler_params=pltpu.CompilerParams(
            dimension_semantics=("arbitrary",), vmem_limit_bytes=VMEM_LIMIT_BYTES),
        name="mix",
    )(x, s0, c0, *weights)


def _moe_body(te_ref, tlo_ref, tv_ref, glo_ref, ghi_ref, wslot_ref, enext_ref, pre_ref, cnt_ref, src_ref,
              xs_p_hbm, xs_s_hbm, zeros_hbm, wgu_hbm, bgu_ref, wd_hbm, bd_ref, y_ref,
              xbuf, wgu_f, wd_f, wgu_b, wd_b, gsem, wsem, *, tm, nt, nb, nb_p):
    i = pl.program_id(0)
    slot = lax.rem(i, 2)

    def issue_tile(t, s):
        e = te_ref[t]
        lo = tlo_ref[t]
        nv = tv_ref[t]
        hi = lo + nv
        dst0 = s * tm

        def runs(xs_hbm):
            def body(g, carry):
                p0 = pre_ref[e * nb + g]
                a = jnp.maximum(p0, lo)
                b = jnp.minimum(p0 + cnt_ref[e * nb + g], hi)
                _copy_rows(xs_hbm, src_ref[e * nb + g] + (a - p0), xbuf, dst0 + (a - lo), b - a, gsem.at[s])
                return carry
            return body
        g_lo, g_hi = glo_ref[t], ghi_ref[t]
        lax.fori_loop(g_lo, jnp.minimum(g_hi, nb_p), runs(xs_p_hbm), 0)
        lax.fori_loop(jnp.maximum(g_lo, nb_p), g_hi, runs(xs_s_hbm), 0)
        _copy_rows(zeros_hbm, 0, xbuf, dst0 + nv, tm - nv, gsem.at[s])

    def weight_fetch(e, ws):
        return (pltpu.make_async_copy(wgu_hbm.at[e], wgu_f.at[ws], wsem.at[ws]),
                pltpu.make_async_copy(wd_hbm.at[e], wd_f.at[ws], wsem.at[ws]))

    @pl.when(i == 0)
    def _prologue():
        issue_tile(0, 0)
        for c in weight_fetch(te_ref[0], wslot_ref[0]):
            c.start()

    @pl.when(i + 1 < nt)
    def _prefetch():
        issue_tile(i + 1, 1 - slot)

    used = tv_ref[i] > 0
    new_expert = (i == 0) | (te_ref[i] != te_ref[jnp.maximum(i - 1, 0)])

    @pl.when(used & new_expert)
    def _switch_expert():
        ws = wslot_ref[i]
        for c in weight_fetch(0, ws):
            c.wait()
        nxt = enext_ref[i]

        @pl.when(nxt >= 0)
        def _():
            for c in weight_fetch(nxt, 1 - ws):
                c.start(priority=1)
        rows = 128

        def cast_gu(c, carry):
            r0 = pl.multiple_of(c * rows, rows)
            wgu_b[pl.ds(r0, rows), :] = wgu_f[ws, pl.ds(r0, rows), :].astype(BF16)
            return carry
        lax.fori_loop(0, D_MODEL // rows, cast_gu, 0)

        def cast_d(c, carry):
            r0 = pl.multiple_of(c * rows, rows)
            wd_b[pl.ds(r0, rows), :] = wd_f[ws, pl.ds(r0, rows), :].astype(BF16)
            return carry
        lax.fori_loop(0, D_FF // rows, cast_d, 0)

    tile = pl.ds(pl.multiple_of(slot * tm * SUBLANES, tm * SUBLANES), tm * SUBLANES)
    pltpu.make_async_copy(zeros_hbm, xbuf.at[tile], gsem.at[slot]).wait()

    @pl.when(used)
    def _compute():
        xb = _from_row_tiles(xbuf, tm, base=slot * tm)
        gu = _dot(xb, wgu_b[...]) + bgu_ref[...]
        gate = jnp.minimum(gu[:, :D_FF], SWIGLU_LIMIT)
        up = jnp.clip(gu[:, D_FF:], -SWIGLU_LIMIT, SWIGLU_LIMIT)
        hid = (up + 1.0) * (gate * jax.nn.sigmoid(SWIGLU_ALPHA * gate))
        _to_row_tiles(y_ref, _dot(hid.astype(BF16), wd_b[...]) + bd_ref[...])

    @pl.when(jnp.logical_not(used))
    def _unused_tile():
        y_ref[...] = jnp.zeros((tm * SUBLANES, LANES), F32)


def _moe_stage(xs_p, xs_s, zeros, tabs, w_gu, b_gu, w_d, b_d, *, tm, nt, nb, nb_p):
    n_pref = len(tabs)

    def wmap(i, te, *_):
        return (te[i], 0, 0)

    grid_spec = pltpu.PrefetchScalarGridSpec(
        num_scalar_prefetch=n_pref,
        grid=(nt,),
        in_specs=[
            pl.BlockSpec(memory_space=pl.ANY),
            pl.BlockSpec(memory_space=pl.ANY),
            pl.BlockSpec(memory_space=pl.ANY),
            pl.BlockSpec(memory_space=pl.ANY),
            pl.BlockSpec((None, 1, 2 * D_FF), wmap),
            pl.BlockSpec(memory_space=pl.ANY),
            pl.BlockSpec((None, 1, D_MODEL), wmap),
        ],
        out_specs=pl.BlockSpec((tm * SUBLANES, LANES), lambda i, *_: (i, 0)),
        scratch_shapes=[
            pltpu.VMEM((2 * tm * SUBLANES, LANES), F32),
            pltpu.VMEM((2, D_MODEL, 2 * D_FF), F32),
            pltpu.VMEM((2, D_FF, D_MODEL), F32),
            pltpu.VMEM((D_MODEL, 2 * D_FF), BF16),
            pltpu.VMEM((D_FF, D_MODEL), BF16),
            pltpu.SemaphoreType.DMA((2,)),
            pltpu.SemaphoreType.DMA((2,)),
        ],
    )
    return pl.pallas_call(
        functools.partial(_moe_body, tm=tm, nt=nt, nb=nb, nb_p=nb_p),
        grid_spec=grid_spec,
        out_shape=jax.ShapeDtypeStruct((nt * tm * SUBLANES, LANES), F32),
        compiler_params=pltpu.CompilerParams(
            dimension_semantics=("arbitrary",), vmem_limit_bytes=VMEM_LIMIT_BYTES),
        name="moe",
    )(*tabs, xs_p, xs_s, zeros, w_gu, b_gu, w_d, b_d)


def _combine_body(pre_ref, cnt_ref, lst_ref, ys0_ref, y_hbm, x1_ref, slot_ref, g_ref, pe_ref, np_ref, wpg_ref,
                  wple_ref, nfin_ref, out_ref, stag, sem, *, tc, nsteps, nb, g0, blk_per_step, rows_per_blk):
    i = pl.program_id(0)
    slot = lax.rem(i, 2)
    rows = TOP_K * tc

    def issue_step(step, s):
        for bi in range(blk_per_step):
            g = g0 + step * blk_per_step + bi
            dst0 = s * rows + bi * rows_per_blk

            def body(e, carry):
                src = ys0_ref[e] + pre_ref[e * nb + g]
                _copy_rows(y_hbm, src, stag, dst0 + lst_ref[g * N_EXPERTS + e], cnt_ref[e * nb + g], sem.at[s])
                return carry
            lax.fori_loop(0, N_EXPERTS, body, 0)

    @pl.when(i == 0)
    def _prologue():
        issue_step(0, 0)

    @pl.when(i + 1 < nsteps)
    def _prefetch():
        issue_step(i + 1, 1 - slot)

    tile = pl.ds(pl.multiple_of(slot * rows * SUBLANES, rows * SUBLANES), rows * SUBLANES)
    pltpu.make_async_copy(y_hbm.at[pl.ds(0, rows * SUBLANES)], stag.at[tile], sem.at[slot]).wait()

    scol = lax.broadcasted_iota(I32, (tc, rows), 1)
    sl = slot_ref[...]
    g = g_ref[...]
    back = jnp.zeros((tc, rows), F32)
    for k in range(TOP_K):
        back = jnp.where(scol == sl[:, k:k + 1], g[:, k:k + 1], back)
    x2 = x1_ref[...] + _dot(back.astype(BF16), _from_row_tiles(stag, rows, base=slot * rows))
    hp = _rms(x2, np_ref[...]).astype(BF16)
    gate = jax.nn.sigmoid(_dot(hp, wpg_ref[...]))
    emb = _dot(pe_ref[...].astype(BF16), wple_ref[...])
    x3 = x2 + gate * emb
    out_ref[...] = _rms(x3, nfin_ref[...])


def _combine_stage(x1, ys, slots_t, gates_t, pe, tabs, w, *, tc, nb, g0, blk_per_step, rows_per_blk):
    n = pe.shape[0]
    nsteps = n // tc

    def full(a):
        return pl.BlockSpec(a.shape, lambda i, *_: (0,) * a.ndim)

    weights = [w["norm_ple"], w["w_ple_gate"], w["w_ple"], w["norm_final"]]
    grid_spec = pltpu.PrefetchScalarGridSpec(
        num_scalar_prefetch=len(tabs),
        grid=(nsteps,),
        in_specs=[
            pl.BlockSpec(memory_space=pl.ANY),
            pl.BlockSpec((tc, D_MODEL), lambda i, *_: (i, 0)),
            pl.BlockSpec((tc, TOP_K), lambda i, *_: (i, 0)),
            pl.BlockSpec((tc, TOP_K), lambda i, *_: (i, 0)),
            pl.BlockSpec((tc, PLE_DIM), lambda i, *_: (i, 0)),
        ] + [full(a) for a in weights],
        out_specs=pl.BlockSpec((tc, D_MODEL), lambda i, *_: (i, 0)),
        scratch_shapes=[
            pltpu.VMEM((2 * TOP_K * tc * SUBLANES, LANES), F32),
            pltpu.SemaphoreType.DMA((2,)),
        ],
    )
    return pl.pallas_call(
        functools.partial(_combine_body, tc=tc, nsteps=nsteps, nb=nb, g0=g0, blk_per_step=blk_per_step,
                          rows_per_blk=rows_per_blk),
        grid_spec=grid_spec,
        out_shape=jax.ShapeDtypeStruct((n, D_MODEL), F32),
        compiler_params=pltpu.CompilerParams(
            dimension_semantics=("arbitrary",), vmem_limit_bytes=VMEM_LIMIT_BYTES),
        name="combine",
    )(*tabs, ys, x1, slots_t, gates_t, pe, *weights)


def _dispatch_tables(cnt, blk_row0, tm, nt):
    nb = cnt.shape[0]
    experts = jnp.arange(N_EXPERTS, dtype=I32)
    pre = jnp.cumsum(cnt, axis=0) - cnt
    total = jnp.sum(cnt, axis=0)
    lst = jnp.cumsum(cnt, axis=1) - cnt
    src = blk_row0[:, None] + lst
    n_tiles_e = (total + tm - 1) // tm
    tile_end = jnp.cumsum(n_tiles_e)
    tile_start = tile_end - n_tiles_e
    tiles = jnp.arange(nt, dtype=I32)
    tile_e_raw = jnp.sum((tiles[:, None] >= tile_end[None, :]).astype(I32), axis=1)
    used = tile_e_raw < N_EXPERTS
    has = total > 0
    last_e = jnp.max(jnp.where(has, experts, 0))
    tile_e = jnp.where(used, jnp.minimum(tile_e_raw, N_EXPERTS - 1), last_e).astype(I32)
    hot = tile_e[:, None] == experts[None, :]

    def per_tile(v):
        if v.ndim == 1:
            return jnp.sum(jnp.where(hot, v[None, :], 0), axis=1)
        return jnp.sum(jnp.where(hot[:, :, None], v[None, :, :], 0), axis=1)

    tile_lo = jnp.where(used, (tiles - per_tile(tile_start)) * tm, 0).astype(I32)
    tile_valid = jnp.where(used, jnp.clip(per_tile(total) - tile_lo, 0, tm), 0).astype(I32)
    p_t = per_tile(pre.T)
    c_t = per_tile(cnt.T)
    meets = (p_t < (tile_lo + tile_valid)[:, None]) & ((p_t + c_t) > tile_lo[:, None]) & (c_t > 0)
    gidx = jnp.arange(nb, dtype=I32)[None, :]
    g_lo = jnp.min(jnp.where(meets, gidx, nb), axis=1).astype(I32)
    g_hi = jnp.max(jnp.where(meets, gidx + 1, 0), axis=1).astype(I32)
    g_lo = jnp.minimum(g_lo, g_hi)
    w_slot_e = (jnp.cumsum(has.astype(I32)) - 1) % 2
    later = has[None, :] & (experts[None, :] > experts[:, None])
    e_next = jnp.min(jnp.where(later, experts[None, :], N_EXPERTS), axis=1)
    e_next = jnp.where(e_next >= N_EXPERTS, -1, e_next)
    flat = lambda a: a.T.reshape(-1).astype(I32)
    return dict(tile_e=tile_e, tile_lo=tile_lo, tile_valid=tile_valid, g_lo=g_lo, g_hi=g_hi,
                w_slot=per_tile(w_slot_e).astype(I32), e_next=per_tile(e_next).astype(I32),
                pre=flat(pre), cnt=flat(cnt), src=flat(src), lst=lst.reshape(-1).astype(I32),
                ys0=(tile_start * tm).astype(I32))


def kernel(x_prompt, x_sample, state_hgrn, cache_conv, p_prompt, p_sample, norm_mix, w_in, hgrn_lb_logits, hgrn_norm, conv_w, conv_b, conv_ln_g, conv_ln_b, w_out, norm_ffn, w_router, b_router, w_gate_up, b_gate_up, w_down, b_down, w_ple, norm_ple, w_ple_gate, norm_final):
    depth = norm_mix.shape[0]
    assert depth == 1
    bp, tp, _ = x_prompt.shape
    bs, ts, _ = x_sample.shape
    n_p, n_s = bp * tp, bs * ts
    n_tok = n_p + n_s

    lb_all = jnp.cumsum(jax.nn.softmax(hgrn_lb_logits.astype(F32), axis=0), axis=0)
    wr_t = w_router[0].T
    wr_hi = wr_t.astype(BF16)
    w = {
        "norm_mix": norm_mix[0][None, :], "w_in": w_in[0].astype(BF16), "lb": lb_all[0][None, :],
        "hgrn_norm": hgrn_norm[0][None, :], "conv_w": conv_w[0], "conv_b": conv_b[0][None, :],
        "ln_g": conv_ln_g[0][None, :], "ln_b": conv_ln_b[0][None, :], "w_out": w_out[0].astype(BF16),
        "norm_ffn": norm_ffn[0][None, :], "wr_hi": wr_hi, "wr_lo": (wr_t - wr_hi.astype(F32)).astype(BF16),
        "b_router": b_router[0][:, None],
        "norm_ple": norm_ple[0][None, :], "w_ple_gate": w_ple_gate[0].astype(BF16),
        "w_ple": w_ple[0].astype(BF16), "norm_final": norm_final[None, :],
    }

    pad = ((0, 0), (CACHE_PAD - CACHE_ROWS, 0), (0, 0))
    s0_p = jnp.zeros((bp, N_HEADS, HEAD_DIM, HEAD_DIM), F32)
    c0_p = jnp.zeros((bp, CACHE_PAD, CONV_CH), F32)
    c0_s = jnp.pad(cache_conv[0], pad)

    tb_p = 256
    x1_p, xs_p, slot_p, gate_p, cnt_p, st_p, cc_p = _mix_stage(x_prompt, s0_p, c0_p, w, nseq=1, tseq=tb_p, chunk=128)
    x1_s, xs_s, slot_s, gate_s, cnt_s, st_s, cc_s = _mix_stage(x_sample, state_hgrn[0], c0_s, w, nseq=bs, tseq=ts,
                                                             chunk=ts)

    nb_p, nb_s = n_p // tb_p, 1
    nb = nb_p + nb_s
    cnt = jnp.concatenate([cnt_p[:, :, 0], cnt_s[:, :, 0]], axis=0)
    blk_row0 = jnp.concatenate([jnp.arange(nb_p, dtype=I32) * (TOP_K * tb_p), jnp.zeros((nb_s,), I32)])
    tm = 256
    nt = -(-(TOP_K * n_tok) // tm) + N_EXPERTS
    t = _dispatch_tables(cnt, blk_row0, tm, nt)

    zeros = jnp.zeros((tm * SUBLANES, LANES), F32)
    moe_tabs = (t["tile_e"], t["tile_lo"], t["tile_valid"], t["g_lo"], t["g_hi"], t["w_slot"], t["e_next"],
                t["pre"], t["cnt"], t["src"])
    ys = _moe_stage(xs_p, xs_s, zeros, moe_tabs, w_gate_up[0], b_gate_up[0][:, None, :], w_down[0],
                    b_down[0][:, None, :], tm=tm, nt=nt, nb=nb, nb_p=nb_p)

    def token_cols(a):
        return a.transpose(0, 2, 1).reshape(-1, TOP_K)

    comb_tabs = (t["pre"], t["cnt"], t["lst"], t["ys0"])
    y_p = _combine_stage(x1_p, ys, token_cols(slot_p), token_cols(gate_p), p_prompt[0].reshape(n_p, PLE_DIM),
                         comb_tabs, w, tc=tb_p, nb=nb, g0=0, blk_per_step=1, rows_per_blk=TOP_K * tb_p)
    y_s = _combine_stage(x1_s, ys, token_cols(slot_s), token_cols(gate_s), p_sample[0].reshape(n_s, PLE_DIM),
                         comb_tabs, w, tc=n_s, nb=nb, g0=nb_p, blk_per_step=1, rows_per_blk=TOP_K * n_s)

    return (y_p.reshape(bp, tp, D_MODEL), y_s.reshape(bs, ts, D_MODEL),
            st_p[None], cc_p[None], st_s[None], cc_s[None])
```

```python
import functools

import jax
import jax.numpy as jnp
from jax import lax
from jax.experimental import pallas as pl
from jax.experimental.pallas import tpu as pltpu

D_MODEL = 1024
HGRN_WIDTH = 512
HEAD_DIM = 128
N_HEADS = HGRN_WIDTH // HEAD_DIM
CONV_CH = 512
CONV_WIDTH = 31
CACHE_ROWS = CONV_WIDTH - 1
CACHE_PAD = 32
IN_COLS = 4 * HGRN_WIDTH + 2 * CONV_CH
N_EXPERTS = 32
TOP_K = 4
D_FF = 1024
SWIGLU_LIMIT = 7.0
SWIGLU_ALPHA = 1.702
PLE_DIM = 256
EPS = 1e-6

SUBLANES = 8
LANES = 128
VMEM_LIMIT_BYTES = 56 * 1024 * 1024

F32 = jnp.float32
BF16 = jnp.bfloat16
I32 = jnp.int32


def _split3(a):
    p1 = a.astype(BF16)
    r1 = a - p1.astype(F32)
    p2 = r1.astype(BF16)
    r2 = r1 - p2.astype(F32)
    return p1, p2, r2.astype(BF16)


def _dot(a, b):
    return jnp.dot(a, b, preferred_element_type=F32)


def _dot_nt(a, b):
    return lax.dot_general(a, b, (((1,), (1,)), ((), ())), preferred_element_type=F32)


def _dot_tn(a, b):
    return lax.dot_general(a, b, (((0,), (0,)), ((), ())), preferred_element_type=F32)


def _rms(x, g):
    return x * lax.rsqrt(jnp.mean(x * x, axis=-1, keepdims=True) + EPS) * g


def _copy_rows(src_hbm, src_row, dst_vmem, dst_row, n, sem):
    @pl.when(n > 0)
    def _():
        size = pl.multiple_of(n * SUBLANES, SUBLANES)
        pltpu.make_async_copy(src_hbm.at[pl.ds(pl.multiple_of(src_row * SUBLANES, SUBLANES), size)],
                              dst_vmem.at[pl.ds(pl.multiple_of(dst_row * SUBLANES, SUBLANES), size)], sem).start()


def _to_row_tiles(ref, x, base=0):
    rows = x.shape[0]
    for s in range(D_MODEL // LANES):
        ref[pl.ds(base * SUBLANES + s, rows, stride=SUBLANES), :] = x[:, s * LANES:(s + 1) * LANES]


def _from_row_tiles(ref, rows, base=0, dtype=BF16):
    return jnp.concatenate(
        [ref[pl.ds(base * SUBLANES + s, rows, stride=SUBLANES), :].astype(dtype) for s in range(D_MODEL // LANES)],
        axis=1)


def _hgrn_chunk(q, kk, f, b, v, st, shift_ref, ones_sq, pair_ref):
    c = q.shape[0]
    sc = None
    n_mask = 0
    m = c // 2
    while m >= SUBLANES:
        pieces = []
        for blk in range(c // (2 * m)):
            r = blk * 2 * m + m - 1
            pieces.append(jnp.broadcast_to(b[r:r + 1, :], (2 * m, HEAD_DIM)))
        bmid = pieces[0] if len(pieces) == 1 else jnp.concatenate(pieces, axis=0)
        e_m = jnp.exp(-jnp.abs(b - bmid))
        qh = (q * e_m).astype(BF16)
        kh = (kk * e_m).astype(BF16)
        term = pair_ref[n_mask] * _dot_nt(qh, kh)
        sc = term if sc is None else sc + term
        n_mask += 1
        m //= 2

    shift_ref[0, SUBLANES:SUBLANES + c, :] = f
    shift_ref[1, SUBLANES:SUBLANES + c, :] = kk
    decay = None
    for d in range(SUBLANES):
        if d == 0:
            kd = kk
        else:
            f_sh = shift_ref[0, SUBLANES - (d - 1):SUBLANES - (d - 1) + c, :]
            decay = f_sh if decay is None else decay * f_sh
            kd = shift_ref[1, SUBLANES - d:SUBLANES - d + c, :] * decay
        s_d = _dot((q * kd).astype(BF16), ones_sq)
        term = pair_ref[n_mask + d] * s_d[:, :c]
        sc = term if sc is None else sc + term

    o = _dot(sc.astype(BF16), v.astype(BF16))
    o = o + _dot_nt((q * jnp.exp(b)).astype(BF16), st.astype(BF16))
    b_last = b[c - 1:c, :]
    kdec = (kk * jnp.exp(b_last - b)).astype(BF16)
    st_new = st * jnp.exp(b_last) + _dot_tn(v.astype(BF16), kdec)
    return o, st_new


def _mix_body(x_ref, s0_ref, c0_ref, nm_ref, win_ref, lb_ref, hn_ref, cw_ref, cb_ref, lng_ref,
              lnb_ref, wout_ref, nf_ref, wrh_ref, wrl_ref, br_ref, tri_ref, before_ref, lower_ref, ones_ref, pair_ref,
              x1_ref, xs_ref, slot_ref, gate_ref, cnt_ref, sout_ref, cout_ref,
              st_scr, upad_scr, ush_scr, shift_scr, mix_scr, carry_scr, *, tb, chunk, nj, nblk, nseq):
    pipelined = nblk > 1
    tseq = tb // nseq
    s = pl.program_id(0)
    j = lax.rem(jnp.minimum(s, nblk - 1), nj)

    if pipelined:
        @pl.when(s == 0)
        def _():
            carry_scr[...] = jnp.zeros((2, tb, D_MODEL), BF16)

    @pl.when(j == 0)
    def _init():
        for q in range(nseq):
            for h in range(N_HEADS):
                st_scr[q * N_HEADS + h] = s0_ref[q, h].T
            upad_scr[q, 0:CACHE_PAD, :] = c0_ref[q]
        shift_scr[:, 0:SUBLANES, :] = jnp.zeros((2, SUBLANES, HEAD_DIM), F32)

    def router(xh, xl):
        return _route_block(xh, xl, wrh_ref, wrl_ref, br_ref, before_ref, lower_ref,
                            xs_ref, slot_ref, gate_ref, cnt_ref, tb=tb)

    route = router(carry_scr[0], carry_scr[1]) if pipelined else iter(())

    def advance_route(n=1):
        for _ in range(n):
            next(route, None)

    advance_route()
    x = x_ref[...].reshape(tb, D_MODEL)
    h_in = _rms(x, nm_ref[...]).astype(BF16)
    proj = _dot(h_in, win_ref[...])
    advance_route(2)

    q_all = proj[:, 0:HGRN_WIDTH]
    fz = proj[:, HGRN_WIDTH:2 * HGRN_WIDTH]
    v_all = proj[:, 2 * HGRN_WIDTH:3 * HGRN_WIDTH]
    g_all = proj[:, 3 * HGRN_WIDTH:4 * HGRN_WIDTH]
    lb = lb_ref[...]
    f_all = lb + (1.0 - lb) * jax.nn.sigmoid(fz)
    lf_all = jnp.log(f_all)
    kk_all = 1.0 - f_all

    tri = tri_ref[...]
    p1, p2, p3 = _split3(lf_all)
    b_all = _dot(tri, p1) + _dot(tri, p2) + _dot(tri, p3)
    advance_route(2)

    ones_sq = ones_ref[...]
    hn = hn_ref[...]
    per_seq = tseq // chunk
    for h in range(N_HEADS):
        ls = slice(h * HEAD_DIM, (h + 1) * HEAD_DIM)
        for ci in range(tb // chunk):
            q, cq = divmod(ci, per_seq)
            if cq == 0:
                st = st_scr[q * N_HEADS + h]
            rs = slice(ci * chunk, (ci + 1) * chunk)
            o, st = _hgrn_chunk(q_all[rs, ls], kk_all[rs, ls], f_all[rs, ls], b_all[rs, ls],
                                v_all[rs, ls], st, shift_scr, ones_sq, pair_ref)
            o = _rms(o, hn[:, ls])
            mix_scr[rs, ls] = (o * jax.nn.silu(g_all[rs, ls])).astype(BF16)
            if cq == per_seq - 1:
                st_scr[q * N_HEADS + h] = st
            advance_route()

    u = proj[:, 4 * HGRN_WIDTH:4 * HGRN_WIDTH + CONV_CH] * jax.nn.sigmoid(
        proj[:, 4 * HGRN_WIDTH + CONV_CH:IN_COLS])
    first = CACHE_PAD - CACHE_ROWS
    span = tseq + CACHE_PAD - SUBLANES
    accs = []
    for q in range(nseq):
        upad = upad_scr.at[q]
        upad[CACHE_PAD:CACHE_PAD + tseq, :] = u[q * tseq:(q + 1) * tseq, :]
        for r in range(1, SUBLANES):
            ush_scr[r, 0:span, :] = upad[r:r + span, :]
        acc = jnp.zeros((tseq, CONV_CH), F32) + cb_ref[...]
        for tap in range(CONV_WIDTH):
            a, r = divmod(first + tap, SUBLANES)
            rows = slice(a * SUBLANES, a * SUBLANES + tseq)
            shifted = upad[rows, :] if r == 0 else ush_scr[r, rows, :]
            acc = acc + cw_ref[tap:tap + 1, :] * shifted
        accs.append(acc)
        new_cache = upad[tseq:tseq + CACHE_PAD, :]
        upad[0:CACHE_PAD, :] = new_cache
    acc = accs[0] if nseq == 1 else jnp.concatenate(accs, axis=0)
    mu = jnp.mean(acc, axis=-1, keepdims=True)
    cen = acc - mu
    var = jnp.mean(cen * cen, axis=-1, keepdims=True)
    cn = cen * lax.rsqrt(var + EPS) * lng_ref[...] + lnb_ref[...]
    mix_scr[:, HGRN_WIDTH:HGRN_WIDTH + CONV_CH] = jax.nn.silu(cn).astype(BF16)
    for _ in route:
        pass

    x1 = x + _dot(mix_scr[...], wout_ref[...])
    x1_ref[...] = x1
    xn = _rms(x1, nf_ref[...])
    xh = xn.astype(BF16)
    xl = (xn - xh.astype(F32)).astype(BF16)
    if pipelined:
        carry_scr[0] = xh
        carry_scr[1] = xl
    else:
        for _ in router(xh, xl):
            pass

    @pl.when((s < nblk) & (j == nj - 1))
    def _fin():
        for q in range(nseq):
            for h in range(N_HEADS):
                sout_ref[q, h] = st_scr[q * N_HEADS + h].T
            cout_ref[q] = upad_scr[q, tseq + first:tseq + CACHE_PAD, :]


def _route_block(xh, xl, wrh_ref, wrl_ref, br_ref, before_ref, lower_ref, xs_ref, slot_ref, gate_ref, cnt_ref, *, tb):
    wrh = wrh_ref[...]
    logits = _dot_nt(wrh, xh) + _dot_nt(wrh, xl) + _dot_nt(wrl_ref[...], xh) + br_ref[...]
    yield

    eidx = lax.broadcasted_iota(I32, (N_EXPERTS, tb), 0).astype(F32)
    vals, hots = [], []
    work = logits
    for _ in range(TOP_K):
        mx = jnp.max(work, axis=0, keepdims=True)
        pick = jnp.min(jnp.where(work == mx, eidx, float(N_EXPERTS)), axis=0, keepdims=True)
        hot = eidx == pick
        vals.append(mx)
        hots.append(hot)
        work = jnp.where(hot, -jnp.inf, work)
        yield
    exps = [jnp.exp(val - vals[0]) for val in vals]
    inv = 1.0 / (exps[0] + exps[1] + exps[2] + exps[3])
    gate_ref[0] = jnp.concatenate([e * inv for e in exps], axis=0)

    before = before_ref[...]
    lower = lower_ref[...]
    hot_f = [hot.astype(F32) for hot in hots]
    tots = [jnp.sum(hf, axis=1, keepdims=True) for hf in hot_f]
    cnt = tots[0] + tots[1] + tots[2] + tots[3]
    cnt_ref[0] = cnt.astype(I32)
    start = _dot(lower, jnp.broadcast_to(cnt, (N_EXPERTS, LANES)).astype(BF16))[:, 0:1]
    yield
    slots = []
    base = start
    for k in range(TOP_K):
        pre = _dot(hot_f[k].astype(BF16), before)
        slots.append(jnp.sum(hot_f[k] * (base + pre), axis=0, keepdims=True))
        base = base + tots[k]
        yield
    slot_ref[0] = jnp.concatenate(slots, axis=0).astype(I32)

    srow = lax.broadcasted_iota(I32, (TOP_K * tb, tb), 0).astype(F32)
    perm = (srow == slots[0])
    for k in range(1, TOP_K):
        perm = perm | (srow == slots[k])
    yield
    xs = _dot(perm.astype(BF16), xh)
    yield
    _to_row_tiles(xs_ref, xs)


def _mix_stage(x, s0, c0, w, *, nseq, tseq, chunk):
    bsz, t, _ = x.shape
    tb = nseq * tseq
    nj = t // tseq
    nblk = (bsz // nseq) * nj
    pipelined = nblk > 1
    n_steps = nblk + 1 if pipelined else 1

    def full(a):
        return pl.BlockSpec(a.shape, lambda s: (0,) * a.ndim)

    def cur(s):
        sc = jnp.minimum(s, nblk - 1)
        return sc // nj, sc % nj

    def prev(s):
        return jnp.maximum(s - 1, 0) if pipelined else s

    weights = [w["norm_mix"], w["w_in"], w["lb"], w["hgrn_norm"], w["conv_w"], w["conv_b"], w["ln_g"],
               w["ln_b"], w["w_out"], w["norm_ffn"], w["wr_hi"], w["wr_lo"], w["b_router"]]
    r_t = lax.broadcasted_iota(I32, (tb, tb), 0)
    c_t = lax.broadcasted_iota(I32, (tb, tb), 1)
    r_e = lax.broadcasted_iota(I32, (N_EXPERTS, N_EXPERTS), 0)
    c_e = lax.broadcasted_iota(I32, (N_EXPERTS, N_EXPERTS), 1)
    weights += [((r_t >= c_t) & ((r_t ^ c_t) < chunk)).astype(BF16), (r_t < c_t).astype(BF16),
                (c_e < r_e).astype(BF16), jnp.ones((HEAD_DIM, HEAD_DIM), BF16)]
    r_c = lax.broadcasted_iota(I32, (chunk, chunk), 0)
    c_c = lax.broadcasted_iota(I32, (chunk, chunk), 1)
    x_c = r_c ^ c_c
    pair = []
    m = chunk // 2
    while m >= SUBLANES:
        pair.append((x_c >= m) & (x_c < 2 * m) & (r_c > c_c))
        m //= 2
    pair += [(x_c < SUBLANES) & (r_c - c_c == d) for d in range(SUBLANES)]
    weights.append(jnp.stack(pair).astype(F32))
    in_specs = [
        pl.BlockSpec((nseq, tseq, D_MODEL), lambda s: (*cur(s), 0)),
        pl.BlockSpec((nseq, N_HEADS, HEAD_DIM, HEAD_DIM), lambda s: (cur(s)[0], 0, 0, 0)),
        pl.BlockSpec((nseq, CACHE_PAD, CONV_CH), lambda s: (cur(s)[0], 0, 0)),
    ] + [full(a) for a in weights]
    out_shape = (
        jax.ShapeDtypeStruct((n_steps * tb, D_MODEL), F32),
        jax.ShapeDtypeStruct((TOP_K * nblk * tb * SUBLANES, LANES), F32),
        jax.ShapeDtypeStruct((nblk, TOP_K, tb), I32),
        jax.ShapeDtypeStruct((nblk, TOP_K, tb), F32),
        jax.ShapeDtypeStruct((nblk, N_EXPERTS, 1), I32),
        jax.ShapeDtypeStruct((bsz, N_HEADS, HEAD_DIM, HEAD_DIM), F32),
        jax.ShapeDtypeStruct((bsz, CACHE_ROWS, CONV_CH), F32),
    )
    out_specs = (
        pl.BlockSpec((tb, D_MODEL), lambda s: (s, 0)),
        pl.BlockSpec((TOP_K * tb * SUBLANES, LANES), lambda s: (prev(s), 0)),
        pl.BlockSpec((1, TOP_K, tb), lambda s: (prev(s), 0, 0)),
        pl.BlockSpec((1, TOP_K, tb), lambda s: (prev(s), 0, 0)),
        pl.BlockSpec((1, N_EXPERTS, 1), lambda s: (prev(s), 0, 0)),
        pl.BlockSpec((nseq, N_HEADS, HEAD_DIM, HEAD_DIM), lambda s: (cur(s)[0], 0, 0, 0)),
        pl.BlockSpec((nseq, CACHE_ROWS, CONV_CH), lambda s: (cur(s)[0], 0, 0)),
    )
    return pl.pallas_call(
        functools.partial(_mix_body, tb=tb, chunk=chunk, nj=nj, nblk=nblk, nseq=nseq),
        grid=(n_steps,),
        in_specs=in_specs,
        out_specs=out_specs,
        out_shape=out_shape,
        scratch_shapes=[
            pltpu.VMEM((nseq * N_HEADS, HEAD_DIM, HEAD_DIM), F32),
            pltpu.VMEM((nseq, CACHE_PAD + tseq, CONV_CH), F32),
            pltpu.VMEM((SUBLANES, CACHE_PAD + tseq - SUBLANES, CONV_CH), F32),
            pltpu.VMEM((2, SUBLANES + chunk, HEAD_DIM), F32),
            pltpu.VMEM((tb, D_MODEL), BF16),
            pltpu.VMEM((2, tb, D_MODEL), BF16),
        ],
        compiler_params=pltpu.CompilerParams(
            dimension_semantics=("arbitrary",), vmem_limit_bytes=VMEM_LIMIT_BYTES),
        name="mix",
    )(x, s0, c0, *weights)


def _moe_body(te_ref, tlo_ref, tv_ref, glo_ref, ghi_ref, wslot_ref, enext_ref, pre_ref, cnt_ref, src_ref,
              xs_p_hbm, xs_s_hbm, zeros_hbm, wgu_hbm, bgu_ref, wd_hbm, bd_ref, y_ref,
              xbuf, wgu_f, wd_f, wgu_b, wd_b, gsem, wsem, *, tm, nt, nb, nb_p):
    i = pl.program_id(0)
    slot = lax.rem(i, 2)

    def issue_tile(t, s):
        e = te_ref[t]
        lo = tlo_ref[t]
        nv = tv_ref[t]
        hi = lo + nv
        dst0 = s * tm

        def runs(xs_hbm):
            def body(g, carry):
                p0 = pre_ref[e * nb + g]
                a = jnp.maximum(p0, lo)
                b = jnp.minimum(p0 + cnt_ref[e * nb + g], hi)
                _copy_rows(xs_hbm, src_ref[e * nb + g] + (a - p0), xbuf, dst0 + (a - lo), b - a, gsem.at[s])
                return carry
            return body
        g_lo, g_hi = glo_ref[t], ghi_ref[t]
        lax.fori_loop(g_lo, jnp.minimum(g_hi, nb_p), runs(xs_p_hbm), 0)
        lax.fori_loop(jnp.maximum(g_lo, nb_p), g_hi, runs(xs_s_hbm), 0)
        _copy_rows(zeros_hbm, 0, xbuf, dst0 + nv, tm - nv, gsem.at[s])

    def weight_fetch(e, ws):
        return (pltpu.make_async_copy(wgu_hbm.at[e], wgu_f.at[ws], wsem.at[ws]),
                pltpu.make_async_copy(wd_hbm.at[e], wd_f.at[ws], wsem.at[ws]))

    @pl.when(i == 0)
    def _prologue():
        issue_tile(0, 0)
        for c in weight_fetch(te_ref[0], wslot_ref[0]):
            c.start()

    @pl.when(i + 1 < nt)
    def _prefetch():
        issue_tile(i + 1, 1 - slot)

    used = tv_ref[i] > 0
    new_expert = (i == 0) | (te_ref[i] != te_ref[jnp.maximum(i - 1, 0)])

    @pl.when(used & new_expert)
    def _switch_expert():
        ws = wslot_ref[i]
        for c in weight_fetch(0, ws):
            c.wait()
        nxt = enext_ref[i]

        @pl.when(nxt >= 0)
        def _():
            for c in weight_fetch(nxt, 1 - ws):
                c.start(priority=1)
        rows = 128

        def cast_gu(c, carry):
            r0 = pl.multiple_of(c * rows, rows)
            wgu_b[pl.ds(r0, rows), :] = wgu_f[ws, pl.ds(r0, rows), :].astype(BF16)
            return carry
        lax.fori_loop(0, D_MODEL // rows, cast_gu, 0)

        def cast_d(c, carry):
            r0 = pl.multiple_of(c * rows, rows)
            wd_b[pl.ds(r0, rows), :] = wd_f[ws, pl.ds(r0, rows), :].astype(BF16)
            return carry
        lax.fori_loop(0, D_FF // rows, cast_d, 0)

    tile = pl.ds(pl.multiple_of(slot * tm * SUBLANES, tm * SUBLANES), tm * SUBLANES)
    pltpu.make_async_copy(zeros_hbm, xbuf.at[tile], gsem.at[slot]).wait()

    @pl.when(used)
    def _compute():
        xb = _from_row_tiles(xbuf, tm, base=slot * tm)
        gu = _dot(xb, wgu_b[...]) + bgu_ref[...]
        gate = jnp.minimum(gu[:, :D_FF], SWIGLU_LIMIT)
        up = jnp.clip(gu[:, D_FF:], -SWIGLU_LIMIT, SWIGLU_LIMIT)
        hid = (up + 1.0) * (gate * jax.nn.sigmoid(SWIGLU_ALPHA * gate))
        _to_row_tiles(y_ref, _dot(hid.astype(BF16), wd_b[...]) + bd_ref[...])

    @pl.when(jnp.logical_not(used))
    def _unused_tile():
        y_ref[...] = jnp.zeros((tm * SUBLANES, LANES), F32)


def _moe_stage(xs_p, xs_s, zeros, tabs, w_gu, b_gu, w_d, b_d, *, tm, nt, nb, nb_p):
    n_pref = len(tabs)

    def wmap(i, te, *_):
        return (te[i], 0, 0)

    grid_spec = pltpu.PrefetchScalarGridSpec(
        num_scalar_prefetch=n_pref,
        grid=(nt,),
        in_specs=[
            pl.BlockSpec(memory_space=pl.ANY),
            pl.BlockSpec(memory_space=pl.ANY),
            pl.BlockSpec(memory_space=pl.ANY),
            pl.BlockSpec(memory_space=pl.ANY),
            pl.BlockSpec((None, 1, 2 * D_FF), wmap),
            pl.BlockSpec(memory_space=pl.ANY),
            pl.BlockSpec((None, 1, D_MODEL), wmap),
        ],
        out_specs=pl.BlockSpec((tm * SUBLANES, LANES), lambda i, *_: (i, 0)),
        scratch_shapes=[
            pltpu.VMEM((2 * tm * SUBLANES, LANES), F32),
            pltpu.VMEM((2, D_MODEL, 2 * D_FF), F32),
            pltpu.VMEM((2, D_FF, D_MODEL), F32),
            pltpu.VMEM((D_MODEL, 2 * D_FF), BF16),
            pltpu.VMEM((D_FF, D_MODEL), BF16),
            pltpu.SemaphoreType.DMA((2,)),
            pltpu.SemaphoreType.DMA((2,)),
        ],
    )
    return pl.pallas_call(
        functools.partial(_moe_body, tm=tm, nt=nt, nb=nb, nb_p=nb_p),
        grid_spec=grid_spec,
        out_shape=jax.ShapeDtypeStruct((nt * tm * SUBLANES, LANES), F32),
        compiler_params=pltpu.CompilerParams(
            dimension_semantics=("arbitrary",), vmem_limit_bytes=VMEM_LIMIT_BYTES),
        name="moe",
    )(*tabs, xs_p, xs_s, zeros, w_gu, b_gu, w_d, b_d)


def _combine_body(pre_ref, cnt_ref, lst_ref, ys0_ref, y_hbm, x1_ref, slot_ref, g_ref, pe_ref, np_ref, wpg_ref,
                  wple_ref, nfin_ref, out_ref, stag, sem, *, tc, nsteps, nb, g0, blk_per_step, rows_per_blk):
    i = pl.program_id(0)
    slot = lax.rem(i, 2)
    rows = TOP_K * tc

    def issue_step(step, s):
        for bi in range(blk_per_step):
            g = g0 + step * blk_per_step + bi
            dst0 = s * rows + bi * rows_per_blk

            def body(e, carry):
                src = ys0_ref[e] + pre_ref[e * nb + g]
                _copy_rows(y_hbm, src, stag, dst0 + lst_ref[g * N_EXPERTS + e], cnt_ref[e * nb + g], sem.at[s])
                return carry
            lax.fori_loop(0, N_EXPERTS, body, 0)

    @pl.when(i == 0)
    def _prologue():
        issue_step(0, 0)

    @pl.when(i + 1 < nsteps)
    def _prefetch():
        issue_step(i + 1, 1 - slot)

    tile = pl.ds(pl.multiple_of(slot * rows * SUBLANES, rows * SUBLANES), rows * SUBLANES)
    pltpu.make_async_copy(y_hbm.at[pl.ds(0, rows * SUBLANES)], stag.at[tile], sem.at[slot]).wait()

    srow = lax.broadcasted_iota(I32, (rows, tc), 0)
    sl = slot_ref[0]
    g = g_ref[0]
    back_t = jnp.zeros((rows, tc), F32)
    for k in range(TOP_K):
        back_t = jnp.where(srow == sl[k:k + 1, :], g[k:k + 1, :], back_t)
    x2 = x1_ref[...] + _dot_tn(back_t.astype(BF16), _from_row_tiles(stag, rows, base=slot * rows))
    hp = _rms(x2, np_ref[...]).astype(BF16)
    gate = jax.nn.sigmoid(_dot(hp, wpg_ref[...]))
    emb = _dot(pe_ref[...].astype(BF16), wple_ref[...])
    x3 = x2 + gate * emb
    out_ref[...] = _rms(x3, nfin_ref[...])


def _combine_stage(x1, ys, slots, gates, pe, tabs, w, *, tc, nb, g0, blk_per_step, rows_per_blk):
    n = pe.shape[0]
    nsteps = n // tc

    def full(a):
        return pl.BlockSpec(a.shape, lambda i, *_: (0,) * a.ndim)

    weights = [w["norm_ple"], w["w_ple_gate"], w["w_ple"], w["norm_final"]]
    grid_spec = pltpu.PrefetchScalarGridSpec(
        num_scalar_prefetch=len(tabs),
        grid=(nsteps,),
        in_specs=[
            pl.BlockSpec(memory_space=pl.ANY),
            pl.BlockSpec((tc, D_MODEL), lambda i, *_: (i, 0)),
            pl.BlockSpec((1, TOP_K, tc), lambda i, *_: (i, 0, 0)),
            pl.BlockSpec((1, TOP_K, tc), lambda i, *_: (i, 0, 0)),
            pl.BlockSpec((tc, PLE_DIM), lambda i, *_: (i, 0)),
        ] + [full(a) for a in weights],
        out_specs=pl.BlockSpec((tc, D_MODEL), lambda i, *_: (i, 0)),
        scratch_shapes=[
            pltpu.VMEM((2 * TOP_K * tc * SUBLANES, LANES), F32),
            pltpu.SemaphoreType.DMA((2,)),
        ],
    )
    return pl.pallas_call(
        functools.partial(_combine_body, tc=tc, nsteps=nsteps, nb=nb, g0=g0, blk_per_step=blk_per_step,
                          rows_per_blk=rows_per_blk),
        grid_spec=grid_spec,
        out_shape=jax.ShapeDtypeStruct((n, D_MODEL), F32),
        compiler_params=pltpu.CompilerParams(
            dimension_semantics=("arbitrary",), vmem_limit_bytes=VMEM_LIMIT_BYTES),
        name="combine",
    )(*tabs, ys, x1, slots, gates, pe, *weights)


def _dispatch_tables(cnt, blk_row0, tm, nt):
    nb = cnt.shape[0]
    experts = jnp.arange(N_EXPERTS, dtype=I32)
    pre = jnp.cumsum(cnt, axis=0) - cnt
    total = jnp.sum(cnt, axis=0)
    lst = jnp.cumsum(cnt, axis=1) - cnt
    src = blk_row0[:, None] + lst
    n_tiles_e = (total + tm - 1) // tm
    tile_end = jnp.cumsum(n_tiles_e)
    tile_start = tile_end - n_tiles_e
    tiles = jnp.arange(nt, dtype=I32)
    tile_e_raw = jnp.sum((tiles[:, None] >= tile_end[None, :]).astype(I32), axis=1)
    used = tile_e_raw < N_EXPERTS
    has = total > 0
    last_e = jnp.max(jnp.where(has, experts, 0))
    tile_e = jnp.where(used, jnp.minimum(tile_e_raw, N_EXPERTS - 1), last_e).astype(I32)
    hot = tile_e[:, None] == experts[None, :]

    def per_tile(v):
        if v.ndim == 1:
            return jnp.sum(jnp.where(hot, v[None, :], 0), axis=1)
        return jnp.sum(jnp.where(hot[:, :, None], v[None, :, :], 0), axis=1)

    tile_lo = jnp.where(used, (tiles - per_tile(tile_start)) * tm, 0).astype(I32)
    tile_valid = jnp.where(used, jnp.clip(per_tile(total) - tile_lo, 0, tm), 0).astype(I32)
    p_t = per_tile(pre.T)
    c_t = per_tile(cnt.T)
    meets = (p_t < (tile_lo + tile_valid)[:, None]) & ((p_t + c_t) > tile_lo[:, None]) & (c_t > 0)
    gidx = jnp.arange(nb, dtype=I32)[None, :]
    g_lo = jnp.min(jnp.where(meets, gidx, nb), axis=1).astype(I32)
    g_hi = jnp.max(jnp.where(meets, gidx + 1, 0), axis=1).astype(I32)
    g_lo = jnp.minimum(g_lo, g_hi)
    w_slot_e = (jnp.cumsum(has.astype(I32)) - 1) % 2
    later = has[None, :] & (experts[None, :] > experts[:, None])
    e_next = jnp.min(jnp.where(later, experts[None, :], N_EXPERTS), axis=1)
    e_next = jnp.where(e_next >= N_EXPERTS, -1, e_next)
    flat = lambda a: a.T.reshape(-1).astype(I32)
    return dict(tile_e=tile_e, tile_lo=tile_lo, tile_valid=tile_valid, g_lo=g_lo, g_hi=g_hi,
                w_slot=per_tile(w_slot_e).astype(I32), e_next=per_tile(e_next).astype(I32),
                pre=flat(pre), cnt=flat(cnt), src=flat(src), lst=lst.reshape(-1).astype(I32),
                ys0=(tile_start * tm).astype(I32))


def kernel(x_prompt, x_sample, state_hgrn, cache_conv, p_prompt, p_sample, norm_mix, w_in, hgrn_lb_logits, hgrn_norm, conv_w, conv_b, conv_ln_g, conv_ln_b, w_out, norm_ffn, w_router, b_router, w_gate_up, b_gate_up, w_down, b_down, w_ple, norm_ple, w_ple_gate, norm_final):
    depth = norm_mix.shape[0]
    assert depth == 1
    bp, tp, _ = x_prompt.shape
    bs, ts, _ = x_sample.shape
    n_p, n_s = bp * tp, bs * ts
    n_tok = n_p + n_s

    lb_all = jnp.cumsum(jax.nn.softmax(hgrn_lb_logits.astype(F32), axis=0), axis=0)
    wr_t = w_router[0].T
    wr_hi = wr_t.astype(BF16)
    w = {
        "norm_mix": norm_mix[0][None, :], "w_in": w_in[0].astype(BF16), "lb": lb_all[0][None, :],
        "hgrn_norm": hgrn_norm[0][None, :], "conv_w": conv_w[0], "conv_b": conv_b[0][None, :],
        "ln_g": conv_ln_g[0][None, :], "ln_b": conv_ln_b[0][None, :], "w_out": w_out[0].astype(BF16),
        "norm_ffn": norm_ffn[0][None, :], "wr_hi": wr_hi, "wr_lo": (wr_t - wr_hi.astype(F32)).astype(BF16),
        "b_router": b_router[0][:, None],
        "norm_ple": norm_ple[0][None, :], "w_ple_gate": w_ple_gate[0].astype(BF16),
        "w_ple": w_ple[0].astype(BF16), "norm_final": norm_final[None, :],
    }

    pad = ((0, 0), (CACHE_PAD - CACHE_ROWS, 0), (0, 0))
    s0_p = jnp.zeros((bp, N_HEADS, HEAD_DIM, HEAD_DIM), F32)
    c0_p = jnp.zeros((bp, CACHE_PAD, CONV_CH), F32)
    c0_s = jnp.pad(cache_conv[0], pad)

    tb_p = 256
    x1_p, xs_p, slot_p, gate_p, cnt_p, st_p, cc_p = _mix_stage(x_prompt, s0_p, c0_p, w, nseq=1, tseq=tb_p, chunk=128)
    x1_s, xs_s, slot_s, gate_s, cnt_s, st_s, cc_s = _mix_stage(x_sample, state_hgrn[0], c0_s, w, nseq=bs, tseq=ts,
                                                             chunk=ts)

    nb_p, nb_s = n_p // tb_p, 1
    nb = nb_p + nb_s
    cnt = jnp.concatenate([cnt_p[:, :, 0], cnt_s[:, :, 0]], axis=0)
    blk_row0 = jnp.concatenate([jnp.arange(nb_p, dtype=I32) * (TOP_K * tb_p), jnp.zeros((nb_s,), I32)])
    tm = 256
    nt = -(-(TOP_K * n_tok) // tm) + N_EXPERTS
    t = _dispatch_tables(cnt, blk_row0, tm, nt)

    zeros = jnp.zeros((tm * SUBLANES, LANES), F32)
    moe_tabs = (t["tile_e"], t["tile_lo"], t["tile_valid"], t["g_lo"], t["g_hi"], t["w_slot"], t["e_next"],
                t["pre"], t["cnt"], t["src"])
    ys = _moe_stage(xs_p, xs_s, zeros, moe_tabs, w_gate_up[0], b_gate_up[0][:, None, :], w_down[0],
                    b_down[0][:, None, :], tm=tm, nt=nt, nb=nb, nb_p=nb_p)

    comb_tabs = (t["pre"], t["cnt"], t["lst"], t["ys0"])
    y_p = _combine_stage(x1_p, ys, slot_p, gate_p, p_prompt[0].reshape(n_p, PLE_DIM),
                         comb_tabs, w, tc=tb_p, nb=nb, g0=0, blk_per_step=1, rows_per_blk=TOP_K * tb_p)
    y_s = _combine_stage(x1_s, ys, slot_s, gate_s, p_sample[0].reshape(n_s, PLE_DIM),
                         comb_tabs, w, tc=n_s, nb=nb, g0=nb_p, blk_per_step=1, rows_per_blk=TOP_K * n_s)

    return (y_p.reshape(bp, tp, D_MODEL), y_s.reshape(bs, ts, D_MODEL),
            st_p[None], cc_p[None], st_s[None], cc_s[None])
```

```python
import functools

import jax
import jax.numpy as jnp
from jax import lax
from jax.experimental import pallas as pl
from jax.experimental.pallas import tpu as pltpu

D_MODEL = 1024
HGRN_WIDTH = 512
HEAD_DIM = 128
N_HEADS = HGRN_WIDTH // HEAD_DIM
CONV_CH = 512
CONV_WIDTH = 31
CACHE_ROWS = CONV_WIDTH - 1
CACHE_PAD = 32
IN_COLS = 4 * HGRN_WIDTH + 2 * CONV_CH
N_EXPERTS = 32
TOP_K = 4
D_FF = 1024
SWIGLU_LIMIT = 7.0
SWIGLU_ALPHA = 1.702
PLE_DIM = 256
EPS = 1e-6

SUBLANES = 8
LANES = 128
VMEM_LIMIT_BYTES = 56 * 1024 * 1024

F32 = jnp.float32
BF16 = jnp.bfloat16
I32 = jnp.int32


def _split3(a):
    p1 = a.astype(BF16)
    r1 = a - p1.astype(F32)
    p2 = r1.astype(BF16)
    r2 = r1 - p2.astype(F32)
    return p1, p2, r2.astype(BF16)


def _dot(a, b):
    return jnp.dot(a, b, preferred_element_type=F32)


def _dot_nt(a, b):
    return lax.dot_general(a, b, (((1,), (1,)), ((), ())), preferred_element_type=F32)


def _dot_tn(a, b):
    return lax.dot_general(a, b, (((0,), (0,)), ((), ())), preferred_element_type=F32)


def _rms(x, g):
    return x * lax.rsqrt(jnp.mean(x * x, axis=-1, keepdims=True) + EPS) * g


def _copy_rows(src_hbm, src_row, dst_vmem, dst_row, n, sem):
    @pl.when(n > 0)
    def _():
        size = pl.multiple_of(n * SUBLANES, SUBLANES)
        pltpu.make_async_copy(src_hbm.at[pl.ds(pl.multiple_of(src_row * SUBLANES, SUBLANES), size)],
                              dst_vmem.at[pl.ds(pl.multiple_of(dst_row * SUBLANES, SUBLANES), size)], sem).start()


def _to_row_tiles(ref, x, base=0):
    rows = x.shape[0]
    for s in range(D_MODEL // LANES):
        ref[pl.ds(base * SUBLANES + s, rows, stride=SUBLANES), :] = x[:, s * LANES:(s + 1) * LANES]


def _from_row_tiles(ref, rows, base=0, dtype=BF16):
    return jnp.concatenate(
        [ref[pl.ds(base * SUBLANES + s, rows, stride=SUBLANES), :].astype(dtype) for s in range(D_MODEL // LANES)],
        axis=1)


def _hgrn_chunk(q, kk, f, b, v, st, shift_ref, ones_sq, pair_ref):
    c = q.shape[0]
    sc = None
    n_mask = 0
    m = c // 2
    while m >= SUBLANES:
        pieces = []
        for blk in range(c // (2 * m)):
            r = blk * 2 * m + m - 1
            pieces.append(jnp.broadcast_to(b[r:r + 1, :], (2 * m, HEAD_DIM)))
        bmid = pieces[0] if len(pieces) == 1 else jnp.concatenate(pieces, axis=0)
        e_m = jnp.exp(-jnp.abs(b - bmid))
        qh = (q * e_m).astype(BF16)
        kh = (kk * e_m).astype(BF16)
        term = pair_ref[n_mask] * _dot_nt(qh, kh)
        sc = term if sc is None else sc + term
        n_mask += 1
        m //= 2

    shift_ref[0, SUBLANES:SUBLANES + c, :] = f
    shift_ref[1, SUBLANES:SUBLANES + c, :] = kk
    decay = None
    for d in range(SUBLANES):
        if d == 0:
            kd = kk
        else:
            f_sh = shift_ref[0, SUBLANES - (d - 1):SUBLANES - (d - 1) + c, :]
            decay = f_sh if decay is None else decay * f_sh
            kd = shift_ref[1, SUBLANES - d:SUBLANES - d + c, :] * decay
        s_d = _dot((q * kd).astype(BF16), ones_sq)
        term = pair_ref[n_mask + d] * s_d[:, :c]
        sc = term if sc is None else sc + term

    o = _dot(sc.astype(BF16), v.astype(BF16))
    o = o + _dot_nt((q * jnp.exp(b)).astype(BF16), st.astype(BF16))
    b_last = b[c - 1:c, :]
    kdec = (kk * jnp.exp(b_last - b)).astype(BF16)
    st_new = st * jnp.exp(b_last) + _dot_tn(v.astype(BF16), kdec)
    return o, st_new


def _mix_body(x_ref, s0_ref, c0_ref, nm_ref, win_ref, lb_ref, hn_ref, cw_ref, cb_ref, lng_ref,
              lnb_ref, wout_ref, nf_ref, wrh_ref, wrl_ref, br_ref, tri_ref, before_ref, lower_ref, ones_ref, pair_ref,
              x1_ref, xs_ref, slot_ref, gate_ref, cnt_ref, sout_ref, cout_ref,
              st_scr, upad_scr, ush_scr, shift_scr, mix_scr, carry_scr, *, tb, chunk, nj, nblk, nseq):
    pipelined = nblk > 1
    tseq = tb // nseq
    s = pl.program_id(0)
    j = lax.rem(jnp.minimum(s, nblk - 1), nj)

    if pipelined:
        @pl.when(s == 0)
        def _():
            carry_scr[...] = jnp.zeros((2, tb, D_MODEL), BF16)

    @pl.when(j == 0)
    def _init():
        for q in range(nseq):
            for h in range(N_HEADS):
                st_scr[q * N_HEADS + h] = s0_ref[q, h].T
            upad_scr[q, 0:CACHE_PAD, :] = c0_ref[q]
        shift_scr[:, 0:SUBLANES, :] = jnp.zeros((2, SUBLANES, HEAD_DIM), F32)

    def router(xh, xl):
        return _route_block(xh, xl, wrh_ref, wrl_ref, br_ref, before_ref, lower_ref,
                            xs_ref, slot_ref, gate_ref, cnt_ref, tb=tb)

    route = router(carry_scr[0], carry_scr[1]) if pipelined else iter(())

    def advance_route(n=1):
        for _ in range(n):
            next(route, None)

    advance_route()
    x = x_ref[...].reshape(tb, D_MODEL)
    h_in = _rms(x, nm_ref[...]).astype(BF16)
    proj = _dot(h_in, win_ref[...])
    advance_route(2)

    q_all = proj[:, 0:HGRN_WIDTH]
    fz = proj[:, HGRN_WIDTH:2 * HGRN_WIDTH]
    v_all = proj[:, 2 * HGRN_WIDTH:3 * HGRN_WIDTH]
    g_all = proj[:, 3 * HGRN_WIDTH:4 * HGRN_WIDTH]
    lb = lb_ref[...]
    f_all = lb + (1.0 - lb) * jax.nn.sigmoid(fz)
    lf_all = jnp.log(f_all)
    kk_all = 1.0 - f_all

    tri = tri_ref[...]
    p1, p2, p3 = _split3(lf_all)
    b_all = _dot(tri, p1) + _dot(tri, p2) + _dot(tri, p3)
    advance_route(2)

    ones_sq = ones_ref[...]
    hn = hn_ref[...]
    per_seq = tseq // chunk
    for h in range(N_HEADS):
        ls = slice(h * HEAD_DIM, (h + 1) * HEAD_DIM)
        for ci in range(tb // chunk):
            q, cq = divmod(ci, per_seq)
            if cq == 0:
                st = st_scr[q * N_HEADS + h]
            rs = slice(ci * chunk, (ci + 1) * chunk)
            o, st = _hgrn_chunk(q_all[rs, ls], kk_all[rs, ls], f_all[rs, ls], b_all[rs, ls],
                                v_all[rs, ls], st, shift_scr, ones_sq, pair_ref)
            o = _rms(o, hn[:, ls])
            mix_scr[rs, ls] = (o * jax.nn.silu(g_all[rs, ls])).astype(BF16)
            if cq == per_seq - 1:
                st_scr[q * N_HEADS + h] = st
            advance_route()

    u = proj[:, 4 * HGRN_WIDTH:4 * HGRN_WIDTH + CONV_CH] * jax.nn.sigmoid(
        proj[:, 4 * HGRN_WIDTH + CONV_CH:IN_COLS])
    first = CACHE_PAD - CACHE_ROWS
    span = tseq + CACHE_PAD - SUBLANES
    accs = []
    for q in range(nseq):
        upad = upad_scr.at[q]
        upad[CACHE_PAD:CACHE_PAD + tseq, :] = u[q * tseq:(q + 1) * tseq, :]
        for r in range(1, SUBLANES):
            ush_scr[r, 0:span, :] = upad[r:r + span, :]
        acc = jnp.zeros((tseq, CONV_CH), F32) + cb_ref[...]
        for tap in range(CONV_WIDTH):
            a, r = divmod(first + tap, SUBLANES)
            rows = slice(a * SUBLANES, a * SUBLANES + tseq)
            shifted = upad[rows, :] if r == 0 else ush_scr[r, rows, :]
            acc = acc + cw_ref[tap:tap + 1, :] * shifted
        accs.append(acc)
        new_cache = upad[tseq:tseq + CACHE_PAD, :]
        upad[0:CACHE_PAD, :] = new_cache
    acc = accs[0] if nseq == 1 else jnp.concatenate(accs, axis=0)
    mu = jnp.mean(acc, axis=-1, keepdims=True)
    cen = acc - mu
    var = jnp.mean(cen * cen, axis=-1, keepdims=True)
    cn = cen * lax.rsqrt(var + EPS) * lng_ref[...] + lnb_ref[...]
    mix_scr[:, HGRN_WIDTH:HGRN_WIDTH + CONV_CH] = jax.nn.silu(cn).astype(BF16)
    for _ in route:
        pass

    x1 = x + _dot(mix_scr[...], wout_ref[...])
    x1_ref[...] = x1
    xn = _rms(x1, nf_ref[...])
    xh = xn.astype(BF16)
    xl = (xn - xh.astype(F32)).astype(BF16)
    if pipelined:
        carry_scr[0] = xh
        carry_scr[1] = xl
    else:
        for _ in router(xh, xl):
            pass

    @pl.when((s < nblk) & (j == nj - 1))
    def _fin():
        for q in range(nseq):
            for h in range(N_HEADS):
                sout_ref[q, h] = st_scr[q * N_HEADS + h].T
            cout_ref[q] = upad_scr[q, tseq + first:tseq + CACHE_PAD, :]


def _route_block(xh, xl, wrh_ref, wrl_ref, br_ref, before_ref, lower_ref, xs_ref, slot_ref, gate_ref, cnt_ref, *, tb):
    wrh = wrh_ref[...]
    logits = _dot_nt(wrh, xh) + _dot_nt(wrh, xl) + _dot_nt(wrl_ref[...], xh) + br_ref[...]
    yield

    eidx = lax.broadcasted_iota(I32, (N_EXPERTS, tb), 0).astype(F32)
    vals, hots = [], []
    work = logits
    for _ in range(TOP_K):
        mx = jnp.max(work, axis=0, keepdims=True)
        pick = jnp.min(jnp.where(work == mx, eidx, float(N_EXPERTS)), axis=0, keepdims=True)
        hot = eidx == pick
        vals.append(mx)
        hots.append(hot)
        work = jnp.where(hot, -jnp.inf, work)
        yield
    exps = [jnp.exp(val - vals[0]) for val in vals]
    inv = 1.0 / (exps[0] + exps[1] + exps[2] + exps[3])
    gate_ref[0] = jnp.concatenate([e * inv for e in exps], axis=0)

    before = before_ref[...]
    lower = lower_ref[...]
    hot_f = [hot.astype(F32) for hot in hots]
    tots = [jnp.sum(hf, axis=1, keepdims=True) for hf in hot_f]
    cnt = tots[0] + tots[1] + tots[2] + tots[3]
    cnt_ref[0] = cnt.astype(I32)
    start = _dot(lower, jnp.broadcast_to(cnt, (N_EXPERTS, LANES)).astype(BF16))[:, 0:1]
    yield
    slots = []
    base = start
    for k in range(TOP_K):
        pre = _dot(hot_f[k].astype(BF16), before)
        slots.append(jnp.sum(hot_f[k] * (base + pre), axis=0, keepdims=True))
        base = base + tots[k]
        yield
    slot_ref[0] = jnp.concatenate(slots, axis=0).astype(I32)

    srow = lax.broadcasted_iota(I32, (TOP_K * tb, tb), 0).astype(F32)
    perm = (srow == slots[0])
    for k in range(1, TOP_K):
        perm = perm | (srow == slots[k])
    yield
    xs = _dot(perm.astype(BF16), xh)
    yield
    _to_row_tiles(xs_ref, xs)


def _mix_stage(x, s0, c0, w, *, nseq, tseq, chunk):
    bsz, t, _ = x.shape
    tb = nseq * tseq
    nj = t // tseq
    nblk = (bsz // nseq) * nj
    pipelined = nblk > 1
    n_steps = nblk + 1 if pipelined else 1

    def full(a):
        return pl.BlockSpec(a.shape, lambda s: (0,) * a.ndim)

    def cur(s):
        sc = jnp.minimum(s, nblk - 1)
        return sc // nj, sc % nj

    def prev(s):
        return jnp.maximum(s - 1, 0) if pipelined else s

    weights = [w["norm_mix"], w["w_in"], w["lb"], w["hgrn_norm"], w["conv_w"], w["conv_b"], w["ln_g"],
               w["ln_b"], w["w_out"], w["norm_ffn"], w["wr_hi"], w["wr_lo"], w["b_router"]]
    r_t = lax.broadcasted_iota(I32, (tb, tb), 0)
    c_t = lax.broadcasted_iota(I32, (tb, tb), 1)
    r_e = lax.broadcasted_iota(I32, (N_EXPERTS, N_EXPERTS), 0)
    c_e = lax.broadcasted_iota(I32, (N_EXPERTS, N_EXPERTS), 1)
    weights += [((r_t >= c_t) & ((r_t ^ c_t) < chunk)).astype(BF16), (r_t < c_t).astype(BF16),
                (c_e < r_e).astype(BF16), jnp.ones((HEAD_DIM, HEAD_DIM), BF16)]
    r_c = lax.broadcasted_iota(I32, (chunk, chunk), 0)
    c_c = lax.broadcasted_iota(I32, (chunk, chunk), 1)
    x_c = r_c ^ c_c
    pair = []
    m = chunk // 2
    while m >= SUBLANES:
        pair.append((x_c >= m) & (x_c < 2 * m) & (r_c > c_c))
        m //= 2
    pair += [(x_c < SUBLANES) & (r_c - c_c == d) for d in range(SUBLANES)]
    weights.append(jnp.stack(pair).astype(F32))
    in_specs = [
        pl.BlockSpec((nseq, tseq, D_MODEL), lambda s: (*cur(s), 0)),
        pl.BlockSpec((nseq, N_HEADS, HEAD_DIM, HEAD_DIM), lambda s: (cur(s)[0], 0, 0, 0)),
        pl.BlockSpec((nseq, CACHE_PAD, CONV_CH), lambda s: (cur(s)[0], 0, 0)),
    ] + [full(a) for a in weights]
    out_shape = (
        jax.ShapeDtypeStruct((n_steps * tb, D_MODEL), F32),
        jax.ShapeDtypeStruct((TOP_K * nblk * tb * SUBLANES, LANES), F32),
        jax.ShapeDtypeStruct((nblk, TOP_K, tb), I32),
        jax.ShapeDtypeStruct((nblk, TOP_K, tb), F32),
        jax.ShapeDtypeStruct((nblk, N_EXPERTS, 1), I32),
        jax.ShapeDtypeStruct((bsz, N_HEADS, HEAD_DIM, HEAD_DIM), F32),
        jax.ShapeDtypeStruct((bsz, CACHE_ROWS, CONV_CH), F32),
    )
    out_specs = (
        pl.BlockSpec((tb, D_MODEL), lambda s: (s, 0)),
        pl.BlockSpec((TOP_K * tb * SUBLANES, LANES), lambda s: (prev(s), 0)),
        pl.BlockSpec((1, TOP_K, tb), lambda s: (prev(s), 0, 0)),
        pl.BlockSpec((1, TOP_K, tb), lambda s: (prev(s), 0, 0)),
        pl.BlockSpec((1, N_EXPERTS, 1), lambda s: (prev(s), 0, 0)),
        pl.BlockSpec((nseq, N_HEADS, HEAD_DIM, HEAD_DIM), lambda s: (cur(s)[0], 0, 0, 0)),
        pl.BlockSpec((nseq, CACHE_ROWS, CONV_CH), lambda s: (cur(s)[0], 0, 0)),
    )
    return pl.pallas_call(
        functools.partial(_mix_body, tb=tb, chunk=chunk, nj=nj, nblk=nblk, nseq=nseq),
        grid=(n_steps,),
        in_specs=in_specs,
        out_specs=out_specs,
        out_shape=out_shape,
        scratch_shapes=[
            pltpu.VMEM((nseq * N_HEADS, HEAD_DIM, HEAD_DIM), F32),
            pltpu.VMEM((nseq, CACHE_PAD + tseq, CONV_CH), F32),
            pltpu.VMEM((SUBLANES, CACHE_PAD + tseq - SUBLANES, CONV_CH), F32),
            pltpu.VMEM((2, SUBLANES + chunk, HEAD_DIM), F32),
            pltpu.VMEM((tb, D_MODEL), BF16),
            pltpu.VMEM((2, tb, D_MODEL), BF16),
        ],
        compiler_params=pltpu.CompilerParams(
            dimension_semantics=("arbitrary",), vmem_limit_bytes=VMEM_LIMIT_BYTES),
        name="mix",
    )(x, s0, c0, *weights)


def _moe_body(te_ref, tlo_ref, tv_ref, glo_ref, ghi_ref, wslot_ref, enext_ref, pre_ref, cnt_ref, src_ref,
              xs_p_hbm, xs_s_hbm, zeros_hbm, wgu_hbm, bgu_ref, wd_hbm, bd_ref, y_ref,
              xbuf, wgu_f, wd_f, wgu_b, wd_b, gsem, wsem, *, tm, nt, nb, nb_p):
    i = pl.program_id(0)
    slot = lax.rem(i, 2)

    def issue_tile(t, s):
        e = te_ref[t]
        lo = tlo_ref[t]
        nv = tv_ref[t]
        hi = lo + nv
        dst0 = s * tm

        def runs(xs_hbm):
            def body(g, carry):
                p0 = pre_ref[e * nb + g]
                a = jnp.maximum(p0, lo)
                b = jnp.minimum(p0 + cnt_ref[e * nb + g], hi)
                _copy_rows(xs_hbm, src_ref[e * nb + g] + (a - p0), xbuf, dst0 + (a - lo), b - a, gsem.at[s])
                return carry
            return body
        g_lo, g_hi = glo_ref[t], ghi_ref[t]
        lax.fori_loop(g_lo, jnp.minimum(g_hi, nb_p), runs(xs_p_hbm), 0)
        lax.fori_loop(jnp.maximum(g_lo, nb_p), g_hi, runs(xs_s_hbm), 0)
        _copy_rows(zeros_hbm, 0, xbuf, dst0 + nv, tm - nv, gsem.at[s])

    def weight_fetch(e, ws):
        return (pltpu.make_async_copy(wgu_hbm.at[e], wgu_f.at[ws], wsem.at[ws]),
                pltpu.make_async_copy(wd_hbm.at[e], wd_f.at[ws], wsem.at[ws]))

    @pl.when(i == 0)
    def _prologue():
        issue_tile(0, 0)
        for c in weight_fetch(te_ref[0], wslot_ref[0]):
            c.start()

    @pl.when(i + 1 < nt)
    def _prefetch():
        issue_tile(i + 1, 1 - slot)

    used = tv_ref[i] > 0
    new_expert = (i == 0) | (te_ref[i] != te_ref[jnp.maximum(i - 1, 0)])

    @pl.when(used & new_expert)
    def _switch_expert():
        ws = wslot_ref[i]
        for c in weight_fetch(0, ws):
            c.wait()
        nxt = enext_ref[i]

        @pl.when(nxt >= 0)
        def _():
            for c in weight_fetch(nxt, 1 - ws):
                c.start(priority=1)
        rows = 128

        def cast_gu(c, carry):
            r0 = pl.multiple_of(c * rows, rows)
            wgu_b[pl.ds(r0, rows), :] = wgu_f[ws, pl.ds(r0, rows), :].astype(BF16)
            return carry
        lax.fori_loop(0, D_MODEL // rows, cast_gu, 0)

        def cast_d(c, carry):
            r0 = pl.multiple_of(c * rows, rows)
            wd_b[pl.ds(r0, rows), :] = wd_f[ws, pl.ds(r0, rows), :].astype(BF16)
            return carry
        lax.fori_loop(0, D_FF // rows, cast_d, 0)

    tile = pl.ds(pl.multiple_of(slot * tm * SUBLANES, tm * SUBLANES), tm * SUBLANES)
    pltpu.make_async_copy(zeros_hbm, xbuf.at[tile], gsem.at[slot]).wait()

    def expert_ffn(rows):
        xb = _from_row_tiles(xbuf, rows, base=slot * tm)
        gu = _dot(xb, wgu_b[...]) + bgu_ref[...]
        gate = jnp.minimum(gu[:, :D_FF], SWIGLU_LIMIT)
        up = jnp.clip(gu[:, D_FF:], -SWIGLU_LIMIT, SWIGLU_LIMIT)
        hid = (up + 1.0) * (gate * jax.nn.sigmoid(SWIGLU_ALPHA * gate))
        _to_row_tiles(y_ref, _dot(hid.astype(BF16), wd_b[...]) + bd_ref[...])

    half = tm // 2
    nv_i = tv_ref[i]

    @pl.when(nv_i > half)
    def _full_tile():
        expert_ffn(tm)

    @pl.when(used & (nv_i <= half))
    def _half_tile():
        expert_ffn(half)
        y_ref[half * SUBLANES:tm * SUBLANES, :] = jnp.zeros((half * SUBLANES, LANES), F32)

    @pl.when(jnp.logical_not(used))
    def _unused_tile():
        y_ref[...] = jnp.zeros((tm * SUBLANES, LANES), F32)


def _moe_stage(xs_p, xs_s, zeros, tabs, w_gu, b_gu, w_d, b_d, *, tm, nt, nb, nb_p):
    n_pref = len(tabs)

    def wmap(i, te, *_):
        return (te[i], 0, 0)

    grid_spec = pltpu.PrefetchScalarGridSpec(
        num_scalar_prefetch=n_pref,
        grid=(nt,),
        in_specs=[
            pl.BlockSpec(memory_space=pl.ANY),
            pl.BlockSpec(memory_space=pl.ANY),
            pl.BlockSpec(memory_space=pl.ANY),
            pl.BlockSpec(memory_space=pl.ANY),
            pl.BlockSpec((None, 1, 2 * D_FF), wmap),
            pl.BlockSpec(memory_space=pl.ANY),
            pl.BlockSpec((None, 1, D_MODEL), wmap),
        ],
        out_specs=pl.BlockSpec((tm * SUBLANES, LANES), lambda i, *_: (i, 0)),
        scratch_shapes=[
            pltpu.VMEM((2 * tm * SUBLANES, LANES), F32),
            pltpu.VMEM((2, D_MODEL, 2 * D_FF), F32),
            pltpu.VMEM((2, D_FF, D_MODEL), F32),
            pltpu.VMEM((D_MODEL, 2 * D_FF), BF16),
            pltpu.VMEM((D_FF, D_MODEL), BF16),
            pltpu.SemaphoreType.DMA((2,)),
            pltpu.SemaphoreType.DMA((2,)),
        ],
    )
    return pl.pallas_call(
        functools.partial(_moe_body, tm=tm, nt=nt, nb=nb, nb_p=nb_p),
        grid_spec=grid_spec,
        out_shape=jax.ShapeDtypeStruct((nt * tm * SUBLANES, LANES), F32),
        compiler_params=pltpu.CompilerParams(
            dimension_semantics=("arbitrary",), vmem_limit_bytes=VMEM_LIMIT_BYTES),
        name="moe",
    )(*tabs, xs_p, xs_s, zeros, w_gu, b_gu, w_d, b_d)


def _combine_body(pre_ref, cnt_ref, lst_ref, ys0_ref, y_hbm, x1_ref, slot_ref, g_ref, pe_ref, np_ref, wpg_ref,
                  wple_ref, nfin_ref, out_ref, stag, sem, *, tc, nsteps, nb, g0, blk_per_step, rows_per_blk):
    i = pl.program_id(0)
    slot = lax.rem(i, 2)
    rows = TOP_K * tc

    def issue_step(step, s):
        for bi in range(blk_per_step):
            g = g0 + step * blk_per_step + bi
            dst0 = s * rows + bi * rows_per_blk

            def body(e, carry):
                src = ys0_ref[e] + pre_ref[e * nb + g]
                _copy_rows(y_hbm, src, stag, dst0 + lst_ref[g * N_EXPERTS + e], cnt_ref[e * nb + g], sem.at[s])
                return carry
            lax.fori_loop(0, N_EXPERTS, body, 0)

    @pl.when(i == 0)
    def _prologue():
        issue_step(0, 0)

    @pl.when(i + 1 < nsteps)
    def _prefetch():
        issue_step(i + 1, 1 - slot)

    tile = pl.ds(pl.multiple_of(slot * rows * SUBLANES, rows * SUBLANES), rows * SUBLANES)
    pltpu.make_async_copy(y_hbm.at[pl.ds(0, rows * SUBLANES)], stag.at[tile], sem.at[slot]).wait()

    srow = lax.broadcasted_iota(I32, (rows, tc), 0)
    sl = slot_ref[0]
    g = g_ref[0]
    back_t = jnp.zeros((rows, tc), F32)
    for k in range(TOP_K):
        back_t = jnp.where(srow == sl[k:k + 1, :], g[k:k + 1, :], back_t)
    x2 = x1_ref[...] + _dot_tn(back_t.astype(BF16), _from_row_tiles(stag, rows, base=slot * rows))
    hp = _rms(x2, np_ref[...]).astype(BF16)
    gate = jax.nn.sigmoid(_dot(hp, wpg_ref[...]))
    emb = _dot(pe_ref[...].astype(BF16), wple_ref[...])
    x3 = x2 + gate * emb
    out_ref[...] = _rms(x3, nfin_ref[...])


def _combine_stage(x1, ys, slots, gates, pe, tabs, w, *, tc, nb, g0, blk_per_step, rows_per_blk):
    n = pe.shape[0]
    nsteps = n // tc

    def full(a):
        return pl.BlockSpec(a.shape, lambda i, *_: (0,) * a.ndim)

    weights = [w["norm_ple"], w["w_ple_gate"], w["w_ple"], w["norm_final"]]
    grid_spec = pltpu.PrefetchScalarGridSpec(
        num_scalar_prefetch=len(tabs),
        grid=(nsteps,),
        in_specs=[
            pl.BlockSpec(memory_space=pl.ANY),
            pl.BlockSpec((tc, D_MODEL), lambda i, *_: (i, 0)),
            pl.BlockSpec((1, TOP_K, tc), lambda i, *_: (i, 0, 0)),
            pl.BlockSpec((1, TOP_K, tc), lambda i, *_: (i, 0, 0)),
            pl.BlockSpec((tc, PLE_DIM), lambda i, *_: (i, 0)),
        ] + [full(a) for a in weights],
        out_specs=pl.BlockSpec((tc, D_MODEL), lambda i, *_: (i, 0)),
        scratch_shapes=[
            pltpu.VMEM((2 * TOP_K * tc * SUBLANES, LANES), F32),
            pltpu.SemaphoreType.DMA((2,)),
        ],
    )
    return pl.pallas_call(
        functools.partial(_combine_body, tc=tc, nsteps=nsteps, nb=nb, g0=g0, blk_per_step=blk_per_step,
                          rows_per_blk=rows_per_blk),
        grid_spec=grid_spec,
        out_shape=jax.ShapeDtypeStruct((n, D_MODEL), F32),
        compiler_params=pltpu.CompilerParams(
            dimension_semantics=("arbitrary",), vmem_limit_bytes=VMEM_LIMIT_BYTES),
        name="combine",
    )(*tabs, ys, x1, slots, gates, pe, *weights)


def _dispatch_tables(cnt, blk_row0, tm, nt):
    nb = cnt.shape[0]
    experts = jnp.arange(N_EXPERTS, dtype=I32)
    pre = jnp.cumsum(cnt, axis=0) - cnt
    total = jnp.sum(cnt, axis=0)
    lst = jnp.cumsum(cnt, axis=1) - cnt
    src = blk_row0[:, None] + lst
    n_tiles_e = (total + tm - 1) // tm
    tile_end = jnp.cumsum(n_tiles_e)
    tile_start = tile_end - n_tiles_e
    tiles = jnp.arange(nt, dtype=I32)
    tile_e_raw = jnp.sum((tiles[:, None] >= tile_end[None, :]).astype(I32), axis=1)
    used = tile_e_raw < N_EXPERTS
    has = total > 0
    last_e = jnp.max(jnp.where(has, experts, 0))
    tile_e = jnp.where(used, jnp.minimum(tile_e_raw, N_EXPERTS - 1), last_e).astype(I32)
    hot = tile_e[:, None] == experts[None, :]

    def per_tile(v):
        if v.ndim == 1:
            return jnp.sum(jnp.where(hot, v[None, :], 0), axis=1)
        return jnp.sum(jnp.where(hot[:, :, None], v[None, :, :], 0), axis=1)

    tile_lo = jnp.where(used, (tiles - per_tile(tile_start)) * tm, 0).astype(I32)
    tile_valid = jnp.where(used, jnp.clip(per_tile(total) - tile_lo, 0, tm), 0).astype(I32)
    p_t = per_tile(pre.T)
    c_t = per_tile(cnt.T)
    meets = (p_t < (tile_lo + tile_valid)[:, None]) & ((p_t + c_t) > tile_lo[:, None]) & (c_t > 0)
    gidx = jnp.arange(nb, dtype=I32)[None, :]
    g_lo = jnp.min(jnp.where(meets, gidx, nb), axis=1).astype(I32)
    g_hi = jnp.max(jnp.where(meets, gidx + 1, 0), axis=1).astype(I32)
    g_lo = jnp.minimum(g_lo, g_hi)
    w_slot_e = (jnp.cumsum(has.astype(I32)) - 1) % 2
    later = has[None, :] & (experts[None, :] > experts[:, None])
    e_next = jnp.min(jnp.where(later, experts[None, :], N_EXPERTS), axis=1)
    e_next = jnp.where(e_next >= N_EXPERTS, -1, e_next)
    flat = lambda a: a.T.reshape(-1).astype(I32)
    return dict(tile_e=tile_e, tile_lo=tile_lo, tile_valid=tile_valid, g_lo=g_lo, g_hi=g_hi,
                w_slot=per_tile(w_slot_e).astype(I32), e_next=per_tile(e_next).astype(I32),
                pre=flat(pre), cnt=flat(cnt), src=flat(src), lst=lst.reshape(-1).astype(I32),
                ys0=(tile_start * tm).astype(I32))


def kernel(x_prompt, x_sample, state_hgrn, cache_conv, p_prompt, p_sample, norm_mix, w_in, hgrn_lb_logits, hgrn_norm, conv_w, conv_b, conv_ln_g, conv_ln_b, w_out, norm_ffn, w_router, b_router, w_gate_up, b_gate_up, w_down, b_down, w_ple, norm_ple, w_ple_gate, norm_final):
    depth = norm_mix.shape[0]
    assert depth == 1
    bp, tp, _ = x_prompt.shape
    bs, ts, _ = x_sample.shape
    n_p, n_s = bp * tp, bs * ts
    n_tok = n_p + n_s

    lb_all = jnp.cumsum(jax.nn.softmax(hgrn_lb_logits.astype(F32), axis=0), axis=0)
    wr_t = w_router[0].T
    wr_hi = wr_t.astype(BF16)
    w = {
        "norm_mix": norm_mix[0][None, :], "w_in": w_in[0].astype(BF16), "lb": lb_all[0][None, :],
        "hgrn_norm": hgrn_norm[0][None, :], "conv_w": conv_w[0], "conv_b": conv_b[0][None, :],
        "ln_g": conv_ln_g[0][None, :], "ln_b": conv_ln_b[0][None, :], "w_out": w_out[0].astype(BF16),
        "norm_ffn": norm_ffn[0][None, :], "wr_hi": wr_hi, "wr_lo": (wr_t - wr_hi.astype(F32)).astype(BF16),
        "b_router": b_router[0][:, None],
        "norm_ple": norm_ple[0][None, :], "w_ple_gate": w_ple_gate[0].astype(BF16),
        "w_ple": w_ple[0].astype(BF16), "norm_final": norm_final[None, :],
    }

    pad = ((0, 0), (CACHE_PAD - CACHE_ROWS, 0), (0, 0))
    s0_p = jnp.zeros((bp, N_HEADS, HEAD_DIM, HEAD_DIM), F32)
    c0_p = jnp.zeros((bp, CACHE_PAD, CONV_CH), F32)
    c0_s = jnp.pad(cache_conv[0], pad)

    tb_p = 256
    x1_p, xs_p, slot_p, gate_p, cnt_p, st_p, cc_p = _mix_stage(x_prompt, s0_p, c0_p, w, nseq=1, tseq=tb_p, chunk=128)
    x1_s, xs_s, slot_s, gate_s, cnt_s, st_s, cc_s = _mix_stage(x_sample, state_hgrn[0], c0_s, w, nseq=bs, tseq=ts,
                                                             chunk=ts)

    nb_p, nb_s = n_p // tb_p, 1
    nb = nb_p + nb_s
    cnt = jnp.concatenate([cnt_p[:, :, 0], cnt_s[:, :, 0]], axis=0)
    blk_row0 = jnp.concatenate([jnp.arange(nb_p, dtype=I32) * (TOP_K * tb_p), jnp.zeros((nb_s,), I32)])
    tm = 512
    nt = -(-(TOP_K * n_tok) // tm) + N_EXPERTS
    t = _dispatch_tables(cnt, blk_row0, tm, nt)

    zeros = jnp.zeros((tm * SUBLANES, LANES), F32)
    moe_tabs = (t["tile_e"], t["tile_lo"], t["tile_valid"], t["g_lo"], t["g_hi"], t["w_slot"], t["e_next"],
                t["pre"], t["cnt"], t["src"])
    ys = _moe_stage(xs_p, xs_s, zeros, moe_tabs, w_gate_up[0], b_gate_up[0][:, None, :], w_down[0],
                    b_down[0][:, None, :], tm=tm, nt=nt, nb=nb, nb_p=nb_p)

    comb_tabs = (t["pre"], t["cnt"], t["lst"], t["ys0"])
    y_p = _combine_stage(x1_p, ys, slot_p, gate_p, p_prompt[0].reshape(n_p, PLE_DIM),
                         comb_tabs, w, tc=tb_p, nb=nb, g0=0, blk_per_step=1, rows_per_blk=TOP_K * tb_p)
    y_s = _combine_stage(x1_s, ys, slot_s, gate_s, p_sample[0].reshape(n_s, PLE_DIM),
                         comb_tabs, w, tc=n_s, nb=nb, g0=nb_p, blk_per_step=1, rows_per_blk=TOP_K * n_s)

    return (y_p.reshape(bp, tp, D_MODEL), y_s.reshape(bs, ts, D_MODEL),
            st_p[None], cc_p[None], st_s[None], cc_s[None])
```

```python
import functools

import jax
import jax.numpy as jnp
from jax import lax
from jax.experimental import pallas as pl
from jax.experimental.pallas import tpu as pltpu

D_MODEL = 1024
HGRN_WIDTH = 512
HEAD_DIM = 128
N_HEADS = HGRN_WIDTH // HEAD_DIM
CONV_CH = 512
CONV_WIDTH = 31
CACHE_ROWS = CONV_WIDTH - 1
CACHE_PAD = 32
IN_COLS = 4 * HGRN_WIDTH + 2 * CONV_CH
N_EXPERTS = 32
TOP_K = 4
D_FF = 1024
SWIGLU_LIMIT = 7.0
SWIGLU_ALPHA = 1.702
PLE_DIM = 256
EPS = 1e-6

SUBLANES = 8
LANES = 128
MOE_ROWS = 256
VMEM_LIMIT_BYTES = 56 * 1024 * 1024

F32 = jnp.float32
BF16 = jnp.bfloat16
I32 = jnp.int32


def _split3(a):
    p1 = a.astype(BF16)
    r1 = a - p1.astype(F32)
    p2 = r1.astype(BF16)
    r2 = r1 - p2.astype(F32)
    return p1, p2, r2.astype(BF16)


def _dot(a, b):
    return jnp.dot(a, b, preferred_element_type=F32)


def _dot_nt(a, b):
    return lax.dot_general(a, b, (((1,), (1,)), ((), ())), preferred_element_type=F32)


def _dot_tn(a, b):
    return lax.dot_general(a, b, (((0,), (0,)), ((), ())), preferred_element_type=F32)


def _rms(x, g):
    return x * lax.rsqrt(jnp.mean(x * x, axis=-1, keepdims=True) + EPS) * g


def _copy_rows(src_hbm, src_row, dst_vmem, dst_row, n, sem):
    @pl.when(n > 0)
    def _():
        size = pl.multiple_of(n * SUBLANES, SUBLANES)
        pltpu.make_async_copy(src_hbm.at[pl.ds(pl.multiple_of(src_row * SUBLANES, SUBLANES), size)],
                              dst_vmem.at[pl.ds(pl.multiple_of(dst_row * SUBLANES, SUBLANES), size)], sem).start()


def _to_row_tiles(ref, x, base=0):
    rows = x.shape[0]
    for s in range(D_MODEL // LANES):
        ref[pl.ds(base * SUBLANES + s, rows, stride=SUBLANES), :] = x[:, s * LANES:(s + 1) * LANES]


def _from_row_tiles(ref, rows, base=0, dtype=BF16):
    return jnp.concatenate(
        [ref[pl.ds(base * SUBLANES + s, rows, stride=SUBLANES), :].astype(dtype) for s in range(D_MODEL // LANES)],
        axis=1)


def _hgrn_chunk(q, kk, f, b, v, st, shift_ref, ones_sq, pair_ref):
    c = q.shape[0]
    sc = None
    n_mask = 0
    m = c // 2
    while m >= SUBLANES:
        pieces = []
        for blk in range(c // (2 * m)):
            r = blk * 2 * m + m - 1
            pieces.append(jnp.broadcast_to(b[r:r + 1, :], (2 * m, HEAD_DIM)))
        bmid = pieces[0] if len(pieces) == 1 else jnp.concatenate(pieces, axis=0)
        e_m = jnp.exp(-jnp.abs(b - bmid))
        qh = (q * e_m).astype(BF16)
        kh = (kk * e_m).astype(BF16)
        term = pair_ref[n_mask] * _dot_nt(qh, kh)
        sc = term if sc is None else sc + term
        n_mask += 1
        m //= 2

    shift_ref[0, SUBLANES:SUBLANES + c, :] = f
    shift_ref[1, SUBLANES:SUBLANES + c, :] = kk
    decay = None
    for d in range(SUBLANES):
        if d == 0:
            kd = kk
        else:
            f_sh = shift_ref[0, SUBLANES - (d - 1):SUBLANES - (d - 1) + c, :]
            decay = f_sh if decay is None else decay * f_sh
            kd = shift_ref[1, SUBLANES - d:SUBLANES - d + c, :] * decay
        s_d = _dot((q * kd).astype(BF16), ones_sq)
        term = pair_ref[n_mask + d] * s_d[:, :c]
        sc = term if sc is None else sc + term

    o = _dot(sc.astype(BF16), v.astype(BF16))
    o = o + _dot_nt((q * jnp.exp(b)).astype(BF16), st.astype(BF16))
    b_last = b[c - 1:c, :]
    kdec = (kk * jnp.exp(b_last - b)).astype(BF16)
    st_new = st * jnp.exp(b_last) + _dot_tn(v.astype(BF16), kdec)
    return o, st_new


def _mix_body(x_ref, s0_ref, c0_ref, nm_ref, win_ref, lb_ref, hn_ref, cw_ref, cb_ref, lng_ref,
              lnb_ref, wout_ref, nf_ref, wrh_ref, wrl_ref, br_ref, tri_ref, before_ref, lower_ref, ones_ref, pair_ref,
              x1_ref, xs_ref, slot_ref, gate_ref, cnt_ref, sout_ref, cout_ref,
              st_scr, upad_scr, ush_scr, shift_scr, mix_scr, carry_scr, *, tb, chunk, nj, nblk, nseq):
    pipelined = nblk > 1
    tseq = tb // nseq
    s = pl.program_id(0)
    j = lax.rem(jnp.minimum(s, nblk - 1), nj)

    if pipelined:
        @pl.when(s == 0)
        def _():
            carry_scr[...] = jnp.zeros((2, tb, D_MODEL), BF16)

    @pl.when(j == 0)
    def _init():
        for q in range(nseq):
            for h in range(N_HEADS):
                st_scr[q * N_HEADS + h] = s0_ref[q, h].T
            upad_scr[q, 0:CACHE_PAD, :] = c0_ref[q]
        shift_scr[:, 0:SUBLANES, :] = jnp.zeros((2, SUBLANES, HEAD_DIM), F32)

    def router(xh, xl):
        return _route_block(xh, xl, wrh_ref, wrl_ref, br_ref, before_ref, lower_ref,
                            xs_ref, slot_ref, gate_ref, cnt_ref, tb=tb)

    route = router(carry_scr[0], carry_scr[1]) if pipelined else iter(())

    def advance_route(n=1):
        for _ in range(n):
            next(route, None)

    advance_route()
    x = x_ref[...].reshape(tb, D_MODEL)
    h_in = _rms(x, nm_ref[...]).astype(BF16)
    proj = _dot(h_in, win_ref[...])
    advance_route(2)

    q_all = proj[:, 0:HGRN_WIDTH]
    fz = proj[:, HGRN_WIDTH:2 * HGRN_WIDTH]
    v_all = proj[:, 2 * HGRN_WIDTH:3 * HGRN_WIDTH]
    g_all = proj[:, 3 * HGRN_WIDTH:4 * HGRN_WIDTH]
    lb = lb_ref[...]
    f_all = lb + (1.0 - lb) * jax.nn.sigmoid(fz)
    lf_all = jnp.log(f_all)
    kk_all = 1.0 - f_all

    tri = tri_ref[...]
    p1, p2, p3 = _split3(lf_all)
    b_all = _dot(tri, p1) + _dot(tri, p2) + _dot(tri, p3)
    advance_route(2)

    ones_sq = ones_ref[...]
    hn = hn_ref[...]
    per_seq = tseq // chunk
    for h in range(N_HEADS):
        ls = slice(h * HEAD_DIM, (h + 1) * HEAD_DIM)
        for ci in range(tb // chunk):
            q, cq = divmod(ci, per_seq)
            if cq == 0:
                st = st_scr[q * N_HEADS + h]
            rs = slice(ci * chunk, (ci + 1) * chunk)
            o, st = _hgrn_chunk(q_all[rs, ls], kk_all[rs, ls], f_all[rs, ls], b_all[rs, ls],
                                v_all[rs, ls], st, shift_scr, ones_sq, pair_ref)
            o = _rms(o, hn[:, ls])
            mix_scr[rs, ls] = (o * jax.nn.silu(g_all[rs, ls])).astype(BF16)
            if cq == per_seq - 1:
                st_scr[q * N_HEADS + h] = st
            advance_route()

    u = proj[:, 4 * HGRN_WIDTH:4 * HGRN_WIDTH + CONV_CH] * jax.nn.sigmoid(
        proj[:, 4 * HGRN_WIDTH + CONV_CH:IN_COLS])
    first = CACHE_PAD - CACHE_ROWS
    span = tseq + CACHE_PAD - SUBLANES
    accs = []
    for q in range(nseq):
        upad = upad_scr.at[q]
        upad[CACHE_PAD:CACHE_PAD + tseq, :] = u[q * tseq:(q + 1) * tseq, :]
        for r in range(1, SUBLANES):
            ush_scr[r, 0:span, :] = upad[r:r + span, :]
        acc = jnp.zeros((tseq, CONV_CH), F32) + cb_ref[...]
        for tap in range(CONV_WIDTH):
            a, r = divmod(first + tap, SUBLANES)
            rows = slice(a * SUBLANES, a * SUBLANES + tseq)
            shifted = upad[rows, :] if r == 0 else ush_scr[r, rows, :]
            acc = acc + cw_ref[tap:tap + 1, :] * shifted
        accs.append(acc)
        new_cache = upad[tseq:tseq + CACHE_PAD, :]
        upad[0:CACHE_PAD, :] = new_cache
    acc = accs[0] if nseq == 1 else jnp.concatenate(accs, axis=0)
    mu = jnp.mean(acc, axis=-1, keepdims=True)
    cen = acc - mu
    var = jnp.mean(cen * cen, axis=-1, keepdims=True)
    cn = cen * lax.rsqrt(var + EPS) * lng_ref[...] + lnb_ref[...]
    mix_scr[:, HGRN_WIDTH:HGRN_WIDTH + CONV_CH] = jax.nn.silu(cn).astype(BF16)
    for _ in route:
        pass

    x1 = x + _dot(mix_scr[...], wout_ref[...])
    x1_ref[...] = x1
    xn = _rms(x1, nf_ref[...])
    xh = xn.astype(BF16)
    xl = (xn - xh.astype(F32)).astype(BF16)
    if pipelined:
        carry_scr[0] = xh
        carry_scr[1] = xl
    else:
        for _ in router(xh, xl):
            pass

    @pl.when((s < nblk) & (j == nj - 1))
    def _fin():
        for q in range(nseq):
            for h in range(N_HEADS):
                sout_ref[q, h] = st_scr[q * N_HEADS + h].T
            cout_ref[q] = upad_scr[q, tseq + first:tseq + CACHE_PAD, :]


def _route_block(xh, xl, wrh_ref, wrl_ref, br_ref, before_ref, lower_ref, xs_ref, slot_ref, gate_ref, cnt_ref, *, tb):
    wrh = wrh_ref[...]
    logits = _dot_nt(wrh, xh) + _dot_nt(wrh, xl) + _dot_nt(wrl_ref[...], xh) + br_ref[...]
    yield

    eidx = lax.broadcasted_iota(I32, (N_EXPERTS, tb), 0).astype(F32)
    vals, hots = [], []
    work = logits
    for _ in range(TOP_K):
        mx = jnp.max(work, axis=0, keepdims=True)
        pick = jnp.min(jnp.where(work == mx, eidx, float(N_EXPERTS)), axis=0, keepdims=True)
        hot = eidx == pick
        vals.append(mx)
        hots.append(hot)
        work = jnp.where(hot, -jnp.inf, work)
        yield
    exps = [jnp.exp(val - vals[0]) for val in vals]
    inv = 1.0 / (exps[0] + exps[1] + exps[2] + exps[3])
    gate_ref[0] = jnp.concatenate([e * inv for e in exps], axis=0)

    before = before_ref[...]
    lower = lower_ref[...]
    hot_f = [hot.astype(F32) for hot in hots]
    tots = [jnp.sum(hf, axis=1, keepdims=True) for hf in hot_f]
    cnt = tots[0] + tots[1] + tots[2] + tots[3]
    cnt_ref[0] = cnt.astype(I32)
    start = _dot(lower, jnp.broadcast_to(cnt, (N_EXPERTS, LANES)).astype(BF16))[:, 0:1]
    yield
    slots = []
    base = start
    for k in range(TOP_K):
        pre = _dot(hot_f[k].astype(BF16), before)
        slots.append(jnp.sum(hot_f[k] * (base + pre), axis=0, keepdims=True))
        base = base + tots[k]
        yield
    slot_ref[0] = jnp.concatenate(slots, axis=0).astype(I32)

    srow = lax.broadcasted_iota(I32, (TOP_K * tb, tb), 0).astype(F32)
    perm = (srow == slots[0])
    for k in range(1, TOP_K):
        perm = perm | (srow == slots[k])
    yield
    xs = _dot(perm.astype(BF16), xh)
    yield
    _to_row_tiles(xs_ref, xs)


def _mix_stage(x, s0, c0, w, *, nseq, tseq, chunk):
    bsz, t, _ = x.shape
    tb = nseq * tseq
    nj = t // tseq
    nblk = (bsz // nseq) * nj
    pipelined = nblk > 1
    n_steps = nblk + 1 if pipelined else 1

    def full(a):
        return pl.BlockSpec(a.shape, lambda s: (0,) * a.ndim)

    def cur(s):
        sc = jnp.minimum(s, nblk - 1)
        return sc // nj, sc % nj

    def prev(s):
        return jnp.maximum(s - 1, 0) if pipelined else s

    weights = [w["norm_mix"], w["w_in"], w["lb"], w["hgrn_norm"], w["conv_w"], w["conv_b"], w["ln_g"],
               w["ln_b"], w["w_out"], w["norm_ffn"], w["wr_hi"], w["wr_lo"], w["b_router"]]
    r_t = lax.broadcasted_iota(I32, (tb, tb), 0)
    c_t = lax.broadcasted_iota(I32, (tb, tb), 1)
    r_e = lax.broadcasted_iota(I32, (N_EXPERTS, N_EXPERTS), 0)
    c_e = lax.broadcasted_iota(I32, (N_EXPERTS, N_EXPERTS), 1)
    weights += [((r_t >= c_t) & ((r_t ^ c_t) < chunk)).astype(BF16), (r_t < c_t).astype(BF16),
                (c_e < r_e).astype(BF16), jnp.ones((HEAD_DIM, HEAD_DIM), BF16)]
    r_c = lax.broadcasted_iota(I32, (chunk, chunk), 0)
    c_c = lax.broadcasted_iota(I32, (chunk, chunk), 1)
    x_c = r_c ^ c_c
    pair = []
    m = chunk // 2
    while m >= SUBLANES:
        pair.append((x_c >= m) & (x_c < 2 * m) & (r_c > c_c))
        m //= 2
    pair += [(x_c < SUBLANES) & (r_c - c_c == d) for d in range(SUBLANES)]
    weights.append(jnp.stack(pair).astype(F32))
    in_specs = [
        pl.BlockSpec((nseq, tseq, D_MODEL), lambda s: (*cur(s), 0)),
        pl.BlockSpec((nseq, N_HEADS, HEAD_DIM, HEAD_DIM), lambda s: (cur(s)[0], 0, 0, 0)),
        pl.BlockSpec((nseq, CACHE_PAD, CONV_CH), lambda s: (cur(s)[0], 0, 0)),
    ] + [full(a) for a in weights]
    out_shape = (
        jax.ShapeDtypeStruct((n_steps * tb, D_MODEL), F32),
        jax.ShapeDtypeStruct((TOP_K * nblk * tb * SUBLANES, LANES), F32),
        jax.ShapeDtypeStruct((nblk, TOP_K, tb), I32),
        jax.ShapeDtypeStruct((nblk, TOP_K, tb), F32),
        jax.ShapeDtypeStruct((nblk, N_EXPERTS, 1), I32),
        jax.ShapeDtypeStruct((bsz, N_HEADS, HEAD_DIM, HEAD_DIM), F32),
        jax.ShapeDtypeStruct((bsz, CACHE_ROWS, CONV_CH), F32),
    )
    out_specs = (
        pl.BlockSpec((tb, D_MODEL), lambda s: (s, 0)),
        pl.BlockSpec((TOP_K * tb * SUBLANES, LANES), lambda s: (prev(s), 0)),
        pl.BlockSpec((1, TOP_K, tb), lambda s: (prev(s), 0, 0)),
        pl.BlockSpec((1, TOP_K, tb), lambda s: (prev(s), 0, 0)),
        pl.BlockSpec((1, N_EXPERTS, 1), lambda s: (prev(s), 0, 0)),
        pl.BlockSpec((nseq, N_HEADS, HEAD_DIM, HEAD_DIM), lambda s: (cur(s)[0], 0, 0, 0)),
        pl.BlockSpec((nseq, CACHE_ROWS, CONV_CH), lambda s: (cur(s)[0], 0, 0)),
    )
    return pl.pallas_call(
        functools.partial(_mix_body, tb=tb, chunk=chunk, nj=nj, nblk=nblk, nseq=nseq),
        grid=(n_steps,),
        in_specs=in_specs,
        out_specs=out_specs,
        out_shape=out_shape,
        scratch_shapes=[
            pltpu.VMEM((nseq * N_HEADS, HEAD_DIM, HEAD_DIM), F32),
            pltpu.VMEM((nseq, CACHE_PAD + tseq, CONV_CH), F32),
            pltpu.VMEM((SUBLANES, CACHE_PAD + tseq - SUBLANES, CONV_CH), F32),
            pltpu.VMEM((2, SUBLANES + chunk, HEAD_DIM), F32),
            pltpu.VMEM((tb, D_MODEL), BF16),
            pltpu.VMEM((2, tb, D_MODEL), BF16),
        ],
        compiler_params=pltpu.CompilerParams(
            dimension_semantics=("arbitrary",), vmem_limit_bytes=VMEM_LIMIT_BYTES),
        name="mix",
    )(x, s0, c0, *weights)


def _moe_body(te_ref, tlo_ref, tv_ref, glo_ref, ghi_ref, wslot_ref, enext_ref, pre_ref, cnt_ref, src_ref,
              xs_p_hbm, xs_s_hbm, zeros_hbm, wgu_hbm, bgu_ref, wd_hbm, bd_ref, y_ref,
              xbuf, wgu_f, wd_f, wgu_b, wd_b, gsem, wsem, *, tm, nt, nb, nb_p):
    i = pl.program_id(0)
    slot = lax.rem(i, 2)

    def issue_tile(t, s):
        e = te_ref[t]
        lo = tlo_ref[t]
        nv = tv_ref[t]
        hi = lo + nv
        dst0 = s * tm

        def runs(xs_hbm):
            def body(g, carry):
                p0 = pre_ref[e * nb + g]
                a = jnp.maximum(p0, lo)
                b = jnp.minimum(p0 + cnt_ref[e * nb + g], hi)
                _copy_rows(xs_hbm, src_ref[e * nb + g] + (a - p0), xbuf, dst0 + (a - lo), b - a, gsem.at[s])
                return carry
            return body
        g_lo, g_hi = glo_ref[t], ghi_ref[t]
        lax.fori_loop(g_lo, jnp.minimum(g_hi, nb_p), runs(xs_p_hbm), 0)
        lax.fori_loop(jnp.maximum(g_lo, nb_p), g_hi, runs(xs_s_hbm), 0)
        _copy_rows(zeros_hbm, 0, xbuf, dst0 + nv, tm - nv, gsem.at[s])

    def weight_fetch(e, ws):
        return (pltpu.make_async_copy(wgu_hbm.at[e], wgu_f.at[ws], wsem.at[ws]),
                pltpu.make_async_copy(wd_hbm.at[e], wd_f.at[ws], wsem.at[ws]))

    @pl.when(i == 0)
    def _prologue():
        issue_tile(0, 0)
        for c in weight_fetch(te_ref[0], wslot_ref[0]):
            c.start()

    @pl.when(i + 1 < nt)
    def _prefetch():
        issue_tile(i + 1, 1 - slot)

    used = tv_ref[i] > 0
    new_expert = (i == 0) | (te_ref[i] != te_ref[jnp.maximum(i - 1, 0)])

    @pl.when(used & new_expert)
    def _switch_expert():
        ws = wslot_ref[i]
        for c in weight_fetch(0, ws):
            c.wait()
        nxt = enext_ref[i]

        @pl.when(nxt >= 0)
        def _():
            for c in weight_fetch(nxt, 1 - ws):
                c.start(priority=1)
        rows = 128

        def cast_gu(c, carry):
            r0 = pl.multiple_of(c * rows, rows)
            wgu_b[pl.ds(r0, rows), :] = wgu_f[ws, pl.ds(r0, rows), :].astype(BF16)
            return carry
        lax.fori_loop(0, D_MODEL // rows, cast_gu, 0)

        def cast_d(c, carry):
            r0 = pl.multiple_of(c * rows, rows)
            wd_b[pl.ds(r0, rows), :] = wd_f[ws, pl.ds(r0, rows), :].astype(BF16)
            return carry
        lax.fori_loop(0, D_FF // rows, cast_d, 0)

    tile = pl.ds(pl.multiple_of(slot * tm * SUBLANES, tm * SUBLANES), tm * SUBLANES)
    pltpu.make_async_copy(zeros_hbm, xbuf.at[tile], gsem.at[slot]).wait()

    def expert_ffn(rows):
        xb = _from_row_tiles(xbuf, rows, base=slot * tm)
        gu = _dot(xb, wgu_b[...]) + bgu_ref[...]
        gate = jnp.minimum(gu[:, :D_FF], SWIGLU_LIMIT)
        up = jnp.clip(gu[:, D_FF:], -SWIGLU_LIMIT, SWIGLU_LIMIT)
        hid = (up + 1.0) * (gate * jax.nn.sigmoid(SWIGLU_ALPHA * gate))
        _to_row_tiles(y_ref, _dot(hid.astype(BF16), wd_b[...]) + bd_ref[...])

    nv_i = tv_ref[i]
    sizes = tuple(range(MOE_ROWS, tm + 1, MOE_ROWS))
    for lo, hi in zip((0,) + sizes[:-1], sizes):
        @pl.when((nv_i > lo) & (nv_i <= hi))
        def _(hi=hi):
            expert_ffn(hi)
            if hi < tm:
                y_ref[hi * SUBLANES:tm * SUBLANES, :] = jnp.zeros(((tm - hi) * SUBLANES, LANES), F32)

    @pl.when(jnp.logical_not(used))
    def _unused_tile():
        y_ref[...] = jnp.zeros((tm * SUBLANES, LANES), F32)


def _moe_stage(xs_p, xs_s, zeros, tabs, w_gu, b_gu, w_d, b_d, *, tm, nt, nb, nb_p):
    n_pref = len(tabs)

    def wmap(i, te, *_):
        return (te[i], 0, 0)

    grid_spec = pltpu.PrefetchScalarGridSpec(
        num_scalar_prefetch=n_pref,
        grid=(nt,),
        in_specs=[
            pl.BlockSpec(memory_space=pl.ANY),
            pl.BlockSpec(memory_space=pl.ANY),
            pl.BlockSpec(memory_space=pl.ANY),
            pl.BlockSpec(memory_space=pl.ANY),
            pl.BlockSpec((None, 1, 2 * D_FF), wmap),
            pl.BlockSpec(memory_space=pl.ANY),
            pl.BlockSpec((None, 1, D_MODEL), wmap),
        ],
        out_specs=pl.BlockSpec((tm * SUBLANES, LANES), lambda i, *_: (i, 0)),
        scratch_shapes=[
            pltpu.VMEM((2 * tm * SUBLANES, LANES), F32),
            pltpu.VMEM((2, D_MODEL, 2 * D_FF), F32),
            pltpu.VMEM((2, D_FF, D_MODEL), F32),
            pltpu.VMEM((D_MODEL, 2 * D_FF), BF16),
            pltpu.VMEM((D_FF, D_MODEL), BF16),
            pltpu.SemaphoreType.DMA((2,)),
            pltpu.SemaphoreType.DMA((2,)),
        ],
    )
    return pl.pallas_call(
        functools.partial(_moe_body, tm=tm, nt=nt, nb=nb, nb_p=nb_p),
        grid_spec=grid_spec,
        out_shape=jax.ShapeDtypeStruct((nt * tm * SUBLANES, LANES), F32),
        compiler_params=pltpu.CompilerParams(
            dimension_semantics=("arbitrary",), vmem_limit_bytes=VMEM_LIMIT_BYTES),
        name="moe",
    )(*tabs, xs_p, xs_s, zeros, w_gu, b_gu, w_d, b_d)


def _combine_body(pre_ref, cnt_ref, lst_ref, ys0_ref, y_hbm, x1_ref, slot_ref, g_ref, pe_ref, np_ref, wpg_ref,
                  wple_ref, nfin_ref, out_ref, stag, sem, *, tc, nsteps, nb, g0, blk_per_step, rows_per_blk):
    i = pl.program_id(0)
    slot = lax.rem(i, 2)
    rows = TOP_K * tc

    def issue_step(step, s):
        for bi in range(blk_per_step):
            g = g0 + step * blk_per_step + bi
            dst0 = s * rows + bi * rows_per_blk

            def body(e, carry):
                src = ys0_ref[e] + pre_ref[e * nb + g]
                _copy_rows(y_hbm, src, stag, dst0 + lst_ref[g * N_EXPERTS + e], cnt_ref[e * nb + g], sem.at[s])
                return carry
            lax.fori_loop(0, N_EXPERTS, body, 0)

    @pl.when(i == 0)
    def _prologue():
        issue_step(0, 0)

    @pl.when(i + 1 < nsteps)
    def _prefetch():
        issue_step(i + 1, 1 - slot)

    tile = pl.ds(pl.multiple_of(slot * rows * SUBLANES, rows * SUBLANES), rows * SUBLANES)
    pltpu.make_async_copy(y_hbm.at[pl.ds(0, rows * SUBLANES)], stag.at[tile], sem.at[slot]).wait()

    srow = lax.broadcasted_iota(I32, (rows, tc), 0)
    sl = slot_ref[0]
    g = g_ref[0]
    back_t = jnp.zeros((rows, tc), F32)
    for k in range(TOP_K):
        back_t = jnp.where(srow == sl[k:k + 1, :], g[k:k + 1, :], back_t)
    x2 = x1_ref[...] + _dot_tn(back_t.astype(BF16), _from_row_tiles(stag, rows, base=slot * rows))
    hp = _rms(x2, np_ref[...]).astype(BF16)
    gate = jax.nn.sigmoid(_dot(hp, wpg_ref[...]))
    emb = _dot(pe_ref[...].astype(BF16), wple_ref[...])
    x3 = x2 + gate * emb
    out_ref[...] = _rms(x3, nfin_ref[...])


def _combine_stage(x1, ys, slots, gates, pe, tabs, w, *, tc, nb, g0, blk_per_step, rows_per_blk):
    n = pe.shape[0]
    nsteps = n // tc

    def full(a):
        return pl.BlockSpec(a.shape, lambda i, *_: (0,) * a.ndim)

    weights = [w["norm_ple"], w["w_ple_gate"], w["w_ple"], w["norm_final"]]
    grid_spec = pltpu.PrefetchScalarGridSpec(
        num_scalar_prefetch=len(tabs),
        grid=(nsteps,),
        in_specs=[
            pl.BlockSpec(memory_space=pl.ANY),
            pl.BlockSpec((tc, D_MODEL), lambda i, *_: (i, 0)),
            pl.BlockSpec((1, TOP_K, tc), lambda i, *_: (i, 0, 0)),
            pl.BlockSpec((1, TOP_K, tc), lambda i, *_: (i, 0, 0)),
            pl.BlockSpec((tc, PLE_DIM), lambda i, *_: (i, 0)),
        ] + [full(a) for a in weights],
        out_specs=pl.BlockSpec((tc, D_MODEL), lambda i, *_: (i, 0)),
        scratch_shapes=[
            pltpu.VMEM((2 * TOP_K * tc * SUBLANES, LANES), F32),
            pltpu.SemaphoreType.DMA((2,)),
        ],
    )
    return pl.pallas_call(
        functools.partial(_combine_body, tc=tc, nsteps=nsteps, nb=nb, g0=g0, blk_per_step=blk_per_step,
                          rows_per_blk=rows_per_blk),
        grid_spec=grid_spec,
        out_shape=jax.ShapeDtypeStruct((n, D_MODEL), F32),
        compiler_params=pltpu.CompilerParams(
            dimension_semantics=("arbitrary",), vmem_limit_bytes=VMEM_LIMIT_BYTES),
        name="combine",
    )(*tabs, ys, x1, slots, gates, pe, *weights)


def _dispatch_tables(cnt, blk_row0, tm, nt):
    nb = cnt.shape[0]
    experts = jnp.arange(N_EXPERTS, dtype=I32)
    pre = jnp.cumsum(cnt, axis=0) - cnt
    total = jnp.sum(cnt, axis=0)
    lst = jnp.cumsum(cnt, axis=1) - cnt
    src = blk_row0[:, None] + lst
    n_tiles_e = (total + tm - 1) // tm
    tile_end = jnp.cumsum(n_tiles_e)
    tile_start = tile_end - n_tiles_e
    tiles = jnp.arange(nt, dtype=I32)
    tile_e_raw = jnp.sum((tiles[:, None] >= tile_end[None, :]).astype(I32), axis=1)
    used = tile_e_raw < N_EXPERTS
    has = total > 0
    last_e = jnp.max(jnp.where(has, experts, 0))
    tile_e = jnp.where(used, jnp.minimum(tile_e_raw, N_EXPERTS - 1), last_e).astype(I32)
    hot = tile_e[:, None] == experts[None, :]

    def per_tile(v):
        if v.ndim == 1:
            return jnp.sum(jnp.where(hot, v[None, :], 0), axis=1)
        return jnp.sum(jnp.where(hot[:, :, None], v[None, :, :], 0), axis=1)

    tile_lo = jnp.where(used, (tiles - per_tile(tile_start)) * tm, 0).astype(I32)
    tile_valid = jnp.where(used, jnp.clip(per_tile(total) - tile_lo, 0, tm), 0).astype(I32)
    p_t = per_tile(pre.T)
    c_t = per_tile(cnt.T)
    meets = (p_t < (tile_lo + tile_valid)[:, None]) & ((p_t + c_t) > tile_lo[:, None]) & (c_t > 0)
    gidx = jnp.arange(nb, dtype=I32)[None, :]
    g_lo = jnp.min(jnp.where(meets, gidx, nb), axis=1).astype(I32)
    g_hi = jnp.max(jnp.where(meets, gidx + 1, 0), axis=1).astype(I32)
    g_lo = jnp.minimum(g_lo, g_hi)
    w_slot_e = (jnp.cumsum(has.astype(I32)) - 1) % 2
    later = has[None, :] & (experts[None, :] > experts[:, None])
    e_next = jnp.min(jnp.where(later, experts[None, :], N_EXPERTS), axis=1)
    e_next = jnp.where(e_next >= N_EXPERTS, -1, e_next)
    flat = lambda a: a.T.reshape(-1).astype(I32)
    return dict(tile_e=tile_e, tile_lo=tile_lo, tile_valid=tile_valid, g_lo=g_lo, g_hi=g_hi,
                w_slot=per_tile(w_slot_e).astype(I32), e_next=per_tile(e_next).astype(I32),
                pre=flat(pre), cnt=flat(cnt), src=flat(src), lst=lst.reshape(-1).astype(I32),
                ys0=(tile_start * tm).astype(I32))


def kernel(x_prompt, x_sample, state_hgrn, cache_conv, p_prompt, p_sample, norm_mix, w_in, hgrn_lb_logits, hgrn_norm, conv_w, conv_b, conv_ln_g, conv_ln_b, w_out, norm_ffn, w_router, b_router, w_gate_up, b_gate_up, w_down, b_down, w_ple, norm_ple, w_ple_gate, norm_final):
    depth = norm_mix.shape[0]
    assert depth == 1
    bp, tp, _ = x_prompt.shape
    bs, ts, _ = x_sample.shape
    n_p, n_s = bp * tp, bs * ts
    n_tok = n_p + n_s

    lb_all = jnp.cumsum(jax.nn.softmax(hgrn_lb_logits.astype(F32), axis=0), axis=0)
    wr_t = w_router[0].T
    wr_hi = wr_t.astype(BF16)
    w = {
        "norm_mix": norm_mix[0][None, :], "w_in": w_in[0].astype(BF16), "lb": lb_all[0][None, :],
        "hgrn_norm": hgrn_norm[0][None, :], "conv_w": conv_w[0], "conv_b": conv_b[0][None, :],
        "ln_g": conv_ln_g[0][None, :], "ln_b": conv_ln_b[0][None, :], "w_out": w_out[0].astype(BF16),
        "norm_ffn": norm_ffn[0][None, :], "wr_hi": wr_hi, "wr_lo": (wr_t - wr_hi.astype(F32)).astype(BF16),
        "b_router": b_router[0][:, None],
        "norm_ple": norm_ple[0][None, :], "w_ple_gate": w_ple_gate[0].astype(BF16),
        "w_ple": w_ple[0].astype(BF16), "norm_final": norm_final[None, :],
    }

    pad = ((0, 0), (CACHE_PAD - CACHE_ROWS, 0), (0, 0))
    s0_p = jnp.zeros((bp, N_HEADS, HEAD_DIM, HEAD_DIM), F32)
    c0_p = jnp.zeros((bp, CACHE_PAD, CONV_CH), F32)
    c0_s = jnp.pad(cache_conv[0], pad)

    tb_p = 256
    x1_p, xs_p, slot_p, gate_p, cnt_p, st_p, cc_p = _mix_stage(x_prompt, s0_p, c0_p, w, nseq=1, tseq=tb_p, chunk=128)
    x1_s, xs_s, slot_s, gate_s, cnt_s, st_s, cc_s = _mix_stage(x_sample, state_hgrn[0], c0_s, w, nseq=bs, tseq=ts,
                                                             chunk=ts)

    nb_p, nb_s = n_p // tb_p, 1
    nb = nb_p + nb_s
    cnt = jnp.concatenate([cnt_p[:, :, 0], cnt_s[:, :, 0]], axis=0)
    blk_row0 = jnp.concatenate([jnp.arange(nb_p, dtype=I32) * (TOP_K * tb_p), jnp.zeros((nb_s,), I32)])
    tm = 3 * MOE_ROWS
    nt = -(-(TOP_K * n_tok) // tm) + N_EXPERTS
    t = _dispatch_tables(cnt, blk_row0, tm, nt)

    zeros = jnp.zeros((tm * SUBLANES, LANES), F32)
    moe_tabs = (t["tile_e"], t["tile_lo"], t["tile_valid"], t["g_lo"], t["g_hi"], t["w_slot"], t["e_next"],
                t["pre"], t["cnt"], t["src"])
    ys = _moe_stage(xs_p, xs_s, zeros, moe_tabs, w_gate_up[0], b_gate_up[0][:, None, :], w_down[0],
                    b_down[0][:, None, :], tm=tm, nt=nt, nb=nb, nb_p=nb_p)

    comb_tabs = (t["pre"], t["cnt"], t["lst"], t["ys0"])
    y_p = _combine_stage(x1_p, ys, slot_p, gate_p, p_prompt[0].reshape(n_p, PLE_DIM),
                         comb_tabs, w, tc=tb_p, nb=nb, g0=0, blk_per_step=1, rows_per_blk=TOP_K * tb_p)
    y_s = _combine_stage(x1_s, ys, slot_s, gate_s, p_sample[0].reshape(n_s, PLE_DIM),
                         comb_tabs, w, tc=n_s, nb=nb, g0=nb_p, blk_per_step=1, rows_per_blk=TOP_K * n_s)

    return (y_p.reshape(bp, tp, D_MODEL), y_s.reshape(bs, ts, D_MODEL),
            st_p[None], cc_p[None], st_s[None], cc_s[None])
```

```python
import functools

import jax
import jax.numpy as jnp
from jax import lax
from jax.experimental import pallas as pl
from jax.experimental.pallas import tpu as pltpu

D_MODEL = 1024
HGRN_WIDTH = 512
HEAD_DIM = 128
N_HEADS = HGRN_WIDTH // HEAD_DIM
CONV_CH = 512
CONV_WIDTH = 31
CACHE_ROWS = CONV_WIDTH - 1
CACHE_PAD = 32
IN_COLS = 4 * HGRN_WIDTH + 2 * CONV_CH
N_EXPERTS = 32
TOP_K = 4
D_FF = 1024
SWIGLU_LIMIT = 7.0
SWIGLU_ALPHA = 1.702
PLE_DIM = 256
EPS = 1e-6

SUBLANES = 8
LANES = 128
MOE_ROWS = 256
VMEM_LIMIT_BYTES = 56 * 1024 * 1024

F32 = jnp.float32
BF16 = jnp.bfloat16
I32 = jnp.int32


def _split3(a):
    p1 = a.astype(BF16)
    r1 = a - p1.astype(F32)
    p2 = r1.astype(BF16)
    r2 = r1 - p2.astype(F32)
    return p1, p2, r2.astype(BF16)


def _dot(a, b):
    return jnp.dot(a, b, preferred_element_type=F32)


def _dot_nt(a, b):
    return lax.dot_general(a, b, (((1,), (1,)), ((), ())), preferred_element_type=F32)


def _dot_tn(a, b):
    return lax.dot_general(a, b, (((0,), (0,)), ((), ())), preferred_element_type=F32)


def _rms(x, g):
    return x * lax.rsqrt(jnp.mean(x * x, axis=-1, keepdims=True) + EPS) * g


def _copy_rows(src_hbm, src_row, dst_vmem, dst_row, n, sem):
    @pl.when(n > 0)
    def _():
        size = pl.multiple_of(n * SUBLANES, SUBLANES)
        pltpu.make_async_copy(src_hbm.at[pl.ds(pl.multiple_of(src_row * SUBLANES, SUBLANES), size)],
                              dst_vmem.at[pl.ds(pl.multiple_of(dst_row * SUBLANES, SUBLANES), size)], sem).start()


def _to_row_tiles(ref, x, base=0):
    rows = x.shape[0]
    for s in range(D_MODEL // LANES):
        ref[pl.ds(base * SUBLANES + s, rows, stride=SUBLANES), :] = x[:, s * LANES:(s + 1) * LANES]


def _from_row_tiles(ref, rows, base=0, dtype=BF16):
    return jnp.concatenate(
        [ref[pl.ds(base * SUBLANES + s, rows, stride=SUBLANES), :].astype(dtype) for s in range(D_MODEL // LANES)],
        axis=1)


def _hgrn_chunk(q, kk, f, b, v, st, shift_ref, ones_sq, pair_ref):
    c = q.shape[0]
    sc = None
    n_mask = 0
    m = c // 2
    while m >= SUBLANES:
        pieces = []
        for blk in range(c // (2 * m)):
            r = blk * 2 * m + m - 1
            pieces.append(jnp.broadcast_to(b[r:r + 1, :], (2 * m, HEAD_DIM)))
        bmid = pieces[0] if len(pieces) == 1 else jnp.concatenate(pieces, axis=0)
        e_m = jnp.exp(-jnp.abs(b - bmid))
        qh = (q * e_m).astype(BF16)
        kh = (kk * e_m).astype(BF16)
        term = pair_ref[n_mask] * _dot_nt(qh, kh)
        sc = term if sc is None else sc + term
        n_mask += 1
        m //= 2

    shift_ref[0, SUBLANES:SUBLANES + c, :] = f
    shift_ref[1, SUBLANES:SUBLANES + c, :] = kk
    decay = None
    for d in range(SUBLANES):
        if d == 0:
            kd = kk
        else:
            f_sh = shift_ref[0, SUBLANES - (d - 1):SUBLANES - (d - 1) + c, :]
            decay = f_sh if decay is None else decay * f_sh
            kd = shift_ref[1, SUBLANES - d:SUBLANES - d + c, :] * decay
        s_d = _dot((q * kd).astype(BF16), ones_sq)
        term = pair_ref[n_mask + d] * s_d[:, :c]
        sc = term if sc is None else sc + term

    o = _dot(sc.astype(BF16), v.astype(BF16))
    o = o + _dot_nt((q * jnp.exp(b)).astype(BF16), st.astype(BF16))
    b_last = b[c - 1:c, :]
    kdec = (kk * jnp.exp(b_last - b)).astype(BF16)
    st_new = st * jnp.exp(b_last) + _dot_tn(v.astype(BF16), kdec)
    return o, st_new


def _mix_body(x_ref, s0_ref, c0_ref, nm_ref, win_ref, lb_ref, hn_ref, cw_ref, cb_ref, lng_ref,
              lnb_ref, wout_ref, nf_ref, wrh_ref, wrl_ref, br_ref, tri_ref, before_ref, lower_ref, ones_ref, pair_ref,
              x1_ref, xs_ref, slot_ref, gate_ref, cnt_ref, sout_ref, cout_ref,
              st_scr, upad_scr, ush_scr, shift_scr, mix_scr, carry_scr, *, tb, chunk, nj, nblk, nseq):
    pipelined = nblk > 1
    tseq = tb // nseq
    s = pl.program_id(0)
    j = lax.rem(jnp.minimum(s, nblk - 1), nj)

    if pipelined:
        @pl.when(s == 0)
        def _():
            carry_scr[...] = jnp.zeros((2, tb, D_MODEL), BF16)

    @pl.when(j == 0)
    def _init():
        for q in range(nseq):
            for h in range(N_HEADS):
                st_scr[q * N_HEADS + h] = s0_ref[q, h].T
            upad_scr[q, 0:CACHE_PAD, :] = c0_ref[q]
        shift_scr[:, 0:SUBLANES, :] = jnp.zeros((2, SUBLANES, HEAD_DIM), F32)

    def router(xh, xl):
        return _route_block(xh, xl, wrh_ref, wrl_ref, br_ref, before_ref, lower_ref,
                            xs_ref, slot_ref, gate_ref, cnt_ref, tb=tb)

    route = router(carry_scr[0], carry_scr[1]) if pipelined else iter(())

    def advance_route(n=1):
        for _ in range(n):
            next(route, None)

    advance_route()
    x = x_ref[...].reshape(tb, D_MODEL)
    h_in = _rms(x, nm_ref[...]).astype(BF16)
    proj = _dot(h_in, win_ref[...])
    advance_route(2)

    q_all = proj[:, 0:HGRN_WIDTH]
    fz = proj[:, HGRN_WIDTH:2 * HGRN_WIDTH]
    v_all = proj[:, 2 * HGRN_WIDTH:3 * HGRN_WIDTH]
    g_all = proj[:, 3 * HGRN_WIDTH:4 * HGRN_WIDTH]
    lb = lb_ref[...]
    f_all = lb + (1.0 - lb) * jax.nn.sigmoid(fz)
    lf_all = jnp.log(f_all)
    kk_all = 1.0 - f_all

    tri = tri_ref[...]
    p1, p2, p3 = _split3(lf_all)
    b_all = _dot(tri, p1) + _dot(tri, p2) + _dot(tri, p3)
    advance_route(2)

    ones_sq = ones_ref[...]
    hn = hn_ref[...]
    per_seq = tseq // chunk
    for h in range(N_HEADS):
        ls = slice(h * HEAD_DIM, (h + 1) * HEAD_DIM)
        for ci in range(tb // chunk):
            q, cq = divmod(ci, per_seq)
            if cq == 0:
                st = st_scr[q * N_HEADS + h]
            rs = slice(ci * chunk, (ci + 1) * chunk)
            o, st = _hgrn_chunk(q_all[rs, ls], kk_all[rs, ls], f_all[rs, ls], b_all[rs, ls],
                                v_all[rs, ls], st, shift_scr, ones_sq, pair_ref)
            o = _rms(o, hn[:, ls])
            mix_scr[rs, ls] = (o * jax.nn.silu(g_all[rs, ls])).astype(BF16)
            if cq == per_seq - 1:
                st_scr[q * N_HEADS + h] = st
            advance_route()

    u = proj[:, 4 * HGRN_WIDTH:4 * HGRN_WIDTH + CONV_CH] * jax.nn.sigmoid(
        proj[:, 4 * HGRN_WIDTH + CONV_CH:IN_COLS])
    first = CACHE_PAD - CACHE_ROWS
    span = tseq + CACHE_PAD - SUBLANES
    accs = []
    for q in range(nseq):
        upad = upad_scr.at[q]
        upad[CACHE_PAD:CACHE_PAD + tseq, :] = u[q * tseq:(q + 1) * tseq, :]
        for r in range(1, SUBLANES):
            ush_scr[r, 0:span, :] = upad[r:r + span, :]
        acc = jnp.zeros((tseq, CONV_CH), F32) + cb_ref[...]
        for tap in range(CONV_WIDTH):
            a, r = divmod(first + tap, SUBLANES)
            rows = slice(a * SUBLANES, a * SUBLANES + tseq)
            shifted = upad[rows, :] if r == 0 else ush_scr[r, rows, :]
            acc = acc + cw_ref[tap:tap + 1, :] * shifted
        accs.append(acc)
        new_cache = upad[tseq:tseq + CACHE_PAD, :]
        upad[0:CACHE_PAD, :] = new_cache
    acc = accs[0] if nseq == 1 else jnp.concatenate(accs, axis=0)
    mu = jnp.mean(acc, axis=-1, keepdims=True)
    cen = acc - mu
    var = jnp.mean(cen * cen, axis=-1, keepdims=True)
    cn = cen * lax.rsqrt(var + EPS) * lng_ref[...] + lnb_ref[...]
    mix_scr[:, HGRN_WIDTH:HGRN_WIDTH + CONV_CH] = jax.nn.silu(cn).astype(BF16)
    for _ in route:
        pass

    x1 = x + _dot(mix_scr[...], wout_ref[...])
    x1_ref[...] = x1
    xn = _rms(x1, nf_ref[...])
    xh = xn.astype(BF16)
    xl = (xn - xh.astype(F32)).astype(BF16)
    if pipelined:
        carry_scr[0] = xh
        carry_scr[1] = xl
    else:
        for _ in router(xh, xl):
            pass

    @pl.when((s < nblk) & (j == nj - 1))
    def _fin():
        for q in range(nseq):
            for h in range(N_HEADS):
                sout_ref[q, h] = st_scr[q * N_HEADS + h].T
            cout_ref[q] = upad_scr[q, tseq + first:tseq + CACHE_PAD, :]


def _route_block(xh, xl, wrh_ref, wrl_ref, br_ref, before_ref, lower_ref, xs_ref, slot_ref, gate_ref, cnt_ref, *, tb):
    wrh = wrh_ref[...]
    logits = _dot_nt(wrh, xh) + _dot_nt(wrh, xl) + _dot_nt(wrl_ref[...], xh) + br_ref[...]
    yield

    eidx = lax.broadcasted_iota(I32, (N_EXPERTS, tb), 0).astype(F32)
    vals, hots = [], []
    work = logits
    for _ in range(TOP_K):
        mx = jnp.max(work, axis=0, keepdims=True)
        pick = jnp.min(jnp.where(work == mx, eidx, float(N_EXPERTS)), axis=0, keepdims=True)
        hot = eidx == pick
        vals.append(mx)
        hots.append(hot)
        work = jnp.where(hot, -jnp.inf, work)
        yield
    exps = [jnp.exp(val - vals[0]) for val in vals]
    inv = 1.0 / (exps[0] + exps[1] + exps[2] + exps[3])
    gate_ref[0] = jnp.concatenate([e * inv for e in exps], axis=0)

    before = before_ref[...]
    lower = lower_ref[...]
    hot_f = [hot.astype(F32) for hot in hots]
    tots = [jnp.sum(hf, axis=1, keepdims=True) for hf in hot_f]
    cnt = tots[0] + tots[1] + tots[2] + tots[3]
    cnt_ref[0] = cnt.astype(I32)
    start = _dot(lower, jnp.broadcast_to(cnt, (N_EXPERTS, LANES)).astype(BF16))[:, 0:1]
    yield
    slots = []
    base = start
    for k in range(TOP_K):
        pre = _dot(hot_f[k].astype(BF16), before)
        slots.append(jnp.sum(hot_f[k] * (base + pre), axis=0, keepdims=True))
        base = base + tots[k]
        yield
    slot_ref[0] = jnp.concatenate(slots, axis=0).astype(I32)

    srow = lax.broadcasted_iota(I32, (TOP_K * tb, tb), 0).astype(F32)
    perm = (srow == slots[0])
    for k in range(1, TOP_K):
        perm = perm | (srow == slots[k])
    yield
    xs = _dot(perm.astype(BF16), xh)
    yield
    _to_row_tiles(xs_ref, xs)


def _mix_stage(x, s0, c0, w, *, nseq, tseq, chunk):
    bsz, t, _ = x.shape
    tb = nseq * tseq
    nj = t // tseq
    nblk = (bsz // nseq) * nj
    pipelined = nblk > 1
    n_steps = nblk + 1 if pipelined else 1

    def full(a):
        return pl.BlockSpec(a.shape, lambda s: (0,) * a.ndim)

    def cur(s):
        sc = jnp.minimum(s, nblk - 1)
        return sc // nj, sc % nj

    def prev(s):
        return jnp.maximum(s - 1, 0) if pipelined else s

    weights = [w["norm_mix"], w["w_in"], w["lb"], w["hgrn_norm"], w["conv_w"], w["conv_b"], w["ln_g"],
               w["ln_b"], w["w_out"], w["norm_ffn"], w["wr_hi"], w["wr_lo"], w["b_router"]]
    r_t = lax.broadcasted_iota(I32, (tb, tb), 0)
    c_t = lax.broadcasted_iota(I32, (tb, tb), 1)
    r_e = lax.broadcasted_iota(I32, (N_EXPERTS, N_EXPERTS), 0)
    c_e = lax.broadcasted_iota(I32, (N_EXPERTS, N_EXPERTS), 1)
    weights += [((r_t >= c_t) & ((r_t ^ c_t) < chunk)).astype(BF16), (r_t < c_t).astype(BF16),
                (c_e < r_e).astype(BF16), jnp.ones((HEAD_DIM, HEAD_DIM), BF16)]
    r_c = lax.broadcasted_iota(I32, (chunk, chunk), 0)
    c_c = lax.broadcasted_iota(I32, (chunk, chunk), 1)
    x_c = r_c ^ c_c
    pair = []
    m = chunk // 2
    while m >= SUBLANES:
        pair.append((x_c >= m) & (x_c < 2 * m) & (r_c > c_c))
        m //= 2
    pair += [(x_c < SUBLANES) & (r_c - c_c == d) for d in range(SUBLANES)]
    weights.append(jnp.stack(pair).astype(F32))
    in_specs = [
        pl.BlockSpec((nseq, tseq, D_MODEL), lambda s: (*cur(s), 0)),
        pl.BlockSpec((nseq, N_HEADS, HEAD_DIM, HEAD_DIM), lambda s: (cur(s)[0], 0, 0, 0)),
        pl.BlockSpec((nseq, CACHE_PAD, CONV_CH), lambda s: (cur(s)[0], 0, 0)),
    ] + [full(a) for a in weights]
    out_shape = (
        jax.ShapeDtypeStruct((n_steps * tb, D_MODEL), F32),
        jax.ShapeDtypeStruct((TOP_K * nblk * tb * SUBLANES, LANES), F32),
        jax.ShapeDtypeStruct((nblk, TOP_K, tb), I32),
        jax.ShapeDtypeStruct((nblk, TOP_K, tb), F32),
        jax.ShapeDtypeStruct((nblk, N_EXPERTS, 1), I32),
        jax.ShapeDtypeStruct((bsz, N_HEADS, HEAD_DIM, HEAD_DIM), F32),
        jax.ShapeDtypeStruct((bsz, CACHE_ROWS, CONV_CH), F32),
    )
    out_specs = (
        pl.BlockSpec((tb, D_MODEL), lambda s: (s, 0)),
        pl.BlockSpec((TOP_K * tb * SUBLANES, LANES), lambda s: (prev(s), 0)),
        pl.BlockSpec((1, TOP_K, tb), lambda s: (prev(s), 0, 0)),
        pl.BlockSpec((1, TOP_K, tb), lambda s: (prev(s), 0, 0)),
        pl.BlockSpec((1, N_EXPERTS, 1), lambda s: (prev(s), 0, 0)),
        pl.BlockSpec((nseq, N_HEADS, HEAD_DIM, HEAD_DIM), lambda s: (cur(s)[0], 0, 0, 0)),
        pl.BlockSpec((nseq, CACHE_ROWS, CONV_CH), lambda s: (cur(s)[0], 0, 0)),
    )
    return pl.pallas_call(
        functools.partial(_mix_body, tb=tb, chunk=chunk, nj=nj, nblk=nblk, nseq=nseq),
        grid=(n_steps,),
        in_specs=in_specs,
        out_specs=out_specs,
        out_shape=out_shape,
        scratch_shapes=[
            pltpu.VMEM((nseq * N_HEADS, HEAD_DIM, HEAD_DIM), F32),
            pltpu.VMEM((nseq, CACHE_PAD + tseq, CONV_CH), F32),
            pltpu.VMEM((SUBLANES, CACHE_PAD + tseq - SUBLANES, CONV_CH), F32),
            pltpu.VMEM((2, SUBLANES + chunk, HEAD_DIM), F32),
            pltpu.VMEM((tb, D_MODEL), BF16),
            pltpu.VMEM((2, tb, D_MODEL), BF16),
        ],
        compiler_params=pltpu.CompilerParams(
            dimension_semantics=("arbitrary",), vmem_limit_bytes=VMEM_LIMIT_BYTES),
        name="mix",
    )(x, s0, c0, *weights)


def _moe_body(te_ref, tlo_ref, tv_ref, glo_ref, ghi_ref, wslot_ref, enext_ref, pre_ref, cnt_ref, src_ref,
              xs_p_hbm, xs_s_hbm, zeros_hbm, wgu_hbm, bgu_ref, wd_hbm, bd_ref, y_ref,
              xbuf, wgu_f, wd_f, wgu_b, wd_b, gsem, wsem, *, tm, nt, nb, nb_p):
    i = pl.program_id(0)
    slot = lax.rem(i, 2)

    def issue_tile(t, s):
        e = te_ref[t]
        lo = tlo_ref[t]
        nv = tv_ref[t]
        hi = lo + nv
        dst0 = s * tm

        def runs(xs_hbm):
            def body(g, carry):
                p0 = pre_ref[e * nb + g]
                a = jnp.maximum(p0, lo)
                b = jnp.minimum(p0 + cnt_ref[e * nb + g], hi)
                _copy_rows(xs_hbm, src_ref[e * nb + g] + (a - p0), xbuf, dst0 + (a - lo), b - a, gsem.at[s])
                return carry
            return body
        g_lo, g_hi = glo_ref[t], ghi_ref[t]
        lax.fori_loop(g_lo, jnp.minimum(g_hi, nb_p), runs(xs_p_hbm), 0)
        lax.fori_loop(jnp.maximum(g_lo, nb_p), g_hi, runs(xs_s_hbm), 0)
        _copy_rows(zeros_hbm, 0, xbuf, dst0 + nv, tm - nv, gsem.at[s])

    def weight_fetch(e, ws):
        return (pltpu.make_async_copy(wgu_hbm.at[e], wgu_f.at[ws], wsem.at[ws]),
                pltpu.make_async_copy(wd_hbm.at[e], wd_f.at[ws], wsem.at[ws]))

    @pl.when(i == 0)
    def _prologue():
        issue_tile(0, 0)
        for c in weight_fetch(te_ref[0], wslot_ref[0]):
            c.start()

    @pl.when(i + 1 < nt)
    def _prefetch():
        issue_tile(i + 1, 1 - slot)

    used = tv_ref[i] > 0
    new_expert = (i == 0) | (te_ref[i] != te_ref[jnp.maximum(i - 1, 0)])

    @pl.when(used & new_expert)
    def _switch_expert():
        ws = wslot_ref[i]
        for c in weight_fetch(0, ws):
            c.wait()
        nxt = enext_ref[i]

        @pl.when(nxt >= 0)
        def _():
            for c in weight_fetch(nxt, 1 - ws):
                c.start(priority=1)
        rows = 128

        def cast_gu(c, carry):
            r0 = pl.multiple_of(c * rows, rows)
            wgu_b[pl.ds(r0, rows), :] = wgu_f[ws, pl.ds(r0, rows), :].astype(BF16)
            return carry
        lax.fori_loop(0, D_MODEL // rows, cast_gu, 0)

        def cast_d(c, carry):
            r0 = pl.multiple_of(c * rows, rows)
            wd_b[pl.ds(r0, rows), :] = wd_f[ws, pl.ds(r0, rows), :].astype(BF16)
            return carry
        lax.fori_loop(0, D_FF // rows, cast_d, 0)

    tile = pl.ds(pl.multiple_of(slot * tm * SUBLANES, tm * SUBLANES), tm * SUBLANES)
    pltpu.make_async_copy(zeros_hbm, xbuf.at[tile], gsem.at[slot]).wait()

    def expert_ffn(rows):
        xb = _from_row_tiles(xbuf, rows, base=slot * tm)
        gu = _dot(xb, wgu_b[...]) + bgu_ref[...]
        gate = jnp.minimum(gu[:, :D_FF], SWIGLU_LIMIT)
        up = jnp.clip(gu[:, D_FF:], -SWIGLU_LIMIT, SWIGLU_LIMIT)
        hid = (up + 1.0) * (gate * jax.nn.sigmoid(SWIGLU_ALPHA * gate))
        _to_row_tiles(y_ref, _dot(hid.astype(BF16), wd_b[...]) + bd_ref[...])

    nv_i = tv_ref[i]
    sizes = tuple(range(MOE_ROWS, tm + 1, MOE_ROWS))
    for lo, hi in zip((0,) + sizes[:-1], sizes):
        @pl.when((nv_i > lo) & (nv_i <= hi))
        def _(hi=hi):
            expert_ffn(hi)
            if hi < tm:
                y_ref[hi * SUBLANES:tm * SUBLANES, :] = jnp.zeros(((tm - hi) * SUBLANES, LANES), F32)

    @pl.when(jnp.logical_not(used))
    def _unused_tile():
        y_ref[...] = jnp.zeros((tm * SUBLANES, LANES), F32)


def _moe_stage(xs_p, xs_s, zeros, tabs, w_gu, b_gu, w_d, b_d, *, tm, nt, nb, nb_p):
    n_pref = len(tabs)

    def wmap(i, te, *_):
        return (te[i], 0, 0)

    grid_spec = pltpu.PrefetchScalarGridSpec(
        num_scalar_prefetch=n_pref,
        grid=(nt,),
        in_specs=[
            pl.BlockSpec(memory_space=pl.ANY),
            pl.BlockSpec(memory_space=pl.ANY),
            pl.BlockSpec(memory_space=pl.ANY),
            pl.BlockSpec(memory_space=pl.ANY),
            pl.BlockSpec((None, 1, 2 * D_FF), wmap),
            pl.BlockSpec(memory_space=pl.ANY),
            pl.BlockSpec((None, 1, D_MODEL), wmap),
        ],
        out_specs=pl.BlockSpec((tm * SUBLANES, LANES), lambda i, *_: (i, 0)),
        scratch_shapes=[
            pltpu.VMEM((2 * tm * SUBLANES, LANES), F32),
            pltpu.VMEM((2, D_MODEL, 2 * D_FF), F32),
            pltpu.VMEM((2, D_FF, D_MODEL), F32),
            pltpu.VMEM((D_MODEL, 2 * D_FF), BF16),
            pltpu.VMEM((D_FF, D_MODEL), BF16),
            pltpu.SemaphoreType.DMA((2,)),
            pltpu.SemaphoreType.DMA((2,)),
        ],
    )
    return pl.pallas_call(
        functools.partial(_moe_body, tm=tm, nt=nt, nb=nb, nb_p=nb_p),
        grid_spec=grid_spec,
        out_shape=jax.ShapeDtypeStruct((nt * tm * SUBLANES, LANES), F32),
        compiler_params=pltpu.CompilerParams(
            dimension_semantics=("arbitrary",), vmem_limit_bytes=VMEM_LIMIT_BYTES),
        name="moe",
    )(*tabs, xs_p, xs_s, zeros, w_gu, b_gu, w_d, b_d)


def _combine_body(pre_ref, cnt_ref, lst_ref, ys0_ref, y_hbm, x1_ref, slot_ref, g_ref, pe_ref, np_ref, wpg_ref,
                  wple_ref, nfin_ref, out_ref, stag, sem, *, tc, nsteps, nb, g0, blk_per_step, rows_per_blk):
    i = pl.program_id(0)
    slot = lax.rem(i, 2)
    rows = TOP_K * tc

    def issue_step(step, s):
        for bi in range(blk_per_step):
            g = g0 + step * blk_per_step + bi
            dst0 = s * rows + bi * rows_per_blk

            def body(e, carry):
                src = ys0_ref[e] + pre_ref[e * nb + g]
                _copy_rows(y_hbm, src, stag, dst0 + lst_ref[g * N_EXPERTS + e], cnt_ref[e * nb + g], sem.at[s])
                return carry
            lax.fori_loop(0, N_EXPERTS, body, 0)

    @pl.when(i == 0)
    def _prologue():
        issue_step(0, 0)

    @pl.when(i + 1 < nsteps)
    def _prefetch():
        issue_step(i + 1, 1 - slot)

    tile = pl.ds(pl.multiple_of(slot * rows * SUBLANES, rows * SUBLANES), rows * SUBLANES)
    pltpu.make_async_copy(y_hbm.at[pl.ds(0, rows * SUBLANES)], stag.at[tile], sem.at[slot]).wait()

    tb = tc // blk_per_step
    srow = lax.broadcasted_iota(I32, (rows_per_blk, tb), 0)
    moe_out = []
    for bi in range(blk_per_step):
        sl = slot_ref[bi]
        g = g_ref[bi]
        back_t = jnp.zeros((rows_per_blk, tb), F32)
        for k in range(TOP_K):
            back_t = jnp.where(srow == sl[k:k + 1, :], g[k:k + 1, :], back_t)
        staged = _from_row_tiles(stag, rows_per_blk, base=slot * rows + bi * rows_per_blk)
        moe_out.append(_dot_tn(back_t.astype(BF16), staged))
    x2 = x1_ref[...] + (moe_out[0] if blk_per_step == 1 else jnp.concatenate(moe_out, axis=0))
    hp = _rms(x2, np_ref[...]).astype(BF16)
    gate = jax.nn.sigmoid(_dot(hp, wpg_ref[...]))
    emb = _dot(pe_ref[...].astype(BF16), wple_ref[...])
    x3 = x2 + gate * emb
    out_ref[...] = _rms(x3, nfin_ref[...])


def _combine_stage(x1, ys, slots, gates, pe, tabs, w, *, tc, nb, g0, blk_per_step, rows_per_blk):
    n = pe.shape[0]
    nsteps = n // tc

    def full(a):
        return pl.BlockSpec(a.shape, lambda i, *_: (0,) * a.ndim)

    weights = [w["norm_ple"], w["w_ple_gate"], w["w_ple"], w["norm_final"]]
    grid_spec = pltpu.PrefetchScalarGridSpec(
        num_scalar_prefetch=len(tabs),
        grid=(nsteps,),
        in_specs=[
            pl.BlockSpec(memory_space=pl.ANY),
            pl.BlockSpec((tc, D_MODEL), lambda i, *_: (i, 0)),
            pl.BlockSpec((blk_per_step, TOP_K, tc // blk_per_step), lambda i, *_: (i, 0, 0)),
            pl.BlockSpec((blk_per_step, TOP_K, tc // blk_per_step), lambda i, *_: (i, 0, 0)),
            pl.BlockSpec((tc, PLE_DIM), lambda i, *_: (i, 0)),
        ] + [full(a) for a in weights],
        out_specs=pl.BlockSpec((tc, D_MODEL), lambda i, *_: (i, 0)),
        scratch_shapes=[
            pltpu.VMEM((2 * TOP_K * tc * SUBLANES, LANES), F32),
            pltpu.SemaphoreType.DMA((2,)),
        ],
    )
    return pl.pallas_call(
        functools.partial(_combine_body, tc=tc, nsteps=nsteps, nb=nb, g0=g0, blk_per_step=blk_per_step,
                          rows_per_blk=rows_per_blk),
        grid_spec=grid_spec,
        out_shape=jax.ShapeDtypeStruct((n, D_MODEL), F32),
        compiler_params=pltpu.CompilerParams(
            dimension_semantics=("arbitrary",), vmem_limit_bytes=VMEM_LIMIT_BYTES),
        name="combine",
    )(*tabs, ys, x1, slots, gates, pe, *weights)


def _dispatch_tables(cnt, blk_row0, tm, nt):
    nb = cnt.shape[0]
    experts = jnp.arange(N_EXPERTS, dtype=I32)
    pre = jnp.cumsum(cnt, axis=0) - cnt
    total = jnp.sum(cnt, axis=0)
    lst = jnp.cumsum(cnt, axis=1) - cnt
    src = blk_row0[:, None] + lst
    n_tiles_e = (total + tm - 1) // tm
    tile_end = jnp.cumsum(n_tiles_e)
    tile_start = tile_end - n_tiles_e
    tiles = jnp.arange(nt, dtype=I32)
    tile_e_raw = jnp.sum((tiles[:, None] >= tile_end[None, :]).astype(I32), axis=1)
    used = tile_e_raw < N_EXPERTS
    has = total > 0
    last_e = jnp.max(jnp.where(has, experts, 0))
    tile_e = jnp.where(used, jnp.minimum(tile_e_raw, N_EXPERTS - 1), last_e).astype(I32)
    hot = tile_e[:, None] == experts[None, :]

    def per_tile(v):
        if v.ndim == 1:
            return jnp.sum(jnp.where(hot, v[None, :], 0), axis=1)
        return jnp.sum(jnp.where(hot[:, :, None], v[None, :, :], 0), axis=1)

    tile_lo = jnp.where(used, (tiles - per_tile(tile_start)) * tm, 0).astype(I32)
    tile_valid = jnp.where(used, jnp.clip(per_tile(total) - tile_lo, 0, tm), 0).astype(I32)
    p_t = per_tile(pre.T)
    c_t = per_tile(cnt.T)
    meets = (p_t < (tile_lo + tile_valid)[:, None]) & ((p_t + c_t) > tile_lo[:, None]) & (c_t > 0)
    gidx = jnp.arange(nb, dtype=I32)[None, :]
    g_lo = jnp.min(jnp.where(meets, gidx, nb), axis=1).astype(I32)
    g_hi = jnp.max(jnp.where(meets, gidx + 1, 0), axis=1).astype(I32)
    g_lo = jnp.minimum(g_lo, g_hi)
    w_slot_e = (jnp.cumsum(has.astype(I32)) - 1) % 2
    later = has[None, :] & (experts[None, :] > experts[:, None])
    e_next = jnp.min(jnp.where(later, experts[None, :], N_EXPERTS), axis=1)
    e_next = jnp.where(e_next >= N_EXPERTS, -1, e_next)
    flat = lambda a: a.T.reshape(-1).astype(I32)
    return dict(tile_e=tile_e, tile_lo=tile_lo, tile_valid=tile_valid, g_lo=g_lo, g_hi=g_hi,
                w_slot=per_tile(w_slot_e).astype(I32), e_next=per_tile(e_next).astype(I32),
                pre=flat(pre), cnt=flat(cnt), src=flat(src), lst=lst.reshape(-1).astype(I32),
                ys0=(tile_start * tm).astype(I32))


def kernel(x_prompt, x_sample, state_hgrn, cache_conv, p_prompt, p_sample, norm_mix, w_in, hgrn_lb_logits, hgrn_norm, conv_w, conv_b, conv_ln_g, conv_ln_b, w_out, norm_ffn, w_router, b_router, w_gate_up, b_gate_up, w_down, b_down, w_ple, norm_ple, w_ple_gate, norm_final):
    depth = norm_mix.shape[0]
    assert depth == 1
    bp, tp, _ = x_prompt.shape
    bs, ts, _ = x_sample.shape
    n_p, n_s = bp * tp, bs * ts
    n_tok = n_p + n_s

    lb_all = jnp.cumsum(jax.nn.softmax(hgrn_lb_logits.astype(F32), axis=0), axis=0)
    wr_t = w_router[0].T
    wr_hi = wr_t.astype(BF16)
    w = {
        "norm_mix": norm_mix[0][None, :], "w_in": w_in[0].astype(BF16), "lb": lb_all[0][None, :],
        "hgrn_norm": hgrn_norm[0][None, :], "conv_w": conv_w[0], "conv_b": conv_b[0][None, :],
        "ln_g": conv_ln_g[0][None, :], "ln_b": conv_ln_b[0][None, :], "w_out": w_out[0].astype(BF16),
        "norm_ffn": norm_ffn[0][None, :], "wr_hi": wr_hi, "wr_lo": (wr_t - wr_hi.astype(F32)).astype(BF16),
        "b_router": b_router[0][:, None],
        "norm_ple": norm_ple[0][None, :], "w_ple_gate": w_ple_gate[0].astype(BF16),
        "w_ple": w_ple[0].astype(BF16), "norm_final": norm_final[None, :],
    }

    pad = ((0, 0), (CACHE_PAD - CACHE_ROWS, 0), (0, 0))
    s0_p = jnp.zeros((bp, N_HEADS, HEAD_DIM, HEAD_DIM), F32)
    c0_p = jnp.zeros((bp, CACHE_PAD, CONV_CH), F32)
    c0_s = jnp.pad(cache_conv[0], pad)

    tb_p = 256
    x1_p, xs_p, slot_p, gate_p, cnt_p, st_p, cc_p = _mix_stage(x_prompt, s0_p, c0_p, w, nseq=1, tseq=tb_p, chunk=128)
    x1_s, xs_s, slot_s, gate_s, cnt_s, st_s, cc_s = _mix_stage(x_sample, state_hgrn[0], c0_s, w, nseq=bs, tseq=ts,
                                                             chunk=ts)

    nb_p, nb_s = n_p // tb_p, 1
    nb = nb_p + nb_s
    cnt = jnp.concatenate([cnt_p[:, :, 0], cnt_s[:, :, 0]], axis=0)
    blk_row0 = jnp.concatenate([jnp.arange(nb_p, dtype=I32) * (TOP_K * tb_p), jnp.zeros((nb_s,), I32)])
    tm = 3 * MOE_ROWS
    nt = -(-(TOP_K * n_tok) // tm) + N_EXPERTS
    t = _dispatch_tables(cnt, blk_row0, tm, nt)

    zeros = jnp.zeros((tm * SUBLANES, LANES), F32)
    moe_tabs = (t["tile_e"], t["tile_lo"], t["tile_valid"], t["g_lo"], t["g_hi"], t["w_slot"], t["e_next"],
                t["pre"], t["cnt"], t["src"])
    ys = _moe_stage(xs_p, xs_s, zeros, moe_tabs, w_gate_up[0], b_gate_up[0][:, None, :], w_down[0],
                    b_down[0][:, None, :], tm=tm, nt=nt, nb=nb, nb_p=nb_p)

    comb_tabs = (t["pre"], t["cnt"], t["lst"], t["ys0"])
    y_p = _combine_stage(x1_p, ys, slot_p, gate_p, p_prompt[0].reshape(n_p, PLE_DIM),
                         comb_tabs, w, tc=2 * tb_p, nb=nb, g0=0, blk_per_step=2, rows_per_blk=TOP_K * tb_p)
    y_s = _combine_stage(x1_s, ys, slot_s, gate_s, p_sample[0].reshape(n_s, PLE_DIM),
                         comb_tabs, w, tc=n_s, nb=nb, g0=nb_p, blk_per_step=1, rows_per_blk=TOP_K * n_s)

    return (y_p.reshape(bp, tp, D_MODEL), y_s.reshape(bs, ts, D_MODEL),
            st_p[None], cc_p[None], st_s[None], cc_s[None])
```

```python
import functools

import jax
import jax.numpy as jnp
from jax import lax
from jax.experimental import pallas as pl
from jax.experimental.pallas import tpu as pltpu

D_MODEL = 1024
HGRN_WIDTH = 512
HEAD_DIM = 128
N_HEADS = HGRN_WIDTH // HEAD_DIM
CONV_CH = 512
CONV_WIDTH = 31
CACHE_ROWS = CONV_WIDTH - 1
CACHE_PAD = 32
IN_COLS = 4 * HGRN_WIDTH + 2 * CONV_CH
N_EXPERTS = 32
TOP_K = 4
D_FF = 1024
SWIGLU_LIMIT = 7.0
SWIGLU_ALPHA = 1.702
PLE_DIM = 256
EPS = 1e-6

SUBLANES = 8
LANES = 128
MOE_ROWS = 256
VMEM_LIMIT_BYTES = 56 * 1024 * 1024

F32 = jnp.float32
BF16 = jnp.bfloat16
I32 = jnp.int32


def _split3(a):
    p1 = a.astype(BF16)
    r1 = a - p1.astype(F32)
    p2 = r1.astype(BF16)
    r2 = r1 - p2.astype(F32)
    return p1, p2, r2.astype(BF16)


def _dot(a, b):
    return jnp.dot(a, b, preferred_element_type=F32)


def _dot_nt(a, b):
    return lax.dot_general(a, b, (((1,), (1,)), ((), ())), preferred_element_type=F32)


def _dot_tn(a, b):
    return lax.dot_general(a, b, (((0,), (0,)), ((), ())), preferred_element_type=F32)


def _rms(x, g):
    return x * lax.rsqrt(jnp.mean(x * x, axis=-1, keepdims=True) + EPS) * g


def _copy_rows(src_hbm, src_row, dst_vmem, dst_row, n, sem):
    @pl.when(n > 0)
    def _():
        size = pl.multiple_of(n * SUBLANES, SUBLANES)
        pltpu.make_async_copy(src_hbm.at[pl.ds(pl.multiple_of(src_row * SUBLANES, SUBLANES), size)],
                              dst_vmem.at[pl.ds(pl.multiple_of(dst_row * SUBLANES, SUBLANES), size)], sem).start()


def _to_row_tiles(ref, x, base=0):
    rows = x.shape[0]
    for s in range(D_MODEL // LANES):
        ref[pl.ds(base * SUBLANES + s, rows, stride=SUBLANES), :] = x[:, s * LANES:(s + 1) * LANES]


def _from_row_tiles(ref, rows, base=0, dtype=BF16):
    return jnp.concatenate(
        [ref[pl.ds(base * SUBLANES + s, rows, stride=SUBLANES), :].astype(dtype) for s in range(D_MODEL // LANES)],
        axis=1)


def _hgrn_chunk(q, kk, f, b, v, st, shift_ref, ones_sq, pair_ref):
    c = q.shape[0]
    sc = None
    n_mask = 0
    m = c // 2
    while m >= SUBLANES:
        pieces = []
        for blk in range(c // (2 * m)):
            r = blk * 2 * m + m - 1
            pieces.append(jnp.broadcast_to(b[r:r + 1, :], (2 * m, HEAD_DIM)))
        bmid = pieces[0] if len(pieces) == 1 else jnp.concatenate(pieces, axis=0)
        e_m = jnp.exp(-jnp.abs(b - bmid))
        qh = (q * e_m).astype(BF16)
        kh = (kk * e_m).astype(BF16)
        term = pair_ref[n_mask] * _dot_nt(qh, kh)
        sc = term if sc is None else sc + term
        n_mask += 1
        m //= 2

    shift_ref[0, SUBLANES:SUBLANES + c, :] = f
    shift_ref[1, SUBLANES:SUBLANES + c, :] = kk
    decay = None
    for d in range(SUBLANES):
        if d == 0:
            kd = kk
        else:
            f_sh = shift_ref[0, SUBLANES - (d - 1):SUBLANES - (d - 1) + c, :]
            decay = f_sh if decay is None else decay * f_sh
            kd = shift_ref[1, SUBLANES - d:SUBLANES - d + c, :] * decay
        s_d = _dot((q * kd).astype(BF16), ones_sq)
        term = pair_ref[n_mask + d] * s_d[:, :c]
        sc = term if sc is None else sc + term

    o = _dot(sc.astype(BF16), v.astype(BF16))
    o = o + _dot_nt((q * jnp.exp(b)).astype(BF16), st.astype(BF16))
    b_last = b[c - 1:c, :]
    kdec = (kk * jnp.exp(b_last - b)).astype(BF16)
    st_new = st * jnp.exp(b_last) + _dot_tn(v.astype(BF16), kdec)
    return o, st_new


def _mix_body(x_ref, s0_ref, c0_ref, nm_ref, win_ref, lb_ref, hn_ref, cw_ref, cb_ref, lng_ref,
              lnb_ref, wout_ref, nf_ref, wrh_ref, wrl_ref, br_ref, tri_ref, before_ref, lower_ref, ones_ref, pair_ref,
              x1_ref, xs_ref, slot_ref, gate_ref, cnt_ref, sout_ref, cout_ref,
              st_scr, upad_scr, ush_scr, shift_scr, mix_scr, carry_scr, *, tb, chunk, nj, nblk, nseq):
    pipelined = nblk > 1
    tseq = tb // nseq
    s = pl.program_id(0)
    j = lax.rem(jnp.minimum(s, nblk - 1), nj)

    if pipelined:
        @pl.when(s == 0)
        def _():
            carry_scr[...] = jnp.zeros((2, tb, D_MODEL), BF16)

    @pl.when(j == 0)
    def _init():
        for q in range(nseq):
            for h in range(N_HEADS):
                st_scr[q * N_HEADS + h] = s0_ref[q, h].T
            upad_scr[q, 0:CACHE_PAD, :] = c0_ref[q]
        shift_scr[:, 0:SUBLANES, :] = jnp.zeros((2, SUBLANES, HEAD_DIM), F32)

    def router(xh, xl):
        return _route_block(xh, xl, wrh_ref, wrl_ref, br_ref, before_ref, lower_ref,
                            xs_ref, slot_ref, gate_ref, cnt_ref, tb=tb)

    route = router(carry_scr[0], carry_scr[1]) if pipelined else iter(())

    def advance_route(n=1):
        for _ in range(n):
            next(route, None)

    advance_route()
    x = x_ref[...].reshape(tb, D_MODEL)
    h_in = _rms(x, nm_ref[...]).astype(BF16)
    proj = _dot(h_in, win_ref[...])
    advance_route(2)

    q_all = proj[:, 0:HGRN_WIDTH]
    fz = proj[:, HGRN_WIDTH:2 * HGRN_WIDTH]
    v_all = proj[:, 2 * HGRN_WIDTH:3 * HGRN_WIDTH]
    g_all = proj[:, 3 * HGRN_WIDTH:4 * HGRN_WIDTH]
    lb = lb_ref[...]
    f_all = lb + (1.0 - lb) * jax.nn.sigmoid(fz)
    lf_all = jnp.log(f_all)
    kk_all = 1.0 - f_all

    tri = tri_ref[...]
    p1, p2, p3 = _split3(lf_all)
    b_all = _dot(tri, p1) + _dot(tri, p2) + _dot(tri, p3)
    advance_route(2)

    ones_sq = ones_ref[...]
    hn = hn_ref[...]
    per_seq = tseq // chunk
    for h in range(N_HEADS):
        ls = slice(h * HEAD_DIM, (h + 1) * HEAD_DIM)
        for ci in range(tb // chunk):
            q, cq = divmod(ci, per_seq)
            if cq == 0:
                st = st_scr[q * N_HEADS + h]
            rs = slice(ci * chunk, (ci + 1) * chunk)
            o, st = _hgrn_chunk(q_all[rs, ls], kk_all[rs, ls], f_all[rs, ls], b_all[rs, ls],
                                v_all[rs, ls], st, shift_scr, ones_sq, pair_ref)
            o = _rms(o, hn[:, ls])
            mix_scr[rs, ls] = (o * jax.nn.silu(g_all[rs, ls])).astype(BF16)
            if cq == per_seq - 1:
                st_scr[q * N_HEADS + h] = st
            advance_route()

    u = proj[:, 4 * HGRN_WIDTH:4 * HGRN_WIDTH + CONV_CH] * jax.nn.sigmoid(
        proj[:, 4 * HGRN_WIDTH + CONV_CH:IN_COLS])
    first = CACHE_PAD - CACHE_ROWS
    span = tseq + CACHE_PAD - SUBLANES
    accs = []
    for q in range(nseq):
        upad = upad_scr.at[q]
        upad[CACHE_PAD:CACHE_PAD + tseq, :] = u[q * tseq:(q + 1) * tseq, :]
        for r in range(1, SUBLANES):
            ush_scr[r, 0:span, :] = upad[r:r + span, :]
        acc = jnp.zeros((tseq, CONV_CH), F32) + cb_ref[...]
        for tap in range(CONV_WIDTH):
            a, r = divmod(first + tap, SUBLANES)
            rows = slice(a * SUBLANES, a * SUBLANES + tseq)
            shifted = upad[rows, :] if r == 0 else ush_scr[r, rows, :]
            acc = acc + cw_ref[tap:tap + 1, :] * shifted
        accs.append(acc)
        new_cache = upad[tseq:tseq + CACHE_PAD, :]
        upad[0:CACHE_PAD, :] = new_cache
    acc = accs[0] if nseq == 1 else jnp.concatenate(accs, axis=0)
    mu = jnp.mean(acc, axis=-1, keepdims=True)
    cen = acc - mu
    var = jnp.mean(cen * cen, axis=-1, keepdims=True)
    cn = cen * lax.rsqrt(var + EPS) * lng_ref[...] + lnb_ref[...]
    mix_scr[:, HGRN_WIDTH:HGRN_WIDTH + CONV_CH] = jax.nn.silu(cn).astype(BF16)
    for _ in route:
        pass

    x1 = x + _dot(mix_scr[...], wout_ref[...])
    x1_ref[...] = x1
    xn = _rms(x1, nf_ref[...])
    xh = xn.astype(BF16)
    xl = (xn - xh.astype(F32)).astype(BF16)
    if pipelined:
        carry_scr[0] = xh
        carry_scr[1] = xl
    else:
        for _ in router(xh, xl):
            pass

    @pl.when((s < nblk) & (j == nj - 1))
    def _fin():
        for q in range(nseq):
            for h in range(N_HEADS):
                sout_ref[q, h] = st_scr[q * N_HEADS + h].T
            cout_ref[q] = upad_scr[q, tseq + first:tseq + CACHE_PAD, :]


def _route_block(xh, xl, wrh_ref, wrl_ref, br_ref, before_ref, lower_ref, xs_ref, slot_ref, gate_ref, cnt_ref, *, tb):
    wrh = wrh_ref[...]
    logits = _dot_nt(wrh, xh) + _dot_nt(wrh, xl) + _dot_nt(wrl_ref[...], xh) + br_ref[...]
    yield

    eidx = lax.broadcasted_iota(I32, (N_EXPERTS, tb), 0).astype(F32)
    vals, hots = [], []
    work = logits
    for _ in range(TOP_K):
        mx = jnp.max(work, axis=0, keepdims=True)
        pick = jnp.min(jnp.where(work == mx, eidx, float(N_EXPERTS)), axis=0, keepdims=True)
        hot = eidx == pick
        vals.append(mx)
        hots.append(hot)
        work = jnp.where(hot, -jnp.inf, work)
        yield
    exps = [jnp.exp(val - vals[0]) for val in vals]
    inv = 1.0 / (exps[0] + exps[1] + exps[2] + exps[3])
    gate_ref[0] = jnp.concatenate([e * inv for e in exps], axis=0)

    before = before_ref[...]
    lower = lower_ref[...]
    hot_f = [hot.astype(F32) for hot in hots]
    tots = [jnp.sum(hf, axis=1, keepdims=True) for hf in hot_f]
    cnt = tots[0] + tots[1] + tots[2] + tots[3]
    cnt_ref[0] = cnt.astype(I32)
    start = _dot(lower, jnp.broadcast_to(cnt, (N_EXPERTS, LANES)).astype(BF16))[:, 0:1]
    yield
    slots = []
    base = start
    for k in range(TOP_K):
        pre = _dot(hot_f[k].astype(BF16), before)
        slots.append(jnp.sum(hot_f[k] * (base + pre), axis=0, keepdims=True))
        base = base + tots[k]
        yield
    slot_ref[0] = jnp.concatenate(slots, axis=0).astype(I32)

    srow = lax.broadcasted_iota(I32, (TOP_K * tb, tb), 0).astype(F32)
    perm = (srow == slots[0])
    for k in range(1, TOP_K):
        perm = perm | (srow == slots[k])
    yield
    xs = _dot(perm.astype(BF16), xh)
    yield
    _to_row_tiles(xs_ref, xs)


def _mix_stage(x, s0, c0, w, *, nseq, tseq, chunk):
    bsz, t, _ = x.shape
    tb = nseq * tseq
    nj = t // tseq
    nblk = (bsz // nseq) * nj
    pipelined = nblk > 1
    n_steps = nblk + 1 if pipelined else 1

    def full(a):
        return pl.BlockSpec(a.shape, lambda s: (0,) * a.ndim)

    def cur(s):
        sc = jnp.minimum(s, nblk - 1)
        return sc // nj, sc % nj

    def prev(s):
        return jnp.maximum(s - 1, 0) if pipelined else s

    weights = [w["norm_mix"], w["w_in"], w["lb"], w["hgrn_norm"], w["conv_w"], w["conv_b"], w["ln_g"],
               w["ln_b"], w["w_out"], w["norm_ffn"], w["wr_hi"], w["wr_lo"], w["b_router"]]
    r_t = lax.broadcasted_iota(I32, (tb, tb), 0)
    c_t = lax.broadcasted_iota(I32, (tb, tb), 1)
    r_e = lax.broadcasted_iota(I32, (N_EXPERTS, N_EXPERTS), 0)
    c_e = lax.broadcasted_iota(I32, (N_EXPERTS, N_EXPERTS), 1)
    weights += [((r_t >= c_t) & ((r_t ^ c_t) < chunk)).astype(BF16), (r_t < c_t).astype(BF16),
                (c_e < r_e).astype(BF16), jnp.ones((HEAD_DIM, HEAD_DIM), BF16)]
    r_c = lax.broadcasted_iota(I32, (chunk, chunk), 0)
    c_c = lax.broadcasted_iota(I32, (chunk, chunk), 1)
    x_c = r_c ^ c_c
    pair = []
    m = chunk // 2
    while m >= SUBLANES:
        pair.append((x_c >= m) & (x_c < 2 * m) & (r_c > c_c))
        m //= 2
    pair += [(x_c < SUBLANES) & (r_c - c_c == d) for d in range(SUBLANES)]
    weights.append(jnp.stack(pair).astype(F32))
    in_specs = [
        pl.BlockSpec((nseq, tseq, D_MODEL), lambda s: (*cur(s), 0)),
        pl.BlockSpec((nseq, N_HEADS, HEAD_DIM, HEAD_DIM), lambda s: (cur(s)[0], 0, 0, 0)),
        pl.BlockSpec((nseq, CACHE_PAD, CONV_CH), lambda s: (cur(s)[0], 0, 0)),
    ] + [full(a) for a in weights]
    out_shape = (
        jax.ShapeDtypeStruct((n_steps * tb, D_MODEL), F32),
        jax.ShapeDtypeStruct((TOP_K * nblk * tb * SUBLANES, LANES), F32),
        jax.ShapeDtypeStruct((nblk, TOP_K, tb), I32),
        jax.ShapeDtypeStruct((nblk, TOP_K, tb), F32),
        jax.ShapeDtypeStruct((nblk, N_EXPERTS, 1), I32),
        jax.ShapeDtypeStruct((bsz, N_HEADS, HEAD_DIM, HEAD_DIM), F32),
        jax.ShapeDtypeStruct((bsz, CACHE_ROWS, CONV_CH), F32),
    )
    out_specs = (
        pl.BlockSpec((tb, D_MODEL), lambda s: (s, 0)),
        pl.BlockSpec((TOP_K * tb * SUBLANES, LANES), lambda s: (prev(s), 0)),
        pl.BlockSpec((1, TOP_K, tb), lambda s: (prev(s), 0, 0)),
        pl.BlockSpec((1, TOP_K, tb), lambda s: (prev(s), 0, 0)),
        pl.BlockSpec((1, N_EXPERTS, 1), lambda s: (prev(s), 0, 0)),
        pl.BlockSpec((nseq, N_HEADS, HEAD_DIM, HEAD_DIM), lambda s: (cur(s)[0], 0, 0, 0)),
        pl.BlockSpec((nseq, CACHE_ROWS, CONV_CH), lambda s: (cur(s)[0], 0, 0)),
    )
    return pl.pallas_call(
        functools.partial(_mix_body, tb=tb, chunk=chunk, nj=nj, nblk=nblk, nseq=nseq),
        grid=(n_steps,),
        in_specs=in_specs,
        out_specs=out_specs,
        out_shape=out_shape,
        scratch_shapes=[
            pltpu.VMEM((nseq * N_HEADS, HEAD_DIM, HEAD_DIM), F32),
            pltpu.VMEM((nseq, CACHE_PAD + tseq, CONV_CH), F32),
            pltpu.VMEM((SUBLANES, CACHE_PAD + tseq - SUBLANES, CONV_CH), F32),
            pltpu.VMEM((2, SUBLANES + chunk, HEAD_DIM), F32),
            pltpu.VMEM((tb, D_MODEL), BF16),
            pltpu.VMEM((2, tb, D_MODEL), BF16),
        ],
        compiler_params=pltpu.CompilerParams(
            dimension_semantics=("arbitrary",), vmem_limit_bytes=VMEM_LIMIT_BYTES),
        name="mix",
    )(x, s0, c0, *weights)


def _moe_body(te_ref, tlo_ref, tv_ref, glo_ref, ghi_ref, wslot_ref, enext_ref, pre_ref, cnt_ref, src_ref,
              xs_p_hbm, xs_s_hbm, zeros_hbm, wgu_hbm, bgu_ref, wd_hbm, bd_ref, y_ref,
              xbuf, wgu_f, wd_f, wgu_b, wd_b, gsem, wsem, *, tm, nt, nb, nb_p):
    i = pl.program_id(0)
    slot = lax.rem(i, 2)

    def issue_tile(t, s):
        e = te_ref[t]
        lo = tlo_ref[t]
        nv = tv_ref[t]
        hi = lo + nv
        dst0 = s * tm

        def runs(xs_hbm):
            def body(g, carry):
                p0 = pre_ref[e * nb + g]
                a = jnp.maximum(p0, lo)
                b = jnp.minimum(p0 + cnt_ref[e * nb + g], hi)
                _copy_rows(xs_hbm, src_ref[e * nb + g] + (a - p0), xbuf, dst0 + (a - lo), b - a, gsem.at[s])
                return carry
            return body
        g_lo, g_hi = glo_ref[t], ghi_ref[t]
        lax.fori_loop(g_lo, jnp.minimum(g_hi, nb_p), runs(xs_p_hbm), 0)
        lax.fori_loop(jnp.maximum(g_lo, nb_p), g_hi, runs(xs_s_hbm), 0)
        _copy_rows(zeros_hbm, 0, xbuf, dst0 + nv, (-nv) & (MOE_ROWS - 1), gsem.at[s])

    def weight_fetch(e, ws):
        return (pltpu.make_async_copy(wgu_hbm.at[e], wgu_f.at[ws], wsem.at[ws]),
                pltpu.make_async_copy(wd_hbm.at[e], wd_f.at[ws], wsem.at[ws]))

    @pl.when(i == 0)
    def _prologue():
        issue_tile(0, 0)
        for c in weight_fetch(te_ref[0], wslot_ref[0]):
            c.start()

    @pl.when(i + 1 < nt)
    def _prefetch():
        issue_tile(i + 1, 1 - slot)

    used = tv_ref[i] > 0
    new_expert = (i == 0) | (te_ref[i] != te_ref[jnp.maximum(i - 1, 0)])

    @pl.when(used & new_expert)
    def _switch_expert():
        ws = wslot_ref[i]
        for c in weight_fetch(0, ws):
            c.wait()
        nxt = enext_ref[i]

        @pl.when(nxt >= 0)
        def _():
            for c in weight_fetch(nxt, 1 - ws):
                c.start(priority=1)
        rows = 128

        def cast_gu(c, carry):
            r0 = pl.multiple_of(c * rows, rows)
            wgu_b[pl.ds(r0, rows), :] = wgu_f[ws, pl.ds(r0, rows), :].astype(BF16)
            return carry
        lax.fori_loop(0, D_MODEL // rows, cast_gu, 0)

        def cast_d(c, carry):
            r0 = pl.multiple_of(c * rows, rows)
            wd_b[pl.ds(r0, rows), :] = wd_f[ws, pl.ds(r0, rows), :].astype(BF16)
            return carry
        lax.fori_loop(0, D_FF // rows, cast_d, 0)

    def expert_ffn(rows):
        got = pl.ds(pl.multiple_of(slot * tm * SUBLANES, tm * SUBLANES), rows * SUBLANES)
        pltpu.make_async_copy(xs_p_hbm.at[pl.ds(0, rows * SUBLANES)], xbuf.at[got], gsem.at[slot]).wait()
        xb = _from_row_tiles(xbuf, rows, base=slot * tm)
        gu = _dot(xb, wgu_b[...]) + bgu_ref[...]
        gate = jnp.minimum(gu[:, :D_FF], SWIGLU_LIMIT)
        up = jnp.clip(gu[:, D_FF:], -SWIGLU_LIMIT, SWIGLU_LIMIT)
        hid = (up + 1.0) * (gate * jax.nn.sigmoid(SWIGLU_ALPHA * gate))
        _to_row_tiles(y_ref, _dot(hid.astype(BF16), wd_b[...]) + bd_ref[...])

    nv_i = tv_ref[i]
    sizes = tuple(range(MOE_ROWS, tm + 1, MOE_ROWS))
    for lo, hi in zip((0,) + sizes[:-1], sizes):
        @pl.when((nv_i > lo) & (nv_i <= hi))
        def _(hi=hi):
            expert_ffn(hi)
            if hi < tm:
                y_ref[hi * SUBLANES:tm * SUBLANES, :] = jnp.zeros(((tm - hi) * SUBLANES, LANES), F32)

    @pl.when(jnp.logical_not(used))
    def _unused_tile():
        y_ref[...] = jnp.zeros((tm * SUBLANES, LANES), F32)


def _moe_stage(xs_p, xs_s, zeros, tabs, w_gu, b_gu, w_d, b_d, *, tm, nt, nb, nb_p):
    n_pref = len(tabs)

    def wmap(i, te, *_):
        return (te[i], 0, 0)

    grid_spec = pltpu.PrefetchScalarGridSpec(
        num_scalar_prefetch=n_pref,
        grid=(nt,),
        in_specs=[
            pl.BlockSpec(memory_space=pl.ANY),
            pl.BlockSpec(memory_space=pl.ANY),
            pl.BlockSpec(memory_space=pl.ANY),
            pl.BlockSpec(memory_space=pl.ANY),
            pl.BlockSpec((None, 1, 2 * D_FF), wmap),
            pl.BlockSpec(memory_space=pl.ANY),
            pl.BlockSpec((None, 1, D_MODEL), wmap),
        ],
        out_specs=pl.BlockSpec((tm * SUBLANES, LANES), lambda i, *_: (i, 0)),
        scratch_shapes=[
            pltpu.VMEM((2 * tm * SUBLANES, LANES), F32),
            pltpu.VMEM((2, D_MODEL, 2 * D_FF), F32),
            pltpu.VMEM((2, D_FF, D_MODEL), F32),
            pltpu.VMEM((D_MODEL, 2 * D_FF), BF16),
            pltpu.VMEM((D_FF, D_MODEL), BF16),
            pltpu.SemaphoreType.DMA((2,)),
            pltpu.SemaphoreType.DMA((2,)),
        ],
    )
    return pl.pallas_call(
        functools.partial(_moe_body, tm=tm, nt=nt, nb=nb, nb_p=nb_p),
        grid_spec=grid_spec,
        out_shape=jax.ShapeDtypeStruct((nt * tm * SUBLANES, LANES), F32),
        compiler_params=pltpu.CompilerParams(
            dimension_semantics=("arbitrary",), vmem_limit_bytes=VMEM_LIMIT_BYTES),
        name="moe",
    )(*tabs, xs_p, xs_s, zeros, w_gu, b_gu, w_d, b_d)


def _combine_body(pre_ref, cnt_ref, lst_ref, ys0_ref, y_hbm, x1_ref, slot_ref, g_ref, pe_ref, np_ref, wpg_ref,
                  wple_ref, nfin_ref, out_ref, stag, sem, *, tc, nsteps, nb, g0, blk_per_step, rows_per_blk):
    i = pl.program_id(0)
    slot = lax.rem(i, 2)
    rows = TOP_K * tc

    def issue_step(step, s):
        for bi in range(blk_per_step):
            g = g0 + step * blk_per_step + bi
            dst0 = s * rows + bi * rows_per_blk

            def body(e, carry):
                src = ys0_ref[e] + pre_ref[e * nb + g]
                _copy_rows(y_hbm, src, stag, dst0 + lst_ref[g * N_EXPERTS + e], cnt_ref[e * nb + g], sem.at[s])
                return carry
            lax.fori_loop(0, N_EXPERTS, body, 0)

    @pl.when(i == 0)
    def _prologue():
        issue_step(0, 0)

    @pl.when(i + 1 < nsteps)
    def _prefetch():
        issue_step(i + 1, 1 - slot)

    tile = pl.ds(pl.multiple_of(slot * rows * SUBLANES, rows * SUBLANES), rows * SUBLANES)
    pltpu.make_async_copy(y_hbm.at[pl.ds(0, rows * SUBLANES)], stag.at[tile], sem.at[slot]).wait()

    tb = tc // blk_per_step
    srow = lax.broadcasted_iota(I32, (rows_per_blk, tb), 0)
    moe_out = []
    for bi in range(blk_per_step):
        sl = slot_ref[bi]
        g = g_ref[bi]
        back_t = jnp.zeros((rows_per_blk, tb), F32)
        for k in range(TOP_K):
            back_t = jnp.where(srow == sl[k:k + 1, :], g[k:k + 1, :], back_t)
        staged = _from_row_tiles(stag, rows_per_blk, base=slot * rows + bi * rows_per_blk)
        moe_out.append(_dot_tn(back_t.astype(BF16), staged))
    x2 = x1_ref[...] + (moe_out[0] if blk_per_step == 1 else jnp.concatenate(moe_out, axis=0))
    hp = _rms(x2, np_ref[...]).astype(BF16)
    gate = jax.nn.sigmoid(_dot(hp, wpg_ref[...]))
    emb = _dot(pe_ref[...].astype(BF16), wple_ref[...])
    x3 = x2 + gate * emb
    out_ref[...] = _rms(x3, nfin_ref[...])


def _combine_stage(x1, ys, slots, gates, pe, tabs, w, *, tc, nb, g0, blk_per_step, rows_per_blk):
    n = pe.shape[0]
    nsteps = n // tc

    def full(a):
        return pl.BlockSpec(a.shape, lambda i, *_: (0,) * a.ndim)

    weights = [w["norm_ple"], w["w_ple_gate"], w["w_ple"], w["norm_final"]]
    grid_spec = pltpu.PrefetchScalarGridSpec(
        num_scalar_prefetch=len(tabs),
        grid=(nsteps,),
        in_specs=[
            pl.BlockSpec(memory_space=pl.ANY),
            pl.BlockSpec((tc, D_MODEL), lambda i, *_: (i, 0)),
            pl.BlockSpec((blk_per_step, TOP_K, tc // blk_per_step), lambda i, *_: (i, 0, 0)),
            pl.BlockSpec((blk_per_step, TOP_K, tc // blk_per_step), lambda i, *_: (i, 0, 0)),
            pl.BlockSpec((tc, PLE_DIM), lambda i, *_: (i, 0)),
        ] + [full(a) for a in weights],
        out_specs=pl.BlockSpec((tc, D_MODEL), lambda i, *_: (i, 0)),
        scratch_shapes=[
            pltpu.VMEM((2 * TOP_K * tc * SUBLANES, LANES), F32),
            pltpu.SemaphoreType.DMA((2,)),
        ],
    )
    return pl.pallas_call(
        functools.partial(_combine_body, tc=tc, nsteps=nsteps, nb=nb, g0=g0, blk_per_step=blk_per_step,
                          rows_per_blk=rows_per_blk),
        grid_spec=grid_spec,
        out_shape=jax.ShapeDtypeStruct((n, D_MODEL), F32),
        compiler_params=pltpu.CompilerParams(
            dimension_semantics=("arbitrary",), vmem_limit_bytes=VMEM_LIMIT_BYTES),
        name="combine",
    )(*tabs, ys, x1, slots, gates, pe, *weights)


def _dispatch_tables(cnt, blk_row0, tm, nt):
    nb = cnt.shape[0]
    experts = jnp.arange(N_EXPERTS, dtype=I32)
    pre = jnp.cumsum(cnt, axis=0) - cnt
    total = jnp.sum(cnt, axis=0)
    lst = jnp.cumsum(cnt, axis=1) - cnt
    src = blk_row0[:, None] + lst
    n_tiles_e = (total + tm - 1) // tm
    tile_end = jnp.cumsum(n_tiles_e)
    tile_start = tile_end - n_tiles_e
    tiles = jnp.arange(nt, dtype=I32)
    tile_e_raw = jnp.sum((tiles[:, None] >= tile_end[None, :]).astype(I32), axis=1)
    used = tile_e_raw < N_EXPERTS
    has = total > 0
    last_e = jnp.max(jnp.where(has, experts, 0))
    tile_e = jnp.where(used, jnp.minimum(tile_e_raw, N_EXPERTS - 1), last_e).astype(I32)
    hot = tile_e[:, None] == experts[None, :]

    def per_tile(v):
        if v.ndim == 1:
            return jnp.sum(jnp.where(hot, v[None, :], 0), axis=1)
        return jnp.sum(jnp.where(hot[:, :, None], v[None, :, :], 0), axis=1)

    tile_lo = jnp.where(used, (tiles - per_tile(tile_start)) * tm, 0).astype(I32)
    tile_valid = jnp.where(used, jnp.clip(per_tile(total) - tile_lo, 0, tm), 0).astype(I32)
    p_t = per_tile(pre.T)
    c_t = per_tile(cnt.T)
    meets = (p_t < (tile_lo + tile_valid)[:, None]) & ((p_t + c_t) > tile_lo[:, None]) & (c_t > 0)
    gidx = jnp.arange(nb, dtype=I32)[None, :]
    g_lo = jnp.min(jnp.where(meets, gidx, nb), axis=1).astype(I32)
    g_hi = jnp.max(jnp.where(meets, gidx + 1, 0), axis=1).astype(I32)
    g_lo = jnp.minimum(g_lo, g_hi)
    w_slot_e = (jnp.cumsum(has.astype(I32)) - 1) % 2
    later = has[None, :] & (experts[None, :] > experts[:, None])
    e_next = jnp.min(jnp.where(later, experts[None, :], N_EXPERTS), axis=1)
    e_next = jnp.where(e_next >= N_EXPERTS, -1, e_next)
    flat = lambda a: a.T.reshape(-1).astype(I32)
    return dict(tile_e=tile_e, tile_lo=tile_lo, tile_valid=tile_valid, g_lo=g_lo, g_hi=g_hi,
                w_slot=per_tile(w_slot_e).astype(I32), e_next=per_tile(e_next).astype(I32),
                pre=flat(pre), cnt=flat(cnt), src=flat(src), lst=lst.reshape(-1).astype(I32),
                ys0=(tile_start * tm).astype(I32))


def kernel(x_prompt, x_sample, state_hgrn, cache_conv, p_prompt, p_sample, norm_mix, w_in, hgrn_lb_logits, hgrn_norm, conv_w, conv_b, conv_ln_g, conv_ln_b, w_out, norm_ffn, w_router, b_router, w_gate_up, b_gate_up, w_down, b_down, w_ple, norm_ple, w_ple_gate, norm_final):
    depth = norm_mix.shape[0]
    assert depth == 1
    bp, tp, _ = x_prompt.shape
    bs, ts, _ = x_sample.shape
    n_p, n_s = bp * tp, bs * ts
    n_tok = n_p + n_s

    lb_all = jnp.cumsum(jax.nn.softmax(hgrn_lb_logits.astype(F32), axis=0), axis=0)
    wr_t = w_router[0].T
    wr_hi = wr_t.astype(BF16)
    w = {
        "norm_mix": norm_mix[0][None, :], "w_in": w_in[0].astype(BF16), "lb": lb_all[0][None, :],
        "hgrn_norm": hgrn_norm[0][None, :], "conv_w": conv_w[0], "conv_b": conv_b[0][None, :],
        "ln_g": conv_ln_g[0][None, :], "ln_b": conv_ln_b[0][None, :], "w_out": w_out[0].astype(BF16),
        "norm_ffn": norm_ffn[0][None, :], "wr_hi": wr_hi, "wr_lo": (wr_t - wr_hi.astype(F32)).astype(BF16),
        "b_router": b_router[0][:, None],
        "norm_ple": norm_ple[0][None, :], "w_ple_gate": w_ple_gate[0].astype(BF16),
        "w_ple": w_ple[0].astype(BF16), "norm_final": norm_final[None, :],
    }

    pad = ((0, 0), (CACHE_PAD - CACHE_ROWS, 0), (0, 0))
    s0_p = jnp.zeros((bp, N_HEADS, HEAD_DIM, HEAD_DIM), F32)
    c0_p = jnp.zeros((bp, CACHE_PAD, CONV_CH), F32)
    c0_s = jnp.pad(cache_conv[0], pad)

    tb_p = 256
    x1_p, xs_p, slot_p, gate_p, cnt_p, st_p, cc_p = _mix_stage(x_prompt, s0_p, c0_p, w, nseq=1, tseq=tb_p, chunk=128)
    x1_s, xs_s, slot_s, gate_s, cnt_s, st_s, cc_s = _mix_stage(x_sample, state_hgrn[0], c0_s, w, nseq=bs, tseq=ts,
                                                             chunk=ts)

    nb_p, nb_s = n_p // tb_p, 1
    nb = nb_p + nb_s
    cnt = jnp.concatenate([cnt_p[:, :, 0], cnt_s[:, :, 0]], axis=0)
    blk_row0 = jnp.concatenate([jnp.arange(nb_p, dtype=I32) * (TOP_K * tb_p), jnp.zeros((nb_s,), I32)])
    tm = 3 * MOE_ROWS
    nt = -(-(TOP_K * n_tok) // tm) + N_EXPERTS
    t = _dispatch_tables(cnt, blk_row0, tm, nt)

    zeros = jnp.zeros((MOE_ROWS * SUBLANES, LANES), F32)
    moe_tabs = (t["tile_e"], t["tile_lo"], t["tile_valid"], t["g_lo"], t["g_hi"], t["w_slot"], t["e_next"],
                t["pre"], t["cnt"], t["src"])
    ys = _moe_stage(xs_p, xs_s, zeros, moe_tabs, w_gate_up[0], b_gate_up[0][:, None, :], w_down[0],
                    b_down[0][:, None, :], tm=tm, nt=nt, nb=nb, nb_p=nb_p)

    comb_tabs = (t["pre"], t["cnt"], t["lst"], t["ys0"])
    y_p = _combine_stage(x1_p, ys, slot_p, gate_p, p_prompt[0].reshape(n_p, PLE_DIM),
                         comb_tabs, w, tc=2 * tb_p, nb=nb, g0=0, blk_per_step=2, rows_per_blk=TOP_K * tb_p)
    y_s = _combine_stage(x1_s, ys, slot_s, gate_s, p_sample[0].reshape(n_s, PLE_DIM),
                         comb_tabs, w, tc=n_s, nb=nb, g0=nb_p, blk_per_step=1, rows_per_blk=TOP_K * n_s)

    return (y_p.reshape(bp, tp, D_MODEL), y_s.reshape(bs, ts, D_MODEL),
            st_p[None], cc_p[None], st_s[None], cc_s[None])
```

```python
import functools

import jax
import jax.numpy as jnp
from jax import lax
from jax.experimental import pallas as pl
from jax.experimental.pallas import tpu as pltpu

D_MODEL = 1024
HGRN_WIDTH = 512
HEAD_DIM = 128
N_HEADS = HGRN_WIDTH // HEAD_DIM
CONV_CH = 512
CONV_WIDTH = 31
CACHE_ROWS = CONV_WIDTH - 1
CACHE_PAD = 32
IN_COLS = 4 * HGRN_WIDTH + 2 * CONV_CH
N_EXPERTS = 32
TOP_K = 4
D_FF = 1024
SWIGLU_LIMIT = 7.0
SWIGLU_ALPHA = 1.702
PLE_DIM = 256
EPS = 1e-6

SUBLANES = 8
LANES = 128
MOE_ROWS = 128
MOE_TILE = 6 * MOE_ROWS
VMEM_LIMIT_BYTES = 56 * 1024 * 1024

F32 = jnp.float32
BF16 = jnp.bfloat16
I32 = jnp.int32


def _split3(a):
    p1 = a.astype(BF16)
    r1 = a - p1.astype(F32)
    p2 = r1.astype(BF16)
    r2 = r1 - p2.astype(F32)
    return p1, p2, r2.astype(BF16)


def _dot(a, b):
    return jnp.dot(a, b, preferred_element_type=F32)


def _dot_nt(a, b):
    return lax.dot_general(a, b, (((1,), (1,)), ((), ())), preferred_element_type=F32)


def _dot_tn(a, b):
    return lax.dot_general(a, b, (((0,), (0,)), ((), ())), preferred_element_type=F32)


def _rms(x, g):
    return x * lax.rsqrt(jnp.mean(x * x, axis=-1, keepdims=True) + EPS) * g


def _copy_rows(src_hbm, src_row, dst_vmem, dst_row, n, sem):
    @pl.when(n > 0)
    def _():
        size = pl.multiple_of(n * SUBLANES, SUBLANES)
        pltpu.make_async_copy(src_hbm.at[pl.ds(pl.multiple_of(src_row * SUBLANES, SUBLANES), size)],
                              dst_vmem.at[pl.ds(pl.multiple_of(dst_row * SUBLANES, SUBLANES), size)], sem).start()


def _to_row_tiles(ref, x, base=0):
    rows = x.shape[0]
    for s in range(D_MODEL // LANES):
        ref[pl.ds(base * SUBLANES + s, rows, stride=SUBLANES), :] = x[:, s * LANES:(s + 1) * LANES]


def _from_row_tiles(ref, rows, base=0, dtype=BF16):
    return jnp.concatenate(
        [ref[pl.ds(base * SUBLANES + s, rows, stride=SUBLANES), :].astype(dtype) for s in range(D_MODEL // LANES)],
        axis=1)


def _hgrn_chunk(q, kk, f, b, v, st, shift_ref, ones_sq, pair_ref):
    c = q.shape[0]
    sc = None
    n_mask = 0
    m = c // 2
    while m >= SUBLANES:
        pieces = []
        for blk in range(c // (2 * m)):
            r = blk * 2 * m + m - 1
            pieces.append(jnp.broadcast_to(b[r:r + 1, :], (2 * m, HEAD_DIM)))
        bmid = pieces[0] if len(pieces) == 1 else jnp.concatenate(pieces, axis=0)
        e_m = jnp.exp(-jnp.abs(b - bmid))
        qh = (q * e_m).astype(BF16)
        kh = (kk * e_m).astype(BF16)
        term = pair_ref[n_mask] * _dot_nt(qh, kh)
        sc = term if sc is None else sc + term
        n_mask += 1
        m //= 2

    shift_ref[0, SUBLANES:SUBLANES + c, :] = f
    shift_ref[1, SUBLANES:SUBLANES + c, :] = kk
    decay = None
    for d in range(SUBLANES):
        if d == 0:
            kd = kk
        else:
            f_sh = shift_ref[0, SUBLANES - (d - 1):SUBLANES - (d - 1) + c, :]
            decay = f_sh if decay is None else decay * f_sh
            kd = shift_ref[1, SUBLANES - d:SUBLANES - d + c, :] * decay
        s_d = _dot((q * kd).astype(BF16), ones_sq)
        term = pair_ref[n_mask + d] * s_d[:, :c]
        sc = term if sc is None else sc + term

    o = _dot(sc.astype(BF16), v.astype(BF16))
    o = o + _dot_nt((q * jnp.exp(b)).astype(BF16), st.astype(BF16))
    b_last = b[c - 1:c, :]
    kdec = (kk * jnp.exp(b_last - b)).astype(BF16)
    st_new = st * jnp.exp(b_last) + _dot_tn(v.astype(BF16), kdec)
    return o, st_new


def _mix_body(x_ref, s0_ref, c0_ref, nm_ref, win_ref, lb_ref, hn_ref, cw_ref, cb_ref, lng_ref,
              lnb_ref, wout_ref, nf_ref, wrh_ref, wrl_ref, br_ref, tri_ref, before_ref, lower_ref, ones_ref, pair_ref,
              x1_ref, xs_ref, slot_ref, gate_ref, cnt_ref, sout_ref, cout_ref,
              st_scr, upad_scr, ush_scr, shift_scr, mix_scr, carry_scr, *, tb, chunk, nj, nblk, nseq):
    pipelined = nblk > 1
    tseq = tb // nseq
    s = pl.program_id(0)
    j = lax.rem(jnp.minimum(s, nblk - 1), nj)

    if pipelined:
        @pl.when(s == 0)
        def _():
            carry_scr[...] = jnp.zeros((2, tb, D_MODEL), BF16)

    @pl.when(j == 0)
    def _init():
        for q in range(nseq):
            for h in range(N_HEADS):
                st_scr[q * N_HEADS + h] = s0_ref[q, h].T
            upad_scr[q, 0:CACHE_PAD, :] = c0_ref[q]
        shift_scr[:, 0:SUBLANES, :] = jnp.zeros((2, SUBLANES, HEAD_DIM), F32)

    def router(xh, xl):
        return _route_block(xh, xl, wrh_ref, wrl_ref, br_ref, before_ref, lower_ref,
                            xs_ref, slot_ref, gate_ref, cnt_ref, tb=tb)

    route = router(carry_scr[0], carry_scr[1]) if pipelined else iter(())

    def advance_route(n=1):
        for _ in range(n):
            next(route, None)

    advance_route()
    x = x_ref[...].reshape(tb, D_MODEL)
    h_in = _rms(x, nm_ref[...]).astype(BF16)
    proj = _dot(h_in, win_ref[...])
    advance_route(2)

    q_all = proj[:, 0:HGRN_WIDTH]
    fz = proj[:, HGRN_WIDTH:2 * HGRN_WIDTH]
    v_all = proj[:, 2 * HGRN_WIDTH:3 * HGRN_WIDTH]
    g_all = proj[:, 3 * HGRN_WIDTH:4 * HGRN_WIDTH]
    lb = lb_ref[...]
    f_all = lb + (1.0 - lb) * jax.nn.sigmoid(fz)
    lf_all = jnp.log(f_all)
    kk_all = 1.0 - f_all

    tri = tri_ref[...]
    p1, p2, p3 = _split3(lf_all)
    b_all = _dot(tri, p1) + _dot(tri, p2) + _dot(tri, p3)
    advance_route(2)

    ones_sq = ones_ref[...]
    hn = hn_ref[...]
    per_seq = tseq // chunk
    for h in range(N_HEADS):
        ls = slice(h * HEAD_DIM, (h + 1) * HEAD_DIM)
        for ci in range(tb // chunk):
            q, cq = divmod(ci, per_seq)
            if cq == 0:
                st = st_scr[q * N_HEADS + h]
            rs = slice(ci * chunk, (ci + 1) * chunk)
            o, st = _hgrn_chunk(q_all[rs, ls], kk_all[rs, ls], f_all[rs, ls], b_all[rs, ls],
                                v_all[rs, ls], st, shift_scr, ones_sq, pair_ref)
            o = _rms(o, hn[:, ls])
            mix_scr[rs, ls] = (o * jax.nn.silu(g_all[rs, ls])).astype(BF16)
            if cq == per_seq - 1:
                st_scr[q * N_HEADS + h] = st
            advance_route()

    u = proj[:, 4 * HGRN_WIDTH:4 * HGRN_WIDTH + CONV_CH] * jax.nn.sigmoid(
        proj[:, 4 * HGRN_WIDTH + CONV_CH:IN_COLS])
    first = CACHE_PAD - CACHE_ROWS
    span = tseq + CACHE_PAD - SUBLANES
    accs = []
    for q in range(nseq):
        upad = upad_scr.at[q]
        upad[CACHE_PAD:CACHE_PAD + tseq, :] = u[q * tseq:(q + 1) * tseq, :]
        for r in range(1, SUBLANES):
            ush_scr[r, 0:span, :] = upad[r:r + span, :]
        acc = jnp.zeros((tseq, CONV_CH), F32) + cb_ref[...]
        for tap in range(CONV_WIDTH):
            a, r = divmod(first + tap, SUBLANES)
            rows = slice(a * SUBLANES, a * SUBLANES + tseq)
            shifted = upad[rows, :] if r == 0 else ush_scr[r, rows, :]
            acc = acc + cw_ref[tap:tap + 1, :] * shifted
        accs.append(acc)
        new_cache = upad[tseq:tseq + CACHE_PAD, :]
        upad[0:CACHE_PAD, :] = new_cache
    acc = accs[0] if nseq == 1 else jnp.concatenate(accs, axis=0)
    mu = jnp.mean(acc, axis=-1, keepdims=True)
    cen = acc - mu
    var = jnp.mean(cen * cen, axis=-1, keepdims=True)
    cn = cen * lax.rsqrt(var + EPS) * lng_ref[...] + lnb_ref[...]
    mix_scr[:, HGRN_WIDTH:HGRN_WIDTH + CONV_CH] = jax.nn.silu(cn).astype(BF16)
    for _ in route:
        pass

    x1 = x + _dot(mix_scr[...], wout_ref[...])
    x1_ref[...] = x1
    xn = _rms(x1, nf_ref[...])
    xh = xn.astype(BF16)
    xl = (xn - xh.astype(F32)).astype(BF16)
    if pipelined:
        carry_scr[0] = xh
        carry_scr[1] = xl
    else:
        for _ in router(xh, xl):
            pass

    @pl.when((s < nblk) & (j == nj - 1))
    def _fin():
        for q in range(nseq):
            for h in range(N_HEADS):
                sout_ref[q, h] = st_scr[q * N_HEADS + h].T
            cout_ref[q] = upad_scr[q, tseq + first:tseq + CACHE_PAD, :]


def _route_block(xh, xl, wrh_ref, wrl_ref, br_ref, before_ref, lower_ref, xs_ref, slot_ref, gate_ref, cnt_ref, *, tb):
    wrh = wrh_ref[...]
    logits = _dot_nt(wrh, xh) + _dot_nt(wrh, xl) + _dot_nt(wrl_ref[...], xh) + br_ref[...]
    yield

    eidx = lax.broadcasted_iota(I32, (N_EXPERTS, tb), 0).astype(F32)
    vals, hots = [], []
    work = logits
    for _ in range(TOP_K):
        mx = jnp.max(work, axis=0, keepdims=True)
        pick = jnp.min(jnp.where(work == mx, eidx, float(N_EXPERTS)), axis=0, keepdims=True)
        hot = eidx == pick
        vals.append(mx)
        hots.append(hot)
        work = jnp.where(hot, -jnp.inf, work)
        yield
    exps = [jnp.exp(val - vals[0]) for val in vals]
    inv = 1.0 / (exps[0] + exps[1] + exps[2] + exps[3])
    gate_ref[0] = jnp.concatenate([e * inv for e in exps], axis=0)

    before = before_ref[...]
    lower = lower_ref[...]
    hot_f = [hot.astype(F32) for hot in hots]
    tots = [jnp.sum(hf, axis=1, keepdims=True) for hf in hot_f]
    cnt = tots[0] + tots[1] + tots[2] + tots[3]
    cnt_ref[0] = cnt.astype(I32)
    start = _dot(lower, jnp.broadcast_to(cnt, (N_EXPERTS, LANES)).astype(BF16))[:, 0:1]
    yield
    slots = []
    base = start
    for k in range(TOP_K):
        pre = _dot(hot_f[k].astype(BF16), before)
        slots.append(jnp.sum(hot_f[k] * (base + pre), axis=0, keepdims=True))
        base = base + tots[k]
        yield
    slot_ref[0] = jnp.concatenate(slots, axis=0).astype(I32)

    srow = lax.broadcasted_iota(I32, (TOP_K * tb, tb), 0).astype(F32)
    perm = (srow == slots[0])
    for k in range(1, TOP_K):
        perm = perm | (srow == slots[k])
    yield
    xs = _dot(perm.astype(BF16), xh)
    yield
    _to_row_tiles(xs_ref, xs)


def _mix_stage(x, s0, c0, w, *, nseq, tseq, chunk):
    bsz, t, _ = x.shape
    tb = nseq * tseq
    nj = t // tseq
    nblk = (bsz // nseq) * nj
    pipelined = nblk > 1
    n_steps = nblk + 1 if pipelined else 1

    def full(a):
        return pl.BlockSpec(a.shape, lambda s: (0,) * a.ndim)

    def cur(s):
        sc = jnp.minimum(s, nblk - 1)
        return sc // nj, sc % nj

    def prev(s):
        return jnp.maximum(s - 1, 0) if pipelined else s

    weights = [w["norm_mix"], w["w_in"], w["lb"], w["hgrn_norm"], w["conv_w"], w["conv_b"], w["ln_g"],
               w["ln_b"], w["w_out"], w["norm_ffn"], w["wr_hi"], w["wr_lo"], w["b_router"]]
    r_t = lax.broadcasted_iota(I32, (tb, tb), 0)
    c_t = lax.broadcasted_iota(I32, (tb, tb), 1)
    r_e = lax.broadcasted_iota(I32, (N_EXPERTS, N_EXPERTS), 0)
    c_e = lax.broadcasted_iota(I32, (N_EXPERTS, N_EXPERTS), 1)
    weights += [((r_t >= c_t) & ((r_t ^ c_t) < chunk)).astype(BF16), (r_t < c_t).astype(BF16),
                (c_e < r_e).astype(BF16), jnp.ones((HEAD_DIM, HEAD_DIM), BF16)]
    r_c = lax.broadcasted_iota(I32, (chunk, chunk), 0)
    c_c = lax.broadcasted_iota(I32, (chunk, chunk), 1)
    x_c = r_c ^ c_c
    pair = []
    m = chunk // 2
    while m >= SUBLANES:
        pair.append((x_c >= m) & (x_c < 2 * m) & (r_c > c_c))
        m //= 2
    pair += [(x_c < SUBLANES) & (r_c - c_c == d) for d in range(SUBLANES)]
    weights.append(jnp.stack(pair).astype(F32))
    in_specs = [
        pl.BlockSpec((nseq, tseq, D_MODEL), lambda s: (*cur(s), 0)),
        pl.BlockSpec((nseq, N_HEADS, HEAD_DIM, HEAD_DIM), lambda s: (cur(s)[0], 0, 0, 0)),
        pl.BlockSpec((nseq, CACHE_PAD, CONV_CH), lambda s: (cur(s)[0], 0, 0)),
    ] + [full(a) for a in weights]
    out_shape = (
        jax.ShapeDtypeStruct((n_steps * tb, D_MODEL), F32),
        jax.ShapeDtypeStruct((TOP_K * nblk * tb * SUBLANES, LANES), F32),
        jax.ShapeDtypeStruct((nblk, TOP_K, tb), I32),
        jax.ShapeDtypeStruct((nblk, TOP_K, tb), F32),
        jax.ShapeDtypeStruct((nblk, N_EXPERTS, 1), I32),
        jax.ShapeDtypeStruct((bsz, N_HEADS, HEAD_DIM, HEAD_DIM), F32),
        jax.ShapeDtypeStruct((bsz, CACHE_ROWS, CONV_CH), F32),
    )
    out_specs = (
        pl.BlockSpec((tb, D_MODEL), lambda s: (s, 0)),
        pl.BlockSpec((TOP_K * tb * SUBLANES, LANES), lambda s: (prev(s), 0)),
        pl.BlockSpec((1, TOP_K, tb), lambda s: (prev(s), 0, 0)),
        pl.BlockSpec((1, TOP_K, tb), lambda s: (prev(s), 0, 0)),
        pl.BlockSpec((1, N_EXPERTS, 1), lambda s: (prev(s), 0, 0)),
        pl.BlockSpec((nseq, N_HEADS, HEAD_DIM, HEAD_DIM), lambda s: (cur(s)[0], 0, 0, 0)),
        pl.BlockSpec((nseq, CACHE_ROWS, CONV_CH), lambda s: (cur(s)[0], 0, 0)),
    )
    return pl.pallas_call(
        functools.partial(_mix_body, tb=tb, chunk=chunk, nj=nj, nblk=nblk, nseq=nseq),
        grid=(n_steps,),
        in_specs=in_specs,
        out_specs=out_specs,
        out_shape=out_shape,
        scratch_shapes=[
            pltpu.VMEM((nseq * N_HEADS, HEAD_DIM, HEAD_DIM), F32),
            pltpu.VMEM((nseq, CACHE_PAD + tseq, CONV_CH), F32),
            pltpu.VMEM((SUBLANES, CACHE_PAD + tseq - SUBLANES, CONV_CH), F32),
            pltpu.VMEM((2, SUBLANES + chunk, HEAD_DIM), F32),
            pltpu.VMEM((tb, D_MODEL), BF16),
            pltpu.VMEM((2, tb, D_MODEL), BF16),
        ],
        compiler_params=pltpu.CompilerParams(
            dimension_semantics=("arbitrary",), vmem_limit_bytes=VMEM_LIMIT_BYTES,
            allow_input_fusion=[False] * 3 + [True] * len(weights)),
        name="mix",
    )(x, s0, c0, *weights)


def _moe_body(te_ref, tlo_ref, tv_ref, glo_ref, ghi_ref, wslot_ref, enext_ref, pre_ref, cnt_ref, src_ref,
              xs_p_hbm, xs_s_hbm, zeros_hbm, wgu_hbm, bgu_ref, wd_hbm, bd_ref, y_ref,
              xbuf, wgu_f, wd_f, wgu_b, wd_b, gsem, wsem, *, tm, nt, nb, nb_p):
    i = pl.program_id(0)
    slot = lax.rem(i, 2)

    def issue_tile(t, s):
        e = te_ref[t]
        lo = tlo_ref[t]
        nv = tv_ref[t]
        hi = lo + nv
        dst0 = s * tm

        def runs(xs_hbm):
            def body(g, carry):
                p0 = pre_ref[e * nb + g]
                a = jnp.maximum(p0, lo)
                b = jnp.minimum(p0 + cnt_ref[e * nb + g], hi)
                _copy_rows(xs_hbm, src_ref[e * nb + g] + (a - p0), xbuf, dst0 + (a - lo), b - a, gsem.at[s])
                return carry
            return body
        g_lo, g_hi = glo_ref[t], ghi_ref[t]
        lax.fori_loop(g_lo, jnp.minimum(g_hi, nb_p), runs(xs_p_hbm), 0)
        lax.fori_loop(jnp.maximum(g_lo, nb_p), g_hi, runs(xs_s_hbm), 0)
        _copy_rows(zeros_hbm, 0, xbuf, dst0 + nv, (-nv) & (MOE_ROWS - 1), gsem.at[s])

    def weight_fetch(e, ws):
        return (pltpu.make_async_copy(wgu_hbm.at[e], wgu_f.at[ws], wsem.at[ws]),
                pltpu.make_async_copy(wd_hbm.at[e], wd_f.at[ws], wsem.at[ws]))

    @pl.when(i == 0)
    def _prologue():
        issue_tile(0, 0)
        for c in weight_fetch(te_ref[0], wslot_ref[0]):
            c.start()

    @pl.when(i + 1 < nt)
    def _prefetch():
        issue_tile(i + 1, 1 - slot)

    used = tv_ref[i] > 0
    new_expert = (i == 0) | (te_ref[i] != te_ref[jnp.maximum(i - 1, 0)])

    @pl.when(used & new_expert)
    def _switch_expert():
        ws = wslot_ref[i]
        for c in weight_fetch(0, ws):
            c.wait()
        nxt = enext_ref[i]

        @pl.when(nxt >= 0)
        def _():
            for c in weight_fetch(nxt, 1 - ws):
                c.start(priority=1)
        rows = 128

        def cast_gu(c, carry):
            r0 = pl.multiple_of(c * rows, rows)
            wgu_b[pl.ds(r0, rows), :] = wgu_f[ws, pl.ds(r0, rows), :].astype(BF16)
            return carry
        lax.fori_loop(0, D_MODEL // rows, cast_gu, 0)

        def cast_d(c, carry):
            r0 = pl.multiple_of(c * rows, rows)
            wd_b[pl.ds(r0, rows), :] = wd_f[ws, pl.ds(r0, rows), :].astype(BF16)
            return carry
        lax.fori_loop(0, D_FF // rows, cast_d, 0)

    def expert_ffn(rows):
        got = pl.ds(pl.multiple_of(slot * tm * SUBLANES, tm * SUBLANES), rows * SUBLANES)
        pltpu.make_async_copy(xs_p_hbm.at[pl.ds(0, rows * SUBLANES)], xbuf.at[got], gsem.at[slot]).wait()
        xb = _from_row_tiles(xbuf, rows, base=slot * tm)
        gu = _dot(xb, wgu_b[...]) + bgu_ref[...]
        gate = jnp.minimum(gu[:, :D_FF], SWIGLU_LIMIT)
        up = jnp.clip(gu[:, D_FF:], -SWIGLU_LIMIT, SWIGLU_LIMIT)
        hid = (up + 1.0) * (gate * jax.nn.sigmoid(SWIGLU_ALPHA * gate))
        _to_row_tiles(y_ref, _dot(hid.astype(BF16), wd_b[...]) + bd_ref[...])

    nv_i = tv_ref[i]
    sizes = tuple(range(MOE_ROWS, tm + 1, MOE_ROWS))
    for lo, hi in zip((0,) + sizes[:-1], sizes):
        @pl.when((nv_i > lo) & (nv_i <= hi))
        def _(hi=hi):
            expert_ffn(hi)
            if hi < tm:
                y_ref[hi * SUBLANES:tm * SUBLANES, :] = jnp.zeros(((tm - hi) * SUBLANES, LANES), F32)

    @pl.when(jnp.logical_not(used))
    def _unused_tile():
        y_ref[...] = jnp.zeros((tm * SUBLANES, LANES), F32)


def _moe_stage(xs_p, xs_s, zeros, tabs, w_gu, b_gu, w_d, b_d, *, tm, nt, nb, nb_p):
    n_pref = len(tabs)

    def wmap(i, te, *_):
        return (te[i], 0, 0)

    grid_spec = pltpu.PrefetchScalarGridSpec(
        num_scalar_prefetch=n_pref,
        grid=(nt,),
        in_specs=[
            pl.BlockSpec(memory_space=pl.ANY),
            pl.BlockSpec(memory_space=pl.ANY),
            pl.BlockSpec(memory_space=pl.ANY),
            pl.BlockSpec(memory_space=pl.ANY),
            pl.BlockSpec((None, 1, 2 * D_FF), wmap),
            pl.BlockSpec(memory_space=pl.ANY),
            pl.BlockSpec((None, 1, D_MODEL), wmap),
        ],
        out_specs=pl.BlockSpec((tm * SUBLANES, LANES), lambda i, *_: (i, 0)),
        scratch_shapes=[
            pltpu.VMEM((2 * tm * SUBLANES, LANES), F32),
            pltpu.VMEM((2, D_MODEL, 2 * D_FF), F32),
            pltpu.VMEM((2, D_FF, D_MODEL), F32),
            pltpu.VMEM((D_MODEL, 2 * D_FF), BF16),
            pltpu.VMEM((D_FF, D_MODEL), BF16),
            pltpu.SemaphoreType.DMA((2,)),
            pltpu.SemaphoreType.DMA((2,)),
        ],
    )
    return pl.pallas_call(
        functools.partial(_moe_body, tm=tm, nt=nt, nb=nb, nb_p=nb_p),
        grid_spec=grid_spec,
        out_shape=jax.ShapeDtypeStruct((nt * tm * SUBLANES, LANES), F32),
        compiler_params=pltpu.CompilerParams(
            dimension_semantics=("arbitrary",), vmem_limit_bytes=VMEM_LIMIT_BYTES),
        name="moe",
    )(*tabs, xs_p, xs_s, zeros, w_gu, b_gu, w_d, b_d)


def _combine_body(pre_ref, cnt_ref, lst_ref, ys0_ref, y_hbm, x1_ref, slot_ref, g_ref, pe_ref, np_ref, wpg_ref,
                  wple_ref, nfin_ref, out_ref, stag, sem, *, tc, nsteps, nb, g0, blk_per_step, rows_per_blk):
    i = pl.program_id(0)
    slot = lax.rem(i, 2)
    rows = TOP_K * tc

    def issue_step(step, s):
        for bi in range(blk_per_step):
            g = g0 + step * blk_per_step + bi
            dst0 = s * rows + bi * rows_per_blk

            def body(e, carry):
                src = ys0_ref[e] + pre_ref[e * nb + g]
                _copy_rows(y_hbm, src, stag, dst0 + lst_ref[g * N_EXPERTS + e], cnt_ref[e * nb + g], sem.at[s])
                return carry
            lax.fori_loop(0, N_EXPERTS, body, 0)

    @pl.when(i == 0)
    def _prologue():
        issue_step(0, 0)

    @pl.when(i + 1 < nsteps)
    def _prefetch():
        issue_step(i + 1, 1 - slot)

    tile = pl.ds(pl.multiple_of(slot * rows * SUBLANES, rows * SUBLANES), rows * SUBLANES)
    pltpu.make_async_copy(y_hbm.at[pl.ds(0, rows * SUBLANES)], stag.at[tile], sem.at[slot]).wait()

    tb = tc // blk_per_step
    srow = lax.broadcasted_iota(I32, (rows_per_blk, tb), 0)
    moe_out = []
    for bi in range(blk_per_step):
        sl = slot_ref[bi]
        g = g_ref[bi]
        back_t = jnp.zeros((rows_per_blk, tb), F32)
        for k in range(TOP_K):
            back_t = jnp.where(srow == sl[k:k + 1, :], g[k:k + 1, :], back_t)
        staged = _from_row_tiles(stag, rows_per_blk, base=slot * rows + bi * rows_per_blk)
        moe_out.append(_dot_tn(back_t.astype(BF16), staged))
    x2 = x1_ref[...] + (moe_out[0] if blk_per_step == 1 else jnp.concatenate(moe_out, axis=0))
    hp = _rms(x2, np_ref[...]).astype(BF16)
    gate = jax.nn.sigmoid(_dot(hp, wpg_ref[...]))
    emb = _dot(pe_ref[...].astype(BF16), wple_ref[...])
    x3 = x2 + gate * emb
    out_ref[...] = _rms(x3, nfin_ref[...])


def _combine_stage(x1, ys, slots, gates, pe, tabs, w, *, tc, nb, g0, blk_per_step, rows_per_blk):
    n = pe.shape[0]
    nsteps = n // tc

    def full(a):
        return pl.BlockSpec(a.shape, lambda i, *_: (0,) * a.ndim)

    weights = [w["norm_ple"], w["w_ple_gate"], w["w_ple"], w["norm_final"]]
    grid_spec = pltpu.PrefetchScalarGridSpec(
        num_scalar_prefetch=len(tabs),
        grid=(nsteps,),
        in_specs=[
            pl.BlockSpec(memory_space=pl.ANY),
            pl.BlockSpec((tc, D_MODEL), lambda i, *_: (i, 0)),
            pl.BlockSpec((blk_per_step, TOP_K, tc // blk_per_step), lambda i, *_: (i, 0, 0)),
            pl.BlockSpec((blk_per_step, TOP_K, tc // blk_per_step), lambda i, *_: (i, 0, 0)),
            pl.BlockSpec((tc, PLE_DIM), lambda i, *_: (i, 0)),
        ] + [full(a) for a in weights],
        out_specs=pl.BlockSpec((tc, D_MODEL), lambda i, *_: (i, 0)),
        scratch_shapes=[
            pltpu.VMEM((2 * TOP_K * tc * SUBLANES, LANES), F32),
            pltpu.SemaphoreType.DMA((2,)),
        ],
    )
    return pl.pallas_call(
        functools.partial(_combine_body, tc=tc, nsteps=nsteps, nb=nb, g0=g0, blk_per_step=blk_per_step,
                          rows_per_blk=rows_per_blk),
        grid_spec=grid_spec,
        out_shape=jax.ShapeDtypeStruct((n, D_MODEL), F32),
        compiler_params=pltpu.CompilerParams(
            dimension_semantics=("arbitrary",), vmem_limit_bytes=VMEM_LIMIT_BYTES,
            allow_input_fusion=[False] * (len(tabs) + 5) + [True] * len(weights)),
        name="combine",
    )(*tabs, ys, x1, slots, gates, pe, *weights)


def _dispatch_tables(cnt, blk_row0, tm, nt):
    nb = cnt.shape[0]
    experts = jnp.arange(N_EXPERTS, dtype=I32)
    pre = jnp.cumsum(cnt, axis=0) - cnt
    total = jnp.sum(cnt, axis=0)
    lst = jnp.cumsum(cnt, axis=1) - cnt
    src = blk_row0[:, None] + lst
    n_tiles_e = (total + tm - 1) // tm
    tile_end = jnp.cumsum(n_tiles_e)
    tile_start = tile_end - n_tiles_e
    tiles = jnp.arange(nt, dtype=I32)
    tile_e_raw = jnp.sum((tiles[:, None] >= tile_end[None, :]).astype(I32), axis=1)
    used = tile_e_raw < N_EXPERTS
    has = total > 0
    last_e = jnp.max(jnp.where(has, experts, 0))
    tile_e = jnp.where(used, jnp.minimum(tile_e_raw, N_EXPERTS - 1), last_e).astype(I32)
    hot = tile_e[:, None] == experts[None, :]

    def per_tile(v):
        if v.ndim == 1:
            return jnp.sum(jnp.where(hot, v[None, :], 0), axis=1)
        return jnp.sum(jnp.where(hot[:, :, None], v[None, :, :], 0), axis=1)

    tile_lo = jnp.where(used, (tiles - per_tile(tile_start)) * tm, 0).astype(I32)
    tile_valid = jnp.where(used, jnp.clip(per_tile(total) - tile_lo, 0, tm), 0).astype(I32)
    p_t = per_tile(pre.T)
    c_t = per_tile(cnt.T)
    meets = (p_t < (tile_lo + tile_valid)[:, None]) & ((p_t + c_t) > tile_lo[:, None]) & (c_t > 0)
    gidx = jnp.arange(nb, dtype=I32)[None, :]
    g_lo = jnp.min(jnp.where(meets, gidx, nb), axis=1).astype(I32)
    g_hi = jnp.max(jnp.where(meets, gidx + 1, 0), axis=1).astype(I32)
    g_lo = jnp.minimum(g_lo, g_hi)
    w_slot_e = (jnp.cumsum(has.astype(I32)) - 1) % 2
    later = has[None, :] & (experts[None, :] > experts[:, None])
    e_next = jnp.min(jnp.where(later, experts[None, :], N_EXPERTS), axis=1)
    e_next = jnp.where(e_next >= N_EXPERTS, -1, e_next)
    flat = lambda a: a.T.reshape(-1).astype(I32)
    return dict(tile_e=tile_e, tile_lo=tile_lo, tile_valid=tile_valid, g_lo=g_lo, g_hi=g_hi,
                w_slot=per_tile(w_slot_e).astype(I32), e_next=per_tile(e_next).astype(I32),
                pre=flat(pre), cnt=flat(cnt), src=flat(src), lst=lst.reshape(-1).astype(I32),
                ys0=(tile_start * tm).astype(I32))


def kernel(x_prompt, x_sample, state_hgrn, cache_conv, p_prompt, p_sample, norm_mix, w_in, hgrn_lb_logits, hgrn_norm, conv_w, conv_b, conv_ln_g, conv_ln_b, w_out, norm_ffn, w_router, b_router, w_gate_up, b_gate_up, w_down, b_down, w_ple, norm_ple, w_ple_gate, norm_final):
    depth = norm_mix.shape[0]
    assert depth == 1
    bp, tp, _ = x_prompt.shape
    bs, ts, _ = x_sample.shape
    n_p, n_s = bp * tp, bs * ts
    n_tok = n_p + n_s

    lb_all = jnp.cumsum(jax.nn.softmax(hgrn_lb_logits.astype(F32), axis=0), axis=0)
    wr_t = w_router[0].T
    wr_hi = wr_t.astype(BF16)
    w = {
        "norm_mix": norm_mix[0][None, :], "w_in": w_in[0].astype(BF16), "lb": lb_all[0][None, :],
        "hgrn_norm": hgrn_norm[0][None, :], "conv_w": conv_w[0], "conv_b": conv_b[0][None, :],
        "ln_g": conv_ln_g[0][None, :], "ln_b": conv_ln_b[0][None, :], "w_out": w_out[0].astype(BF16),
        "norm_ffn": norm_ffn[0][None, :], "wr_hi": wr_hi, "wr_lo": (wr_t - wr_hi.astype(F32)).astype(BF16),
        "b_router": b_router[0][:, None],
        "norm_ple": norm_ple[0][None, :], "w_ple_gate": w_ple_gate[0].astype(BF16),
        "w_ple": w_ple[0].astype(BF16), "norm_final": norm_final[None, :],
    }

    pad = ((0, 0), (CACHE_PAD - CACHE_ROWS, 0), (0, 0))
    s0_p = jnp.zeros((bp, N_HEADS, HEAD_DIM, HEAD_DIM), F32)
    c0_p = jnp.zeros((bp, CACHE_PAD, CONV_CH), F32)
    c0_s = jnp.pad(cache_conv[0], pad)

    tb_p = 256
    x1_p, xs_p, slot_p, gate_p, cnt_p, st_p, cc_p = _mix_stage(x_prompt, s0_p, c0_p, w, nseq=1, tseq=tb_p, chunk=128)
    x1_s, xs_s, slot_s, gate_s, cnt_s, st_s, cc_s = _mix_stage(x_sample, state_hgrn[0], c0_s, w, nseq=bs, tseq=ts,
                                                             chunk=ts)

    nb_p, nb_s = n_p // tb_p, 1
    nb = nb_p + nb_s
    cnt = jnp.concatenate([cnt_p[:, :, 0], cnt_s[:, :, 0]], axis=0)
    blk_row0 = jnp.concatenate([jnp.arange(nb_p, dtype=I32) * (TOP_K * tb_p), jnp.zeros((nb_s,), I32)])
    tm = MOE_TILE
    nt = -(-(TOP_K * n_tok) // tm) + N_EXPERTS
    t = _dispatch_tables(cnt, blk_row0, tm, nt)

    zeros = jnp.zeros((MOE_ROWS * SUBLANES, LANES), F32)
    moe_tabs = (t["tile_e"], t["tile_lo"], t["tile_valid"], t["g_lo"], t["g_hi"], t["w_slot"], t["e_next"],
                t["pre"], t["cnt"], t["src"])
    ys = _moe_stage(xs_p, xs_s, zeros, moe_tabs, w_gate_up[0], b_gate_up[0][:, None, :], w_down[0],
                    b_down[0][:, None, :], tm=tm, nt=nt, nb=nb, nb_p=nb_p)

    comb_tabs = (t["pre"], t["cnt"], t["lst"], t["ys0"])
    y_p = _combine_stage(x1_p, ys, slot_p, gate_p, p_prompt[0].reshape(n_p, PLE_DIM),
                         comb_tabs, w, tc=2 * tb_p, nb=nb, g0=0, blk_per_step=2, rows_per_blk=TOP_K * tb_p)
    y_s = _combine_stage(x1_s, ys, slot_s, gate_s, p_sample[0].reshape(n_s, PLE_DIM),
                         comb_tabs, w, tc=n_s, nb=nb, g0=nb_p, blk_per_step=1, rows_per_blk=TOP_K * n_s)

    return (y_p.reshape(bp, tp, D_MODEL), y_s.reshape(bs, ts, D_MODEL),
            st_p[None], cc_p[None], st_s[None], cc_s[None])
```
